```python
import math, functools
import jax, jax.numpy as jnp
from jax import lax
import numpy as np

D_MODEL = 2048
BATCH = 4
SEQ = 2048
DEPTH = 4
DEC_BATCH = 8
DEC_SEQ = 8
PAST_LEN = 16384
PAGE_SIZE = 128

MIX = D_MODEL
HEAD_DIM = 64
ATT_WIDTH = MIX // 2
N_HEADS = ATT_WIDTH // HEAD_DIM
SSM_WIDTH = MIX // 4
SSM_GROUP = 16
N_SSM_GROUPS = SSM_WIDTH // SSM_GROUP
SSM_STATE = 64
CONV_CH = MIX // 4
CONV_K = 3
ATT_PATTERNS = ((128, 1), (512, 4), (2048, 16))
ATT_SPAN = 128
ATT_BLOCK = 128
ATT_MAX_WINDOW = 2048
D_FF = -(-8 * D_MODEL // (3 * 256)) * 256
IN_COLS = 3 * ATT_WIDTH + SSM_WIDTH + 3 * CONV_CH
SPLITS = (ATT_WIDTH, 2 * ATT_WIDTH, 3 * ATT_WIDTH, 3 * ATT_WIDTH + SSM_WIDTH,
          3 * ATT_WIDTH + SSM_WIDTH + CONV_CH, 3 * ATT_WIDTH + SSM_WIDTH + 2 * CONV_CH)
DT_MIN = 0.001
DT_MAX = 0.1
RMS_EPS = 1e-6
NEG = -1e30

kernel_name = 'hymba_s5_shortconv_dilated_swa_step'


def rmsnorm(x, g):
    xf = x.astype(jnp.float32)
    y = xf * lax.rsqrt(jnp.mean(xf * xf, axis=-1, keepdims=True) + RMS_EPS)
    return (y * g.astype(jnp.float32)).astype(x.dtype)


def to_streams(t, d):
    b, s = t.shape[:2]
    t = t.reshape((b, s // d, d) + t.shape[2:])
    t = jnp.moveaxis(t, 2, 1)
    return t.reshape((b * d, s // d) + t.shape[3:])


def from_streams(t, b, d):
    n, l = t.shape[:2]
    t = t.reshape((b, d, l) + t.shape[2:])
    t = jnp.moveaxis(t, 1, 2)
    return t.reshape((b, l * d) + t.shape[3:])


def band_attention(q, k, v):
    n, l = q.shape[:2]
    nb = -(-l // ATT_BLOCK)
    lp = nb * ATT_BLOCK
    pad = ((0, 0), (0, lp - l), (0, 0), (0, 0))
    q, k, v = [jnp.pad(t, pad).reshape(n, nb, ATT_BLOCK, N_HEADS, HEAD_DIM) for t in (q, k, v)]

    def with_prev(t):
        prev = jnp.pad(t[:, :-1], ((0, 0), (1, 0), (0, 0), (0, 0), (0, 0)))
        return jnp.concatenate([prev, t], axis=2)

    kb, vb = with_prev(k), with_prev(v)
    s = jnp.einsum('nbqhd,nbkhd->nbhqk', q, kb) * (HEAD_DIM ** -0.5)
    qi = jnp.arange(ATT_BLOCK)[:, None]
    kj = jnp.arange(2 * ATT_BLOCK)[None, :]
    dist = qi + ATT_BLOCK - kj
    blk = jnp.arange(nb)[:, None, None]
    valid = (dist >= 0) & (dist <= ATT_SPAN) & ((blk > 0) | (kj >= ATT_BLOCK))
    s = jnp.where(valid[None, :, None], s, NEG)
    lse = jax.nn.logsumexp(s, axis=-1)
    p = jnp.exp(s - lse[..., None])
    o = jnp.einsum('nbhqk,nbkhd->nbqhd', p, vb).reshape(n, lp, N_HEADS, HEAD_DIM)[:, :l]
    lse = jnp.moveaxis(lse, 2, 3).reshape(n, lp, N_HEADS)[:, :l]
    return o, lse


def combine_branches(outs, lses):
    w = jax.nn.softmax(jnp.stack(lses, axis=0), axis=0)
    return jnp.einsum('pbsh,pbshd->bshd', w, jnp.stack(outs, axis=0))


def dilated_attention_prompt(q, k, v):
    b = q.shape[0]
    outs, lses = [], []
    for _, d in ATT_PATTERNS:
        o, lse = band_attention(to_streams(q, d), to_streams(k, d), to_streams(v, d))
        outs.append(from_streams(o, b, d))
        lses.append(from_streams(lse, b, d))
    return combine_branches(outs, lses)


def dilated_attention_sample(q, k, v, kv_buf):
    t = q.shape[1]
    w_buf = kv_buf.shape[1]
    k_all = jnp.concatenate([kv_buf[:, :, 0].astype(jnp.float32), k], axis=1)
    v_all = jnp.concatenate([kv_buf[:, :, 1].astype(jnp.float32), v], axis=1)
    steps = jnp.arange(ATT_SPAN + 1)
    outs, lses = [], []
    for _, d in ATT_PATTERNS:
        idx = w_buf + jnp.arange(t)[:, None] - steps[None, :] * d
        valid = idx >= 0
        idx = jnp.maximum(idx, 0)
        kg = k_all[:, idx]
        vg = v_all[:, idx]
        s = jnp.einsum('bthd,btkhd->bthk', q, kg) * (HEAD_DIM ** -0.5)
        s = jnp.where(valid[None, :, None, :], s, NEG)
        lse = jax.nn.logsumexp(s, axis=-1)
        p = jnp.exp(s - lse[..., None])
        outs.append(jnp.einsum('bthk,btkhd->bthd', p, vg))
        lses.append(lse)
    return combine_branches(outs, lses)


def ssm_mixer(u, p, h0):
    f32 = jnp.float32
    bt, l, _ = u.shape
    uf = u.astype(f32).reshape(bt, l, N_SSM_GROUPS, SSM_GROUP)
    a_re = p['ssm_a_re'].astype(f32)
    a_im = p['ssm_a_im'].astype(f32)
    dt = jnp.exp(p['ssm_log_dt'].astype(f32))[:, None]
    mag = jnp.exp(a_re * dt)
    ang = a_im * dt
    ab_re, ab_im = mag * jnp.cos(ang), mag * jnp.sin(ang)
    den = a_re * a_re + a_im * a_im
    nr, ni = ab_re - 1.0, ab_im
    cf_re = (nr * a_re + ni * a_im) / den
    cf_im = (ni * a_re - nr * a_im) / den
    b_re, b_im = p['ssm_b_re'].astype(f32), p['ssm_b_im'].astype(f32)
    bb_re = cf_re[..., None] * b_re - cf_im[..., None] * b_im
    bb_im = cf_re[..., None] * b_im + cf_im[..., None] * b_re
    x_re = jnp.einsum('blgc,gnc->blgn', uf, bb_re)
    x_im = jnp.einsum('blgc,gnc->blgn', uf, bb_im)
    a_re_b = jnp.broadcast_to(ab_re, x_re.shape)
    a_im_b = jnp.broadcast_to(ab_im, x_re.shape)

    def comb(e1, e2):
        ar1, ai1, br1, bi1 = e1
        ar2, ai2, br2, bi2 = e2
        return (ar2 * ar1 - ai2 * ai1, ar2 * ai1 + ai2 * ar1,
                ar2 * br1 - ai2 * bi1 + br2, ar2 * bi1 + ai2 * br1 + bi2)

    pr, pi, h_re, h_im = lax.associative_scan(comb, (a_re_b, a_im_b, x_re, x_im), axis=1)
    if h0 is not None:
        s_re = h0[0].astype(f32)[:, None]
        s_im = h0[1].astype(f32)[:, None]
        h_re, h_im = h_re + pr * s_re - pi * s_im, h_im + pr * s_im + pi * s_re
    c_re, c_im = p['ssm_c_re'].astype(f32), p['ssm_c_im'].astype(f32)
    y = (jnp.einsum('blgn,gcn->blgc', h_re, c_re) - jnp.einsum('blgn,gcn->blgc', h_im, c_im)
         + p['ssm_d'].astype(f32) * uf)
    y = jax.nn.gelu(y.reshape(bt, l, SSM_WIDTH))
    y = y * jax.nn.sigmoid(y @ p['ssm_w_glu'].astype(f32) + p['ssm_b_glu'].astype(f32))
    return y.astype(u.dtype), h_re[:, -1], h_im[:, -1]


def conv_mixer(h, gb, gc, w_conv, buf):
    u = gc * h
    l = u.shape[1]
    if buf is None:
        prefix = jnp.zeros((u.shape[0], CONV_K - 1, u.shape[2]), u.dtype)
    else:
        prefix = buf.astype(u.dtype)
    up = jnp.concatenate([prefix, u], axis=1)
    y = sum(w_conv[j] * up[:, j:j + l] for j in range(CONV_K))
    return gb * y, up[:, -(CONV_K - 1):]


def trunk_layer(x, p, attend, conv_buf, ssm_h0):
    bt, l, _ = x.shape
    xn = rmsnorm(x, p['g_mix_norm'])
    z = xn @ p['w_in']
    q, k, v, u, h, gb, gc = jnp.split(z, SPLITS, axis=-1)
    heads = lambda t: t.reshape(bt, l, N_HEADS, HEAD_DIM).astype(jnp.float32)
    q, k, v = heads(q), heads(k), heads(v)
    o_att = attend(q, k, v).reshape(bt, l, ATT_WIDTH).astype(x.dtype)
    y_ssm, h_re, h_im = ssm_mixer(u, p, ssm_h0)
    y_conv, conv_state = conv_mixer(h, gb, gc, p['conv_w'], conv_buf)
    g = p['g_out']
    mixed = jnp.concatenate([
        rmsnorm(o_att, g[:ATT_WIDTH]),
        rmsnorm(y_ssm.astype(x.dtype), g[ATT_WIDTH:ATT_WIDTH + SSM_WIDTH]),
        rmsnorm(y_conv.astype(x.dtype), g[ATT_WIDTH + SSM_WIDTH:])], axis=-1)
    x = x + mixed @ p['w_out']
    xn = rmsnorm(x, p['g_ffn_norm'])
    x = x + (jax.nn.silu(xn @ p['w_gate']) * (xn @ p['w_up'])) @ p['w_down']
    return x, k, v, conv_state, h_re, h_im


def setup_inputs(seed: int = 0) -> dict:
    key = jax.random.key(seed)
    ks = jax.random.split(key, 26)
    f32 = jnp.float32
    nrm = lambda k, shape, scale: scale * jax.random.normal(k, shape, f32)
    gain = lambda k, shape: 1.0 + 0.02 * jax.random.normal(k, shape, f32)
    kv_buf = min(ATT_MAX_WINDOW, PAST_LEN)
    sshape = (DEPTH, N_SSM_GROUPS, SSM_STATE)
    a_im_init = jnp.pi * jnp.arange(SSM_STATE, dtype=f32)
    return {
        'x_prompt': nrm(ks[0], (BATCH, SEQ, D_MODEL), 1.0),
        'x_sample': nrm(ks[1], (DEC_BATCH, DEC_SEQ, D_MODEL), 1.0),
        'cache_kv': nrm(ks[2], (DEPTH, DEC_BATCH, kv_buf, 2, N_HEADS, HEAD_DIM), 1.0),
        'state_conv': nrm(ks[3], (DEPTH, DEC_BATCH, CONV_K - 1, CONV_CH), 1.0),
        'state_ssm_re': nrm(ks[4], (DEPTH, DEC_BATCH, N_SSM_GROUPS, SSM_STATE), 0.5),
        'state_ssm_im': nrm(ks[5], (DEPTH, DEC_BATCH, N_SSM_GROUPS, SSM_STATE), 0.5),
        'g_mix_norm': gain(ks[6], (DEPTH, D_MODEL)),
        'w_in': nrm(ks[7], (DEPTH, D_MODEL, IN_COLS), D_MODEL ** -0.5),
        'ssm_a_re': -0.5 + nrm(ks[8], sshape, 0.01),
        'ssm_a_im': a_im_init + nrm(ks[9], sshape, 0.01),
        'ssm_log_dt': jax.random.uniform(ks[10], (DEPTH, N_SSM_GROUPS), f32, math.log(DT_MIN), math.log(DT_MAX)),
        'ssm_b_re': nrm(ks[11], (DEPTH, N_SSM_GROUPS, SSM_STATE, SSM_GROUP), (2 * SSM_GROUP) ** -0.5),
        'ssm_b_im': nrm(ks[12], (DEPTH, N_SSM_GROUPS, SSM_STATE, SSM_GROUP), (2 * SSM_GROUP) ** -0.5),
        'ssm_c_re': nrm(ks[13], (DEPTH, N_SSM_GROUPS, SSM_GROUP, SSM_STATE), (2 * SSM_STATE) ** -0.5),
        'ssm_c_im': nrm(ks[14], (DEPTH, N_SSM_GROUPS, SSM_GROUP, SSM_STATE), (2 * SSM_STATE) ** -0.5),
        'ssm_d': nrm(ks[15], (DEPTH, N_SSM_GROUPS, SSM_GROUP), 1.0),
        'ssm_w_glu': nrm(ks[16], (DEPTH, SSM_WIDTH, SSM_WIDTH), SSM_WIDTH ** -0.5),
        'ssm_b_glu': nrm(ks[17], (DEPTH, SSM_WIDTH), 0.01),
        'conv_w': nrm(ks[18], (DEPTH, CONV_K, CONV_CH), CONV_K ** -0.5),
        'g_out': gain(ks[19], (DEPTH, MIX)),
        'w_out': nrm(ks[20], (DEPTH, MIX, D_MODEL), MIX ** -0.5),
        'g_ffn_norm': gain(ks[21], (DEPTH, D_MODEL)),
        'w_gate': nrm(ks[22], (DEPTH, D_MODEL, D_FF), D_MODEL ** -0.5),
        'w_up': nrm(ks[23], (DEPTH, D_MODEL, D_FF), D_MODEL ** -0.5),
        'w_down': nrm(ks[24], (DEPTH, D_FF, D_MODEL), D_FF ** -0.5),
        'g_final': gain(ks[25], (D_MODEL,)),
    }


def reference(x_prompt, x_sample, cache_kv, state_conv, state_ssm_re, state_ssm_im,
              g_mix_norm, w_in, ssm_a_re, ssm_a_im, ssm_log_dt, ssm_b_re, ssm_b_im,
              ssm_c_re, ssm_c_im, ssm_d, ssm_w_glu, ssm_b_glu, conv_w, g_out, w_out,
              g_ffn_norm, w_gate, w_up, w_down, g_final):
    n_keep = min(ATT_MAX_WINDOW, x_prompt.shape[1])
    xp, xs = x_prompt, x_sample
    kv_p, conv_p, sre_p, sim_p = [], [], [], []
    kv_s, conv_s, sre_s, sim_s = [], [], [], []
    for i in range(DEPTH):
        p = {'g_mix_norm': g_mix_norm[i], 'w_in': w_in[i], 'ssm_a_re': ssm_a_re[i],
             'ssm_a_im': ssm_a_im[i], 'ssm_log_dt': ssm_log_dt[i], 'ssm_b_re': ssm_b_re[i],
             'ssm_b_im': ssm_b_im[i], 'ssm_c_re': ssm_c_re[i], 'ssm_c_im': ssm_c_im[i],
             'ssm_d': ssm_d[i], 'ssm_w_glu': ssm_w_glu[i], 'ssm_b_glu': ssm_b_glu[i],
             'conv_w': conv_w[i], 'g_out': g_out[i], 'w_out': w_out[i],
             'g_ffn_norm': g_ffn_norm[i], 'w_gate': w_gate[i], 'w_up': w_up[i], 'w_down': w_down[i]}
        xp, k, v, c_state, h_re, h_im = trunk_layer(xp, p, dilated_attention_prompt, None, None)
        kv_p.append(jnp.stack([k[:, -n_keep:], v[:, -n_keep:]], axis=2).astype(x_prompt.dtype))
        conv_p.append(c_state)
        sre_p.append(h_re)
        sim_p.append(h_im)
        attend = functools.partial(dilated_attention_sample, kv_buf=cache_kv[i])
        xs, k, v, c_state, h_re, h_im = trunk_layer(xs, p, attend, state_conv[i],
                                                    (state_ssm_re[i], state_ssm_im[i]))
        kv_s.append(jnp.stack([k, v], axis=2).astype(cache_kv.dtype))
        conv_s.append(c_state)
        sre_s.append(h_re)
        sim_s.append(h_im)
    y_prompt = rmsnorm(xp, g_final)
    y_sample = rmsnorm(xs, g_final)
    return (y_prompt, y_sample,
            jnp.stack(kv_p), jnp.stack(conv_p), jnp.stack(sre_p), jnp.stack(sim_p),
            jnp.stack(kv_s), jnp.stack(conv_s), jnp.stack(sre_s), jnp.stack(sim_s))
```

```python
import functools
import math

import jax
import jax.numpy as jnp
from jax import lax
from jax.experimental import pallas as pl
from jax.experimental.pallas import tpu as pltpu

F32 = jnp.float32
BF16 = jnp.bfloat16

HEAD_DIM = 64
ATT_WIDTH = 1024
SSM_WIDTH = 512
CONV_CH = 512
CONV_K = 3
SSM_GROUP = 16
SSM_STATE = 64
ATT_DILATIONS = (1, 4, 16)
ATT_SPAN = 128
RMS_EPS = 1e-6
NEG = -1e30

LANES = 128
SUBLANES = 8
SCAN_SEQS = SUBLANES
SSM_BLOCK_GROUPS = LANES // SSM_GROUP
SSM_BLOCK_STATE = SSM_BLOCK_GROUPS * SSM_STATE

MIB = 1024 * 1024


def _params(semantics, vmem_bytes):
    return pltpu.CompilerParams(dimension_semantics=semantics, vmem_limit_bytes=int(vmem_bytes))


def _nbytes(shape, dtype):
    return math.prod(shape) * jnp.dtype(dtype).itemsize


def _rms(x, g):
    ms = jnp.mean(x * x, axis=-1, keepdims=True)
    return x * lax.rsqrt(ms + RMS_EPS) * g


def _sigmoid(x):
    return 1.0 / (1.0 + jnp.exp(-x))


def _gelu_tanh(x):
    c = math.sqrt(2.0 / math.pi)
    return 0.5 * x * (1.0 + jnp.tanh(c * (x + 0.044715 * (x * x * x))))


def _dot(a, b):
    return jnp.dot(a, b, preferred_element_type=F32)


def _dot_nt(a, b):
    return lax.dot_general(a, b, (((1,), (1,)), ((), ())), preferred_element_type=F32)


def _in_proj_kernel(x_ref, g_ref, w_ref, q_ref, kv_ref, rest_ref, xn_ref, *, nq, nkv):
    j = pl.program_id(1)

    @pl.when(j == 0)
    def _():
        xn_ref[...] = _rms(x_ref[...], g_ref[...]).astype(BF16)

    @pl.when(j < nq)
    def _():
        q_ref[...] = _dot(xn_ref[...], w_ref[...])

    @pl.when((j >= nq) & (j < nq + nkv))
    def _():
        kv_ref[...] = _dot(xn_ref[...], w_ref[...])

    @pl.when(j >= nq + nkv)
    def _():
        rest_ref[...] = _dot(xn_ref[...], w_ref[...])


def _in_proj(x, g, w, *, tm, tn=512):
    m, k = x.shape
    n = w.shape[1]
    nq, nkv = ATT_WIDTH // tn, 2 * ATT_WIDTH // tn
    nrest = n // tn - nq - nkv
    vmem = (2 * (_nbytes((tm, k), F32) + _nbytes((k, tn), BF16) + 3 * _nbytes((tm, tn), F32))
            + _nbytes((tm, k), BF16) + 2 * _nbytes((tm, k), F32) + 4 * MIB)
    return pl.pallas_call(
        functools.partial(_in_proj_kernel, nq=nq, nkv=nkv),
        out_shape=(jax.ShapeDtypeStruct((m, nq * tn), F32),
                   jax.ShapeDtypeStruct((m, nkv * tn), F32),
                   jax.ShapeDtypeStruct((m, nrest * tn), F32)),
        grid=(m // tm, n // tn),
        in_specs=[pl.BlockSpec((tm, k), lambda i, j: (i, 0)),
                  pl.BlockSpec((1, k), lambda i, j: (0, 0)),
                  pl.BlockSpec((k, tn), lambda i, j: (0, j))],
        out_specs=(pl.BlockSpec((tm, tn), lambda i, j: (i, jnp.minimum(j, nq - 1))),
                   pl.BlockSpec((tm, tn), lambda i, j: (i, jnp.clip(j - nq, 0, nkv - 1))),
                   pl.BlockSpec((tm, tn), lambda i, j: (i, jnp.clip(j - nq - nkv, 0, nrest - 1)))),
        scratch_shapes=[pltpu.VMEM((tm, k), BF16)],
        compiler_params=_params(("parallel", "arbitrary"), vmem),
        name="in_proj",
    )(x, g.reshape(1, k), w)


def _attn_prompt_kernel(q_ref, k_ref, v_ref, o_ref, ob_ref, lb_ref):
    s_len = q_ref.shape[0]
    blk = ATT_SPAN
    lane = lax.broadcasted_iota(jnp.int32, (blk, LANES), 1)
    head0 = lane < HEAD_DIM
    qi = lax.broadcasted_iota(jnp.int32, (blk, blk), 0)
    kj = lax.broadcasted_iota(jnp.int32, (blk, blk), 1)
    own_ok = kj <= qi
    prev_ok = kj >= qi
    scale = HEAD_DIM ** -0.5

    def block(p, d, q0, kp0):
        def rows(start):
            return pl.ds(start, blk, stride=d) if d > 1 else pl.ds(start, blk)

        qt = q_ref[rows(q0), :]
        k_own = k_ref[rows(q0), :].astype(BF16)
        v_own = v_ref[rows(q0), :].astype(BF16)
        if kp0 is not None:
            k_prev = k_ref[rows(kp0), :].astype(BF16)
            v_prev = v_ref[rows(kp0), :].astype(BF16)
        outs, lses = [], []
        for h in range(2):
            hm = head0 if h == 0 else jnp.logical_not(head0)
            qh = jnp.where(hm, qt, 0.0).astype(BF16)
            s = jnp.where(own_ok, _dot_nt(qh, k_own) * scale, NEG)
            m = jnp.max(s, axis=-1, keepdims=True)
            if kp0 is not None:
                sp = jnp.where(prev_ok, _dot_nt(qh, k_prev) * scale, NEG)
                m = jnp.maximum(m, jnp.max(sp, axis=-1, keepdims=True))
            e = jnp.exp(s - m)
            l = jnp.sum(e, axis=-1, keepdims=True)
            pv = _dot(e.astype(BF16), v_own)
            if kp0 is not None:
                ep = jnp.exp(sp - m)
                l = l + jnp.sum(ep, axis=-1, keepdims=True)
                pv = pv + _dot(ep.astype(BF16), v_prev)
            outs.append(pv / l)
            lses.append(jnp.broadcast_to(m + jnp.log(l), (blk, LANES)))
        ob_ref[p, rows(q0), :] = jnp.where(head0, outs[0], outs[1])
        lb_ref[p, rows(q0), :] = jnp.where(head0, lses[0], lses[1])

    for p, d in enumerate(ATT_DILATIONS):
        nblk = s_len // (d * blk)

        def stream(r, carry, p=p, d=d, nblk=nblk):
            start = (lambda b: pl.multiple_of(b * blk, blk)) if d == 1 else (lambda b: r + d * blk * b)
            block(p, d, 0 if d == 1 else r, None)
            if nblk > 1:
                def later(b, c):
                    block(p, d, start(b), start(b - 1))
                    return c
                lax.fori_loop(1, nblk, later, 0)
            return carry

        if d == 1:
            stream(0, 0)
        else:
            lax.fori_loop(0, d, stream, 0)

    def merge(c, carry):
        rows = pl.ds(pl.multiple_of(c * blk, blk), blk)
        l0, l1, l2 = lb_ref[0, rows, :], lb_ref[1, rows, :], lb_ref[2, rows, :]
        m = jnp.maximum(jnp.maximum(l0, l1), l2)
        e0, e1, e2 = jnp.exp(l0 - m), jnp.exp(l1 - m), jnp.exp(l2 - m)
        num = e0 * ob_ref[0, rows, :] + e1 * ob_ref[1, rows, :] + e2 * ob_ref[2, rows, :]
        o_ref[rows, :] = num / (e0 + e1 + e2)
        return carry

    lax.fori_loop(0, s_len // blk, merge, 0)


def _attn_prompt(q, kv):
    b, s, _ = q.shape
    assert s % (max(ATT_DILATIONS) * ATT_SPAN) == 0
    npair = ATT_WIDTH // LANES
    blk_bytes = _nbytes((s, LANES), F32)
    vmem = 2 * 4 * blk_bytes + 6 * blk_bytes + 4 * MIB
    return pl.pallas_call(
        _attn_prompt_kernel,
        out_shape=jax.ShapeDtypeStruct((b, s, ATT_WIDTH), F32),
        grid=(b, npair),
        in_specs=[pl.BlockSpec((None, s, LANES), lambda i, j: (i, 0, j)),
                  pl.BlockSpec((None, s, LANES), lambda i, j: (i, 0, j)),
                  pl.BlockSpec((None, s, LANES), lambda i, j: (i, 0, npair + j))],
        out_specs=pl.BlockSpec((None, s, LANES), lambda i, j: (i, 0, j)),
        scratch_shapes=[pltpu.VMEM((len(ATT_DILATIONS), s, LANES), F32),
                        pltpu.VMEM((len(ATT_DILATIONS), s, LANES), F32)],
        compiler_params=_params(("parallel", "parallel"), vmem),
        name="attn_prompt",
    )(q, kv, kv)


def _attn_sample_kernel(q_ref, kn_ref, vn_ref, kc_ref, vc_ref, o_ref):
    t_len = q_ref.shape[0]
    w_len = kc_ref.shape[0]
    pad = LANES
    scale = HEAD_DIM ** -0.5
    q = q_ref[...]
    zpad = jnp.zeros((pad - t_len, LANES), F32)
    kn = jnp.concatenate([kn_ref[...], zpad], axis=0).astype(BF16)
    vn = jnp.concatenate([vn_ref[...], zpad], axis=0).astype(BF16)
    kc = kc_ref[...].astype(BF16)
    vc = vc_ref[...].astype(BF16)

    def multiplicity(dist):
        c = jnp.zeros(dist.shape, F32)
        for d in ATT_DILATIONS:
            ok = (dist >= 0) & ((dist & (d - 1)) == 0) & (dist <= ATT_SPAN * d)
            c = c + jnp.where(ok, 1.0, 0.0)
        return c

    tc = lax.broadcasted_iota(jnp.int32, (t_len, w_len), 0)
    jc = lax.broadcasted_iota(jnp.int32, (t_len, w_len), 1)
    cnt_c = multiplicity(w_len + tc - jc)
    tn = lax.broadcasted_iota(jnp.int32, (t_len, pad), 0)
    jn = lax.broadcasted_iota(jnp.int32, (t_len, pad), 1)
    cnt_n = multiplicity(jnp.where(jn < t_len, tn - jn, -1))

    lane = lax.broadcasted_iota(jnp.int32, (t_len, LANES), 1)
    head0 = lane < HEAD_DIM
    outs = []
    for h in range(2):
        hm = head0 if h == 0 else jnp.logical_not(head0)
        qh = jnp.where(hm, q, 0.0).astype(BF16)
        sc = jnp.where(cnt_c > 0, _dot_nt(qh, kc) * scale, NEG)
        sn = jnp.where(cnt_n > 0, _dot_nt(qh, kn) * scale, NEG)
        m = jnp.maximum(jnp.max(sc, axis=-1, keepdims=True), jnp.max(sn, axis=-1, keepdims=True))
        ec = cnt_c * jnp.exp(sc - m)
        en = cnt_n * jnp.exp(sn - m)
        l = jnp.sum(ec, axis=-1, keepdims=True) + jnp.sum(en, axis=-1, keepdims=True)
        pv = _dot(ec.astype(BF16), vc) + _dot(en.astype(BF16), vn)
        outs.append(pv / l)
    o_ref[...] = jnp.where(head0, outs[0], outs[1])


def _attn_sample(q, kv, cache):
    b, t, _ = q.shape
    w = cache.shape[1]
    assert all(d & (d - 1) == 0 for d in ATT_DILATIONS) and t <= LANES
    npair = ATT_WIDTH // LANES
    vmem = 2 * 2 * _nbytes((w, LANES), F32) + 8 * _nbytes((w, LANES), F32) + 4 * MIB
    new_spec = lambda off: pl.BlockSpec((None, t, LANES), lambda i, j: (i, 0, off + j))
    buf_spec = lambda off: pl.BlockSpec((None, w, LANES), lambda i, j: (i, 0, off + j))
    return pl.pallas_call(
        _attn_sample_kernel,
        out_shape=jax.ShapeDtypeStruct((b, t, ATT_WIDTH), F32),
        grid=(b, npair),
        in_specs=[new_spec(0), new_spec(0), new_spec(npair), buf_spec(0), buf_spec(npair)],
        out_specs=new_spec(0),
        compiler_params=_params(("parallel", "parallel"), vmem),
        name="attn_sample",
    )(q, kv, kv, cache, cache)


def _ssm_prep_kernel(are_ref, aim_ref, ldt_ref, bre_ref, bim_ref, abre_ref, abim_ref, bbre_ref, bbim_ref):
    a_re, a_im = are_ref[...], aim_ref[...]
    dt = jnp.exp(ldt_ref[...])
    mag = jnp.exp(a_re * dt)
    ang = a_im * dt
    ab_re, ab_im = mag * jnp.cos(ang), mag * jnp.sin(ang)
    den = a_re * a_re + a_im * a_im
    nr, ni = ab_re - 1.0, ab_im
    cf_re = (nr * a_re + ni * a_im) / den
    cf_im = (ni * a_re - nr * a_im) / den
    abre_ref[...] = ab_re
    abim_ref[...] = ab_im
    for c in range(SSM_GROUP):
        b_re, b_im = bre_ref[c], bim_ref[c]
        bbre_ref[c] = cf_re * b_re - cf_im * b_im
        bbim_ref[c] = cf_re * b_im + cf_im * b_re


def _ssm_prep(a_re, a_im, log_dt, b_re, b_im):
    depth, g, n = a_re.shape
    rows = depth * g
    flat = lambda a: a.reshape(rows, n)
    chan_major = lambda b: jnp.transpose(b.reshape(rows, n, SSM_GROUP), (2, 0, 1))
    ab_re, ab_im, bb_re, bb_im = pl.pallas_call(
        _ssm_prep_kernel,
        out_shape=(jax.ShapeDtypeStruct((rows, n), F32), jax.ShapeDtypeStruct((rows, n), F32),
                   jax.ShapeDtypeStruct((SSM_GROUP, rows, n), F32),
                   jax.ShapeDtypeStruct((SSM_GROUP, rows, n), F32)),
        name="ssm_prep",
    )(flat(a_re), flat(a_im), log_dt.reshape(rows, 1), chan_major(b_re), chan_major(b_im))
    unflat = lambda b: jnp.transpose(b, (1, 0, 2)).reshape(depth, g, SSM_GROUP, n)
    return ab_re.reshape(depth, g, n), ab_im.reshape(depth, g, n), unflat(bb_re), unflat(bb_im)


def _block_diag(m):
    g, r, c = m.shape
    nb = g // SSM_BLOCK_GROUPS
    m = m.reshape(nb, SSM_BLOCK_GROUPS, r, c)
    eye = jnp.eye(SSM_BLOCK_GROUPS, dtype=m.dtype)
    out = m[:, :, :, None, :] * eye[None, :, None, :, None]
    return out.reshape(nb, SSM_BLOCK_GROUPS * r, SSM_BLOCK_GROUPS * c)


def _ssm_kernel(u_ref, h0re_ref, h0im_ref, are_ref, aim_ref, bb_ref, cre_ref, cim_ref, d_ref,
                y_ref, hre_ref, him_ref, x_scr, up_scr, *, steps, chained):
    ns = SSM_BLOCK_STATE
    nrows = steps * SCAN_SEQS
    chunk = min(nrows, 256)

    def permute_in(t, c):
        up_scr[pl.ds(pl.multiple_of(t * SCAN_SEQS, SCAN_SEQS), SCAN_SEQS), :] = (
            u_ref[pl.ds(t, SCAN_SEQS, stride=steps), :])
        return c
    lax.fori_loop(0, steps, permute_in, 0)

    def input_map(c, carry):
        rows = pl.ds(pl.multiple_of(c * chunk, chunk), chunk)
        x_scr[rows, :] = _dot(up_scr[rows, :].astype(BF16), bb_ref[...])
        return carry
    lax.fori_loop(0, nrows // chunk, input_map, 0)

    a_re = jnp.broadcast_to(are_ref[...], (SCAN_SEQS, ns))
    a_im = jnp.broadcast_to(aim_ref[...], (SCAN_SEQS, ns))

    def scan(init, store):
        def step(t, h):
            h_re, h_im = h
            rows = pl.ds(pl.multiple_of(t * SCAN_SEQS, SCAN_SEQS), SCAN_SEQS)
            n_re = a_re * h_re - a_im * h_im + x_scr[rows, :ns]
            n_im = a_re * h_im + a_im * h_re + x_scr[rows, ns:]
            if store:
                x_scr[rows, :ns] = n_re
                x_scr[rows, ns:] = n_im
            return n_re, n_im
        return lax.fori_loop(0, steps, step, init, unroll=8 if steps >= 8 else steps)

    init = (h0re_ref[...], h0im_ref[...])
    if chained:
        assert steps & (steps - 1) == 0
        e_re, e_im = scan(init, store=False)
        p_re, p_im = a_re, a_im
        for _ in range(steps.bit_length() - 1):
            p_re, p_im = p_re * p_re - p_im * p_im, 2.0 * p_re * p_im
        sub = lax.broadcasted_iota(jnp.int32, (SCAN_SEQS, ns), 0)
        shift = 1
        while shift < SCAN_SEQS:
            r_re = pltpu.roll(e_re, shift, axis=0)
            r_im = pltpu.roll(e_im, shift, axis=0)
            keep = sub >= shift
            e_re, e_im = (e_re + jnp.where(keep, p_re * r_re - p_im * r_im, 0.0),
                          e_im + jnp.where(keep, p_re * r_im + p_im * r_re, 0.0))
            p_re, p_im = p_re * p_re - p_im * p_im, 2.0 * p_re * p_im
            shift *= 2
        first = sub == 0
        init = (init[0] + jnp.where(first, 0.0, pltpu.roll(e_re, 1, axis=0)),
                init[1] + jnp.where(first, 0.0, pltpu.roll(e_im, 1, axis=0)))
    f_re, f_im = scan(init, store=True)
    hre_ref[...] = f_re
    him_ref[...] = f_im

    def output_map(c, carry):
        rows = pl.ds(pl.multiple_of(c * chunk, chunk), chunk)
        y = (_dot(x_scr[rows, :ns].astype(BF16), cre_ref[...])
             - _dot(x_scr[rows, ns:].astype(BF16), cim_ref[...])
             + d_ref[...] * up_scr[rows, :])
        up_scr[rows, :] = _gelu_tanh(y)
        return carry
    lax.fori_loop(0, nrows // chunk, output_map, 0)

    def permute_out(t, c):
        y_ref[pl.ds(t, SCAN_SEQS, stride=steps), :] = (
            up_scr[pl.ds(pl.multiple_of(t * SCAN_SEQS, SCAN_SEQS), SCAN_SEQS), :])
        return c
    lax.fori_loop(0, steps, permute_out, 0)


def _ssm(u, h0_re, h0_im, ab_re, ab_im, bb, c_re, c_im, d_skip, *, chained):
    nb, rows, _ = u.shape
    steps = rows // SCAN_SEQS
    nblk = SSM_WIDTH // LANES
    ns = SSM_BLOCK_STATE
    vmem = (2 * 2 * _nbytes((rows, LANES), F32) + _nbytes((rows, 2 * ns), F32)
            + _nbytes((rows, LANES), F32) + 4 * _nbytes((min(rows, 256), 2 * ns), F32) + 8 * MIB)
    state_spec = pl.BlockSpec((None, SCAN_SEQS, ns), lambda i, j: (i, 0, j))
    return pl.pallas_call(
        functools.partial(_ssm_kernel, steps=steps, chained=chained),
        out_shape=(jax.ShapeDtypeStruct((nb, rows, SSM_WIDTH), F32),
                   jax.ShapeDtypeStruct(h0_re.shape, F32),
                   jax.ShapeDtypeStruct(h0_im.shape, F32)),
        grid=(nb, nblk),
        in_specs=[pl.BlockSpec((None, rows, LANES), lambda i, j: (i, 0, j)),
                  state_spec, state_spec,
                  pl.BlockSpec((1, ns), lambda i, j: (0, j)),
                  pl.BlockSpec((1, ns), lambda i, j: (0, j)),
                  pl.BlockSpec((None, LANES, 2 * ns), lambda i, j: (j, 0, 0)),
                  pl.BlockSpec((None, ns, LANES), lambda i, j: (j, 0, 0)),
                  pl.BlockSpec((None, ns, LANES), lambda i, j: (j, 0, 0)),
                  pl.BlockSpec((1, LANES), lambda i, j: (0, j))],
        out_specs=(pl.BlockSpec((None, rows, LANES), lambda i, j: (i, 0, j)), state_spec, state_spec),
        scratch_shapes=[pltpu.VMEM((rows, 2 * ns), F32), pltpu.VMEM((rows, LANES), F32)],
        compiler_params=_params(("parallel", "parallel"), vmem),
        name="ssm_chained" if chained else "ssm_step",
    )(u, h0_re, h0_im, ab_re, ab_im, bb, c_re, c_im, d_skip)


def _conv_kernel(h_ref, gb_ref, gc_ref, w_ref, pre_ref, y_ref, st_ref):
    n = h_ref.shape[0]
    u = gc_ref[...] * h_ref[...]
    row = lax.broadcasted_iota(jnp.int32, u.shape, 0)
    pre = pre_ref[...]
    p0, p1 = pre[0:1, :], pre[1:2, :]
    u1 = jnp.where(row == 0, p1, pltpu.roll(u, 1, axis=0))
    u2 = jnp.where(row == 0, p0, jnp.where(row == 1, p1, pltpu.roll(u, 2, axis=0)))
    w = w_ref[...]
    y = w[0:1, :] * u2 + w[1:2, :] * u1 + w[2:3, :] * u
    y_ref[...] = gb_ref[...] * y
    st_ref[...] = u[n - (CONV_K - 1):, :]


def _conv(rest, w, prefix):
    b, n, _ = rest.shape
    assert n >= CONV_K - 1
    nblk = CONV_CH // LANES
    off = SSM_WIDTH // LANES
    col = lambda part: pl.BlockSpec((None, n, LANES), lambda i, j: (i, 0, off + part * nblk + j))
    vmem = 2 * 4 * _nbytes((n, LANES), F32) + 8 * _nbytes((n, LANES), F32) + 4 * MIB
    return pl.pallas_call(
        _conv_kernel,
        out_shape=(jax.ShapeDtypeStruct((b, n, CONV_CH), F32),
                   jax.ShapeDtypeStruct((b, CONV_K - 1, CONV_CH), F32)),
        grid=(b, nblk),
        in_specs=[col(0), col(1), col(2),
                  pl.BlockSpec((CONV_K, LANES), lambda i, j: (0, j)),
                  pl.BlockSpec((None, CONV_K - 1, LANES), lambda i, j: (i, 0, j))],
        out_specs=(pl.BlockSpec((None, n, LANES), lambda i, j: (i, 0, j)),
                   pl.BlockSpec((None, CONV_K - 1, LANES), lambda i, j: (i, 0, j))),
        compiler_params=_params(("parallel", "parallel"), vmem),
        name="conv",
    )(rest, rest, rest, w, prefix)


def _out_proj_kernel(oatt_ref, yssm_ref, yconv_ref, g_ref, wglu_ref, bglu_ref, w_ref, x_ref,
                     o_ref, mix_ref):
    j = pl.program_id(1)
    a0, a1 = ATT_WIDTH, ATT_WIDTH + SSM_WIDTH

    @pl.when(j == 0)
    def _():
        g = g_ref[...]
        mix_ref[:, :a0] = _rms(oatt_ref[...], g[:, :a0]).astype(BF16)
        ys = yssm_ref[...]
        ys = ys * _sigmoid(_dot(ys.astype(BF16), wglu_ref[...]) + bglu_ref[...])
        mix_ref[:, a0:a1] = _rms(ys, g[:, a0:a1]).astype(BF16)
        mix_ref[:, a1:] = _rms(yconv_ref[...], g[:, a1:]).astype(BF16)

    o_ref[...] = x_ref[...] + _dot(mix_ref[...], w_ref[...])


def _out_proj(o_att, y_ssm, y_conv, g, w_glu, b_glu, w, x, *, tm, tn=512):
    m, d = x.shape
    k = w.shape[0]
    vmem = (2 * (_nbytes((tm, k), F32) + _nbytes((k, tn), BF16) + 2 * _nbytes((tm, tn), F32)
                 + _nbytes((SSM_WIDTH, SSM_WIDTH), BF16))
            + _nbytes((tm, k), BF16) + 2 * _nbytes((tm, k), F32) + 4 * MIB)
    row = lambda c: pl.BlockSpec((tm, c), lambda i, j: (i, 0))
    const = lambda r, c: pl.BlockSpec((r, c), lambda i, j: (0, 0))
    return pl.pallas_call(
        _out_proj_kernel,
        out_shape=jax.ShapeDtypeStruct((m, d), F32),
        grid=(m // tm, d // tn),
        in_specs=[row(ATT_WIDTH), row(SSM_WIDTH), row(CONV_CH), const(1, k),
                  const(SSM_WIDTH, SSM_WIDTH), const(1, SSM_WIDTH),
                  pl.BlockSpec((k, tn), lambda i, j: (0, j)),
                  pl.BlockSpec((tm, tn), lambda i, j: (i, j))],
        out_specs=pl.BlockSpec((tm, tn), lambda i, j: (i, j)),
        scratch_shapes=[pltpu.VMEM((tm, k), BF16)],
        compiler_params=_params(("parallel", "arbitrary"), vmem),
        name="out_proj",
    )(o_att, y_ssm, y_conv, g.reshape(1, k), w_glu, b_glu.reshape(1, SSM_WIDTH), w, x)


def _ffn_kernel(x_ref, g_ref, wg_ref, wu_ref, wd_ref, o_ref, xn_ref):
    f = pl.program_id(1)

    @pl.when(f == 0)
    def _():
        x = x_ref[...]
        xn_ref[...] = _rms(x, g_ref[...]).astype(BF16)
        o_ref[...] = x

    xn = xn_ref[...]
    gate = _dot(xn, wg_ref[...])
    up = _dot(xn, wu_ref[...])
    act = (gate * _sigmoid(gate) * up).astype(BF16)
    o_ref[...] += _dot(act, wd_ref[...])


def _ffn(x, g, w_gate, w_up, w_down, *, tm, tf=512):
    m, d = x.shape
    ff = w_gate.shape[1]
    vmem = (2 * (2 * _nbytes((tm, d), F32) + 3 * _nbytes((d, tf), BF16))
            + _nbytes((tm, d), BF16) + 2 * _nbytes((tm, d), F32) + 4 * _nbytes((tm, tf), F32) + 4 * MIB)
    return pl.pallas_call(
        _ffn_kernel,
        out_shape=jax.ShapeDtypeStruct((m, d), F32),
        grid=(m // tm, ff // tf),
        in_specs=[pl.BlockSpec((tm, d), lambda i, f: (i, 0)),
                  pl.BlockSpec((1, d), lambda i, f: (0, 0)),
                  pl.BlockSpec((d, tf), lambda i, f: (0, f)),
                  pl.BlockSpec((d, tf), lambda i, f: (0, f)),
                  pl.BlockSpec((tf, d), lambda i, f: (f, 0))],
        out_specs=pl.BlockSpec((tm, d), lambda i, f: (i, 0)),
        scratch_shapes=[pltpu.VMEM((tm, d), BF16)],
        compiler_params=_params(("parallel", "arbitrary"), vmem),
        name="ffn",
    )(x, g.reshape(1, d), w_gate, w_up, w_down)


def _final_norm_kernel(x_ref, g_ref, o_ref):
    o_ref[...] = _rms(x_ref[...], g_ref[...])


def _final_norm(x, g, *, tm):
    m, d = x.shape
    vmem = 2 * 2 * _nbytes((tm, d), F32) + 3 * _nbytes((tm, d), F32) + 4 * MIB
    return pl.pallas_call(
        _final_norm_kernel,
        out_shape=jax.ShapeDtypeStruct((m, d), F32),
        grid=(m // tm,),
        in_specs=[pl.BlockSpec((tm, d), lambda i: (i, 0)), pl.BlockSpec((1, d), lambda i: (0, 0))],
        out_specs=pl.BlockSpec((tm, d), lambda i: (i, 0)),
        compiler_params=_params(("parallel",), vmem),
        name="final_norm",
    )(x, g.reshape(1, d))


def _layer(x, lw, *, batch, cache, conv_prefix, ssm_h0, tm):
    m, d = x.shape
    n = m // batch
    q, kv, rest = _in_proj(x, lw['g_mix_norm'], lw['w_in'], tm=tm)
    q3, kv3, rest3 = (a.reshape(batch, n, a.shape[-1]) for a in (q, kv, rest))
    if cache is None:
        o_att = _attn_prompt(q3, kv3)
        u = rest3
        y_ssm, h_re, h_im = _ssm(u, *ssm_h0, lw['ab_re'], lw['ab_im'], lw['bb'], lw['c_re'], lw['c_im'],
                                 lw['ssm_d'], chained=True)
        h_re, h_im = h_re[:, SCAN_SEQS - 1], h_im[:, SCAN_SEQS - 1]
    else:
        o_att = _attn_sample(q3, kv3, cache)
        u = rest.reshape(1, m, rest.shape[-1])
        y_ssm, h_re, h_im = _ssm(u, *ssm_h0, lw['ab_re'], lw['ab_im'], lw['bb'], lw['c_re'], lw['c_im'],
                                 lw['ssm_d'], chained=False)
        h_re, h_im = h_re[0], h_im[0]
    y_conv, conv_state = _conv(rest3, lw['conv_w'], conv_prefix)
    x = _out_proj(o_att.reshape(m, ATT_WIDTH), y_ssm.reshape(m, SSM_WIDTH), y_conv.reshape(m, CONV_CH),
                  lw['g_out'], lw['ssm_w_glu'], lw['ssm_b_glu'], lw['w_out'], x, tm=min(tm, 512))
    x = _ffn(x, lw['g_ffn_norm'], lw['w_gate'], lw['w_up'], lw['w_down'], tm=min(tm, 512))
    return x, kv3, conv_state, h_re, h_im


def kernel(x_prompt, x_sample, cache_kv, state_conv, state_ssm_re, state_ssm_im, g_mix_norm, w_in,
           ssm_a_re, ssm_a_im, ssm_log_dt, ssm_b_re, ssm_b_im, ssm_c_re, ssm_c_im, ssm_d, ssm_w_glu,
           ssm_b_glu, conv_w, g_out, w_out, g_ffn_norm, w_gate, w_up, w_down, g_final):
    batch, seq, d_model = x_prompt.shape
    dec_batch, dec_seq, _ = x_sample.shape
    depth, n_groups, n_state = ssm_a_re.shape
    n_heads = ATT_WIDTH // HEAD_DIM
    assert dec_batch == SCAN_SEQS and seq % SCAN_SEQS == 0

    ab_re, ab_im, bb_re, bb_im = _ssm_prep(ssm_a_re, ssm_a_im, ssm_log_dt, ssm_b_re, ssm_b_im)

    xp = x_prompt.reshape(batch * seq, d_model)
    xs = x_sample.reshape(dec_batch * dec_seq, d_model)
    zero_state = jnp.zeros((batch, SCAN_SEQS, n_groups * n_state), F32)
    zero_prefix = jnp.zeros((batch, CONV_K - 1, CONV_CH), F32)

    outs_p, outs_s = [], []
    for i in range(depth):
        bb = jnp.concatenate([_block_diag(bb_re[i]), _block_diag(bb_im[i])], axis=-1).astype(BF16)
        c_t = lambda c: _block_diag(jnp.transpose(c[i], (0, 2, 1))).astype(BF16)
        lw = {
            'g_mix_norm': g_mix_norm[i], 'w_in': w_in[i].astype(BF16),
            'ab_re': ab_re[i].reshape(1, -1), 'ab_im': ab_im[i].reshape(1, -1), 'bb': bb,
            'c_re': c_t(ssm_c_re), 'c_im': c_t(ssm_c_im), 'ssm_d': ssm_d[i].reshape(1, -1),
            'ssm_w_glu': ssm_w_glu[i].astype(BF16), 'ssm_b_glu': ssm_b_glu[i], 'conv_w': conv_w[i],
            'g_out': g_out[i], 'w_out': w_out[i].astype(BF16), 'g_ffn_norm': g_ffn_norm[i],
            'w_gate': w_gate[i].astype(BF16), 'w_up': w_up[i].astype(BF16),
            'w_down': w_down[i].astype(BF16),
        }
        xp, kv, c_state, h_re, h_im = _layer(
            xp, lw, batch=batch, cache=None, conv_prefix=zero_prefix,
            ssm_h0=(zero_state, zero_state), tm=1024)
        n_keep = min(max(ATT_DILATIONS) * ATT_SPAN, seq)
        outs_p.append((kv[:, seq - n_keep:].reshape(batch, n_keep, 2, n_heads, HEAD_DIM), c_state,
                       h_re.reshape(batch, n_groups, n_state), h_im.reshape(batch, n_groups, n_state)))

        cache = cache_kv[i].reshape(dec_batch, cache_kv.shape[2], 2 * ATT_WIDTH)
        h0 = (state_ssm_re[i].reshape(1, dec_batch, -1), state_ssm_im[i].reshape(1, dec_batch, -1))
        xs, kv, c_state, h_re, h_im = _layer(
            xs, lw, batch=dec_batch, cache=cache, conv_prefix=state_conv[i], ssm_h0=h0,
            tm=dec_batch * dec_seq)
        outs_s.append((kv.reshape(dec_batch, dec_seq, 2, n_heads, HEAD_DIM), c_state,
                       h_re.reshape(dec_batch, n_groups, n_state),
                       h_im.reshape(dec_batch, n_groups, n_state)))

    y_prompt = _final_norm(xp, g_final, tm=1024).reshape(batch, seq, d_model)
    y_sample = _final_norm(xs, g_final, tm=dec_batch * dec_seq).reshape(dec_batch, dec_seq, d_model)
    stack = lambda outs, k: jnp.stack([o[k] for o in outs])
    return (y_prompt, y_sample,
            stack(outs_p, 0), stack(outs_p, 1), stack(outs_p, 2), stack(outs_p, 3),
            stack(outs_s, 0), stack(outs_s, 1), stack(outs_s, 2), stack(outs_s, 3))
```

```python
import functools
import math

import jax
import jax.numpy as jnp
from jax import lax
from jax.experimental import pallas as pl
from jax.experimental.pallas import tpu as pltpu

F32 = jnp.float32
BF16 = jnp.bfloat16

HEAD_DIM = 64
ATT_WIDTH = 1024
SSM_WIDTH = 512
CONV_CH = 512
CONV_K = 3
SSM_GROUP = 16
SSM_STATE = 64
ATT_DILATIONS = (1, 4, 16)
ATT_SPAN = 128
RMS_EPS = 1e-6
NEG = -1e30

LANES = 128
SUBLANES = 8
SCAN_SEQS = SUBLANES
SSM_BLOCK_GROUPS = LANES // SSM_GROUP
SSM_BLOCK_STATE = SSM_BLOCK_GROUPS * SSM_STATE

MIB = 1024 * 1024


def _params(semantics, vmem_bytes):
    return pltpu.CompilerParams(dimension_semantics=semantics, vmem_limit_bytes=int(vmem_bytes))


def _nbytes(shape, dtype):
    return math.prod(shape) * jnp.dtype(dtype).itemsize


def _rms(x, g):
    ms = jnp.mean(x * x, axis=-1, keepdims=True)
    return x * lax.rsqrt(ms + RMS_EPS) * g


def _sigmoid(x):
    return 1.0 / (1.0 + jnp.exp(-x))


def _gelu_tanh(x):
    c = math.sqrt(2.0 / math.pi)
    return 0.5 * x * (1.0 + jnp.tanh(c * (x + 0.044715 * (x * x * x))))


def _dot(a, b):
    return jnp.dot(a, b, preferred_element_type=F32)


def _dot_nt(a, b):
    return lax.dot_general(a, b, (((1,), (1,)), ((), ())), preferred_element_type=F32)


def _in_proj_kernel(x_ref, g_ref, w_ref, q_ref, kv_ref, rest_ref, *more, nq, nkv):
    kvt_ref = more[0] if len(more) == 2 else None
    xn_ref = more[-1]
    j = pl.program_id(1)

    @pl.when(j == 0)
    def _():
        xn_ref[...] = _rms(x_ref[...], g_ref[...]).astype(BF16)

    @pl.when(j < nq)
    def _():
        q_ref[...] = _dot(xn_ref[...], w_ref[...])

    @pl.when((j >= nq) & (j < nq + nkv))
    def _():
        z = _dot(xn_ref[...], w_ref[...])
        kv_ref[...] = z
        if kvt_ref is not None:
            kvt_ref[...] = z.T

    @pl.when(j >= nq + nkv)
    def _():
        rest_ref[...] = _dot(xn_ref[...], w_ref[...])


def _in_proj(x, g, w, layer, *, tm, seq, tn=512):
    m, k = x.shape
    n = w.shape[2]
    nq, nkv = ATT_WIDTH // tn, 2 * ATT_WIDTH // tn
    nrest = n // tn - nq - nkv
    with_kvt = seq % tm == 0 and tm % LANES == 0
    tiles_per_seq = seq // tm if with_kvt else 1
    vmem = (2 * (_nbytes((tm, k), F32) + _nbytes((k, tn), BF16) + 4 * _nbytes((tm, tn), F32))
            + _nbytes((tm, k), BF16) + 2 * _nbytes((tm, k), F32) + 2 * _nbytes((tm, tn), F32) + 4 * MIB)
    kv_col = lambda j: jnp.clip(j - nq, 0, nkv - 1)
    out_shape = [jax.ShapeDtypeStruct((m, nq * tn), F32),
                 jax.ShapeDtypeStruct((m, nkv * tn), F32),
                 jax.ShapeDtypeStruct((m, nrest * tn), F32)]
    out_specs = [pl.BlockSpec((tm, tn), lambda i, j: (i, jnp.minimum(j, nq - 1))),
                 pl.BlockSpec((tm, tn), lambda i, j: (i, kv_col(j))),
                 pl.BlockSpec((tm, tn), lambda i, j: (i, jnp.clip(j - nq - nkv, 0, nrest - 1)))]
    if with_kvt:
        out_shape.append(jax.ShapeDtypeStruct((m // seq, nkv * tn, seq), F32))
        out_specs.append(pl.BlockSpec(
            (None, tn, tm), lambda i, j: (i // tiles_per_seq, kv_col(j), i % tiles_per_seq)))
    outs = pl.pallas_call(
        functools.partial(_in_proj_kernel, nq=nq, nkv=nkv),
        out_shape=tuple(out_shape),
        grid=(m // tm, n // tn),
        in_specs=[pl.BlockSpec((tm, k), lambda i, j: (i, 0)),
                  pl.BlockSpec((1, k), lambda i, j: (0, 0)),
                  pl.BlockSpec((None, k, tn), lambda i, j: (layer, 0, j))],
        out_specs=tuple(out_specs),
        scratch_shapes=[pltpu.VMEM((tm, k), BF16)],
        compiler_params=_params(("parallel", "arbitrary"), vmem),
        name="in_proj",
    )(x, g.reshape(1, k), w)
    return outs if with_kvt else (*outs, None)


def _attn_prompt_kernel(q_ref, k_ref, v_ref, o_ref, ob_ref, lb_ref):
    s_len = q_ref.shape[0]
    blk = ATT_SPAN
    lane = lax.broadcasted_iota(jnp.int32, (blk, LANES), 1)
    head0 = lane < HEAD_DIM
    qi = lax.broadcasted_iota(jnp.int32, (2 * blk, blk), 0) & (blk - 1)
    kj = lax.broadcasted_iota(jnp.int32, (2 * blk, blk), 1)
    own_ok = kj <= qi
    prev_ok = kj >= qi
    scale = HEAD_DIM ** -0.5

    def block(p, d, q0, kp0, prev_live=None):
        def rows(start):
            return pl.ds(start, blk, stride=d) if d > 1 else pl.ds(start, blk)

        qt = q_ref[rows(q0), :] * scale
        qq = jnp.concatenate([jnp.where(head0, qt, 0.0), jnp.where(head0, 0.0, qt)], axis=0).astype(BF16)
        k_own = k_ref[rows(q0), :].astype(BF16)
        v_own = v_ref[rows(q0), :].astype(BF16)
        s = jnp.where(own_ok, _dot_nt(qq, k_own), NEG)
        m = jnp.max(s, axis=-1, keepdims=True)
        if kp0 is not None:
            k_prev = k_ref[rows(kp0), :].astype(BF16)
            v_prev = v_ref[rows(kp0), :].astype(BF16)
            ok = prev_ok if prev_live is None else kj >= qi + jnp.where(prev_live, 0, blk)
            sp = jnp.where(ok, _dot_nt(qq, k_prev), NEG)
            m = jnp.maximum(m, jnp.max(sp, axis=-1, keepdims=True))
        e = jnp.exp(s - m)
        l = jnp.sum(e, axis=-1, keepdims=True)
        pv = _dot(e.astype(BF16), v_own)
        if kp0 is not None:
            ep = jnp.exp(sp - m)
            l = l + jnp.sum(ep, axis=-1, keepdims=True)
            pv = pv + _dot(ep.astype(BF16), v_prev)
        o = pv * (1.0 / l)
        lse = jnp.broadcast_to(m + jnp.log(l), (2 * blk, LANES))
        ob_ref[p, rows(q0), :] = jnp.where(head0, o[:blk], o[blk:])
        lb_ref[p, rows(q0), :] = jnp.where(head0, lse[:blk], lse[blk:])

    group = 4
    for p, d in enumerate(ATT_DILATIONS):
        nblk = s_len // (d * blk)
        if nblk == 1:
            def streams(i, c, p=p, d=d):
                for g in range(group):
                    block(p, d, i * group + g, None)
                return c
            lax.fori_loop(0, d // group, streams, 0)
        elif d == 1:
            def blocks(i, c, p=p, d=d):
                for g in range(group):
                    b = i * group + g
                    block(p, d, pl.multiple_of(b * blk, blk),
                          pl.multiple_of(jnp.maximum(b - 1, 0) * blk, blk), prev_live=b > 0)
                return c
            lax.fori_loop(0, nblk // group, blocks, 0)
        else:
            assert d == group
            for r in range(d):
                block(p, d, r, None)

            def later(b, c, p=p, d=d):
                for r in range(d):
                    block(p, d, r + d * blk * b, r + d * blk * (b - 1))
                return c
            lax.fori_loop(1, nblk, later, 0)

    def merge(c, carry):
        rows = pl.ds(pl.multiple_of(c * blk, blk), blk)
        l0, l1, l2 = lb_ref[0, rows, :], lb_ref[1, rows, :], lb_ref[2, rows, :]
        m = jnp.maximum(jnp.maximum(l0, l1), l2)
        e0, e1, e2 = jnp.exp(l0 - m), jnp.exp(l1 - m), jnp.exp(l2 - m)
        num = e0 * ob_ref[0, rows, :] + e1 * ob_ref[1, rows, :] + e2 * ob_ref[2, rows, :]
        o_ref[rows, :] = num / (e0 + e1 + e2)
        return carry

    lax.fori_loop(0, s_len // blk, merge, 0)


def _attn_prompt(q, kv):
    b, s, _ = q.shape
    assert s % (max(ATT_DILATIONS) * ATT_SPAN) == 0
    npair = ATT_WIDTH // LANES
    blk_bytes = _nbytes((s, LANES), F32)
    vmem = 2 * 4 * blk_bytes + 6 * blk_bytes + 4 * MIB
    return pl.pallas_call(
        _attn_prompt_kernel,
        out_shape=jax.ShapeDtypeStruct((b, s, ATT_WIDTH), F32),
        grid=(b, npair),
        in_specs=[pl.BlockSpec((None, s, LANES), lambda i, j: (i, 0, j)),
                  pl.BlockSpec((None, s, LANES), lambda i, j: (i, 0, j)),
                  pl.BlockSpec((None, s, LANES), lambda i, j: (i, 0, npair + j))],
        out_specs=pl.BlockSpec((None, s, LANES), lambda i, j: (i, 0, j)),
        scratch_shapes=[pltpu.VMEM((len(ATT_DILATIONS), s, LANES), F32),
                        pltpu.VMEM((len(ATT_DILATIONS), s, LANES), F32)],
        compiler_params=_params(("parallel", "parallel"), vmem),
        name="attn_prompt",
    )(q, kv, kv)


def _attn_sample_kernel(q_ref, kn_ref, vn_ref, kct_ref, vct_ref, o_ref):
    t_len = q_ref.shape[0]
    w_len = kct_ref.shape[-1]
    pad = LANES
    q = q_ref[...] * (HEAD_DIM ** -0.5)
    zpad = jnp.zeros((pad - t_len, LANES), F32)
    kn = jnp.concatenate([kn_ref[...], zpad], axis=0).astype(BF16)
    vn = jnp.concatenate([vn_ref[...], zpad], axis=0).astype(BF16)
    kct = kct_ref[...].reshape(LANES, w_len).astype(BF16)
    vct = vct_ref[...].reshape(LANES, w_len).astype(BF16)

    def multiplicity(dist):
        c = jnp.zeros(dist.shape, F32)
        for d in ATT_DILATIONS:
            ok = (dist >= 0) & ((dist & (d - 1)) == 0) & (dist <= ATT_SPAN * d)
            c = c + jnp.where(ok, 1.0, 0.0)
        return c

    tc = lax.broadcasted_iota(jnp.int32, (t_len, w_len), 0)
    jc = lax.broadcasted_iota(jnp.int32, (t_len, w_len), 1)
    cnt_c = multiplicity(w_len + tc - jc)
    tn = lax.broadcasted_iota(jnp.int32, (t_len, pad), 0)
    jn = lax.broadcasted_iota(jnp.int32, (t_len, pad), 1)
    cnt_n = multiplicity(jnp.where(jn < t_len, tn - jn, -1))

    lane = lax.broadcasted_iota(jnp.int32, (t_len, LANES), 1)
    head0 = lane < HEAD_DIM
    outs = []
    for h in range(2):
        hm = head0 if h == 0 else jnp.logical_not(head0)
        qh = jnp.where(hm, q, 0.0).astype(BF16)
        sc = jnp.where(cnt_c > 0, _dot(qh, kct), NEG)
        sn = jnp.where(cnt_n > 0, _dot_nt(qh, kn), NEG)
        m = jnp.maximum(jnp.max(sc, axis=-1, keepdims=True), jnp.max(sn, axis=-1, keepdims=True))
        ec = cnt_c * jnp.exp(sc - m)
        en = cnt_n * jnp.exp(sn - m)
        l = jnp.sum(ec, axis=-1, keepdims=True) + jnp.sum(en, axis=-1, keepdims=True)
        pv = _dot_nt(ec.astype(BF16), vct) + _dot(en.astype(BF16), vn)
        outs.append(pv / l)
    o_ref[...] = jnp.where(head0, outs[0], outs[1])


def _attn_sample(q, kv, cache_t, layer):
    b, t, _ = q.shape
    w = cache_t.shape[-1]
    assert all(d & (d - 1) == 0 for d in ATT_DILATIONS) and t <= LANES
    npair = ATT_WIDTH // LANES
    vmem = 2 * 2 * _nbytes((LANES, w), F32) + 10 * _nbytes((LANES, w), F32) + 4 * MIB
    new_spec = lambda off: pl.BlockSpec((None, t, LANES), lambda i, j: (i, 0, off + j))
    buf_spec = lambda c: pl.BlockSpec((None, None, None, 2, HEAD_DIM, w),
                                      lambda i, j: (layer, i, c, j, 0, 0))
    return pl.pallas_call(
        _attn_sample_kernel,
        out_shape=jax.ShapeDtypeStruct((b, t, ATT_WIDTH), F32),
        grid=(b, npair),
        in_specs=[new_spec(0), new_spec(0), new_spec(npair), buf_spec(0), buf_spec(1)],
        out_specs=new_spec(0),
        compiler_params=_params(("parallel", "parallel"), vmem),
        name="attn_sample",
    )(q, kv, kv, cache_t, cache_t)


def _ssm_prep_kernel(are_ref, aim_ref, ldt_ref, bre_ref, bim_ref, abre_ref, abim_ref, bbre_ref, bbim_ref):
    a_re, a_im = are_ref[...], aim_ref[...]
    dt = jnp.exp(ldt_ref[...])
    mag = jnp.exp(a_re * dt)
    ang = a_im * dt
    ab_re, ab_im = mag * jnp.cos(ang), mag * jnp.sin(ang)
    den = a_re * a_re + a_im * a_im
    nr, ni = ab_re - 1.0, ab_im
    cf_re = (nr * a_re + ni * a_im) / den
    cf_im = (ni * a_re - nr * a_im) / den
    abre_ref[...] = ab_re
    abim_ref[...] = ab_im
    for c in range(SSM_GROUP):
        b_re, b_im = bre_ref[c], bim_ref[c]
        bbre_ref[c] = cf_re * b_re - cf_im * b_im
        bbim_ref[c] = cf_re * b_im + cf_im * b_re


def _ssm_prep(a_re, a_im, log_dt, b_re, b_im):
    depth, g, n = a_re.shape
    rows = depth * g
    flat = lambda a: a.reshape(rows, n)
    chan_major = lambda b: jnp.transpose(b.reshape(rows, n, SSM_GROUP), (2, 0, 1))
    ab_re, ab_im, bb_re, bb_im = pl.pallas_call(
        _ssm_prep_kernel,
        out_shape=(jax.ShapeDtypeStruct((rows, n), F32), jax.ShapeDtypeStruct((rows, n), F32),
                   jax.ShapeDtypeStruct((SSM_GROUP, rows, n), F32),
                   jax.ShapeDtypeStruct((SSM_GROUP, rows, n), F32)),
        name="ssm_prep",
    )(flat(a_re), flat(a_im), log_dt.reshape(rows, 1), chan_major(b_re), chan_major(b_im))
    unflat = lambda b: jnp.transpose(b, (1, 0, 2)).reshape(depth, g, SSM_GROUP, n)
    return ab_re.reshape(depth, g, n), ab_im.reshape(depth, g, n), unflat(bb_re), unflat(bb_im)


def _block_diag(m):
    g, r, c = m.shape
    nb = g // SSM_BLOCK_GROUPS
    m = m.reshape(nb, SSM_BLOCK_GROUPS, r, c)
    eye = jnp.eye(SSM_BLOCK_GROUPS, dtype=m.dtype)
    out = m[:, :, :, None, :] * eye[None, :, None, :, None]
    return out.reshape(nb, SSM_BLOCK_GROUPS * r, SSM_BLOCK_GROUPS * c)


def _ssm_kernel(u_ref, h0re_ref, h0im_ref, are_ref, aim_ref, bb_ref, cre_ref, cim_ref, d_ref,
                y_ref, hre_ref, him_ref, x_scr, up_scr, *, steps, chained):
    ns = SSM_BLOCK_STATE
    nrows = steps * SCAN_SEQS
    chunk = min(nrows, 256)

    def permute_in(t, c):
        up_scr[pl.ds(pl.multiple_of(t * SCAN_SEQS, SCAN_SEQS), SCAN_SEQS), :] = (
            u_ref[pl.ds(t, SCAN_SEQS, stride=steps), :])
        return c
    lax.fori_loop(0, steps, permute_in, 0)

    def input_map(c, carry):
        rows = pl.ds(pl.multiple_of(c * chunk, chunk), chunk)
        x_scr[rows, :] = _dot(up_scr[rows, :].astype(BF16), bb_ref[...])
        return carry
    lax.fori_loop(0, nrows // chunk, input_map, 0)

    a_re = jnp.broadcast_to(are_ref[...], (SCAN_SEQS, ns))
    a_im = jnp.broadcast_to(aim_ref[...], (SCAN_SEQS, ns))

    def scan(init, store):
        def step(t, h):
            h_re, h_im = h
            rows = pl.ds(pl.multiple_of(t * SCAN_SEQS, SCAN_SEQS), SCAN_SEQS)
            n_re = a_re * h_re - a_im * h_im + x_scr[rows, :ns]
            n_im = a_re * h_im + a_im * h_re + x_scr[rows, ns:]
            if store:
                x_scr[rows, :ns] = n_re
                x_scr[rows, ns:] = n_im
            return n_re, n_im
        return lax.fori_loop(0, steps, step, init, unroll=8 if steps >= 8 else steps)

    init = (h0re_ref[...], h0im_ref[...])
    if chained:
        assert steps & (steps - 1) == 0
        e_re, e_im = scan(init, store=False)
        p_re, p_im = a_re, a_im
        for _ in range(steps.bit_length() - 1):
            p_re, p_im = p_re * p_re - p_im * p_im, 2.0 * p_re * p_im
        sub = lax.broadcasted_iota(jnp.int32, (SCAN_SEQS, ns), 0)
        shift = 1
        while shift < SCAN_SEQS:
            r_re = pltpu.roll(e_re, shift, axis=0)
            r_im = pltpu.roll(e_im, shift, axis=0)
            keep = sub >= shift
            e_re, e_im = (e_re + jnp.where(keep, p_re * r_re - p_im * r_im, 0.0),
                          e_im + jnp.where(keep, p_re * r_im + p_im * r_re, 0.0))
            p_re, p_im = p_re * p_re - p_im * p_im, 2.0 * p_re * p_im
            shift *= 2
        first = sub == 0
        init = (init[0] + jnp.where(first, 0.0, pltpu.roll(e_re, 1, axis=0)),
                init[1] + jnp.where(first, 0.0, pltpu.roll(e_im, 1, axis=0)))
    f_re, f_im = scan(init, store=True)
    hre_ref[...] = f_re
    him_ref[...] = f_im

    def output_map(c, carry):
        rows = pl.ds(pl.multiple_of(c * chunk, chunk), chunk)
        y = (_dot(x_scr[rows, :ns].astype(BF16), cre_ref[...])
             - _dot(x_scr[rows, ns:].astype(BF16), cim_ref[...])
             + d_ref[...] * up_scr[rows, :])
        up_scr[rows, :] = _gelu_tanh(y)
        return carry
    lax.fori_loop(0, nrows // chunk, output_map, 0)

    def permute_out(t, c):
        y_ref[pl.ds(t, SCAN_SEQS, stride=steps), :] = (
            up_scr[pl.ds(pl.multiple_of(t * SCAN_SEQS, SCAN_SEQS), SCAN_SEQS), :])
        return c
    lax.fori_loop(0, steps, permute_out, 0)


def _ssm(u, h0_re, h0_im, ab_re, ab_im, bb, c_re, c_im, d_skip, *, chained):
    nb, rows, _ = u.shape
    steps = rows // SCAN_SEQS
    nblk = SSM_WIDTH // LANES
    ns = SSM_BLOCK_STATE
    vmem = (2 * 2 * _nbytes((rows, LANES), F32) + _nbytes((rows, 2 * ns), F32)
            + _nbytes((rows, LANES), F32) + 4 * _nbytes((min(rows, 256), 2 * ns), F32) + 8 * MIB)
    state_spec = pl.BlockSpec((None, SCAN_SEQS, ns), lambda i, j: (i, 0, j))
    return pl.pallas_call(
        functools.partial(_ssm_kernel, steps=steps, chained=chained),
        out_shape=(jax.ShapeDtypeStruct((nb, rows, SSM_WIDTH), F32),
                   jax.ShapeDtypeStruct(h0_re.shape, F32),
                   jax.ShapeDtypeStruct(h0_im.shape, F32)),
        grid=(nb, nblk),
        in_specs=[pl.BlockSpec((None, rows, LANES), lambda i, j: (i, 0, j)),
                  state_spec, state_spec,
                  pl.BlockSpec((1, ns), lambda i, j: (0, j)),
                  pl.BlockSpec((1, ns), lambda i, j: (0, j)),
                  pl.BlockSpec((None, LANES, 2 * ns), lambda i, j: (j, 0, 0)),
                  pl.BlockSpec((None, ns, LANES), lambda i, j: (j, 0, 0)),
                  pl.BlockSpec((None, ns, LANES), lambda i, j: (j, 0, 0)),
                  pl.BlockSpec((1, LANES), lambda i, j: (0, j))],
        out_specs=(pl.BlockSpec((None, rows, LANES), lambda i, j: (i, 0, j)), state_spec, state_spec),
        scratch_shapes=[pltpu.VMEM((rows, 2 * ns), F32), pltpu.VMEM((rows, LANES), F32)],
        compiler_params=_params(("parallel", "parallel"), vmem),
        name="ssm_chained" if chained else "ssm_step",
    )(u, h0_re, h0_im, ab_re, ab_im, bb, c_re, c_im, d_skip)


def _conv_kernel(h_ref, gb_ref, gc_ref, w_ref, pre_ref, y_ref, st_ref):
    n = h_ref.shape[0]
    u = gc_ref[...] * h_ref[...]
    row = lax.broadcasted_iota(jnp.int32, u.shape, 0)
    pre = pre_ref[...]
    p0, p1 = pre[0:1, :], pre[1:2, :]
    u1 = jnp.where(row == 0, p1, pltpu.roll(u, 1, axis=0))
    u2 = jnp.where(row == 0, p0, jnp.where(row == 1, p1, pltpu.roll(u, 2, axis=0)))
    w = w_ref[...]
    y = w[0:1, :] * u2 + w[1:2, :] * u1 + w[2:3, :] * u
    y_ref[...] = gb_ref[...] * y
    st_ref[...] = u[n - (CONV_K - 1):, :]


def _conv(rest, w, prefix):
    b, n, _ = rest.shape
    assert n >= CONV_K - 1
    nblk = CONV_CH // LANES
    off = SSM_WIDTH // LANES
    col = lambda part: pl.BlockSpec((None, n, LANES), lambda i, j: (i, 0, off + part * nblk + j))
    vmem = 2 * 4 * _nbytes((n, LANES), F32) + 8 * _nbytes((n, LANES), F32) + 4 * MIB
    return pl.pallas_call(
        _conv_kernel,
        out_shape=(jax.ShapeDtypeStruct((b, n, CONV_CH), F32),
                   jax.ShapeDtypeStruct((b, CONV_K - 1, CONV_CH), F32)),
        grid=(b, nblk),
        in_specs=[col(0), col(1), col(2),
                  pl.BlockSpec((CONV_K, LANES), lambda i, j: (0, j)),
                  pl.BlockSpec((None, CONV_K - 1, LANES), lambda i, j: (i, 0, j))],
        out_specs=(pl.BlockSpec((None, n, LANES), lambda i, j: (i, 0, j)),
                   pl.BlockSpec((None, CONV_K - 1, LANES), lambda i, j: (i, 0, j))),
        compiler_params=_params(("parallel", "parallel"), vmem),
        name="conv",
    )(rest, rest, rest, w, prefix)


def _out_proj_kernel(oatt_ref, yssm_ref, yconv_ref, g_ref, wglu_ref, bglu_ref, w_ref, x_ref,
                     o_ref, mix_ref):
    j = pl.program_id(1)
    a0, a1 = ATT_WIDTH, ATT_WIDTH + SSM_WIDTH

    @pl.when(j == 0)
    def _():
        g = g_ref[...]
        mix_ref[:, :a0] = _rms(oatt_ref[...], g[:, :a0]).astype(BF16)
        ys = yssm_ref[...]
        ys = ys * _sigmoid(_dot(ys.astype(BF16), wglu_ref[...]) + bglu_ref[...])
        mix_ref[:, a0:a1] = _rms(ys, g[:, a0:a1]).astype(BF16)
        mix_ref[:, a1:] = _rms(yconv_ref[...], g[:, a1:]).astype(BF16)

    o_ref[...] = x_ref[...] + _dot(mix_ref[...], w_ref[...])


def _out_proj(o_att, y_ssm, y_conv, g, w_glu, b_glu, w, layer, x, *, tm, tn=512):
    m, d = x.shape
    k = w.shape[1]
    vmem = (2 * (_nbytes((tm, k), F32) + _nbytes((k, tn), BF16) + 2 * _nbytes((tm, tn), F32)
                 + _nbytes((SSM_WIDTH, SSM_WIDTH), BF16))
            + _nbytes((tm, k), BF16) + 2 * _nbytes((tm, k), F32) + 4 * MIB)
    row = lambda c: pl.BlockSpec((tm, c), lambda i, j: (i, 0))
    const = lambda r, c: pl.BlockSpec((r, c), lambda i, j: (0, 0))
    return pl.pallas_call(
        _out_proj_kernel,
        out_shape=jax.ShapeDtypeStruct((m, d), F32),
        grid=(m // tm, d // tn),
        in_specs=[row(ATT_WIDTH), row(SSM_WIDTH), row(CONV_CH), const(1, k),
                  pl.BlockSpec((None, SSM_WIDTH, SSM_WIDTH), lambda i, j: (layer, 0, 0)),
                  const(1, SSM_WIDTH),
                  pl.BlockSpec((None, k, tn), lambda i, j: (layer, 0, j)),
                  pl.BlockSpec((tm, tn), lambda i, j: (i, j))],
        out_specs=pl.BlockSpec((tm, tn), lambda i, j: (i, j)),
        scratch_shapes=[pltpu.VMEM((tm, k), BF16)],
        compiler_params=_params(("parallel", "arbitrary"), vmem),
        name="out_proj",
    )(o_att, y_ssm, y_conv, g.reshape(1, k), w_glu, b_glu.reshape(1, SSM_WIDTH), w, x)


def _ffn_kernel(x_ref, g_ref, wg_ref, wu_ref, wd_ref, o_ref, xn_ref):
    f = pl.program_id(1)

    @pl.when(f == 0)
    def _():
        x = x_ref[...]
        xn_ref[...] = _rms(x, g_ref[...]).astype(BF16)
        o_ref[...] = x

    xn = xn_ref[...]
    gate = _dot(xn, wg_ref[...])
    up = _dot(xn, wu_ref[...])
    act = (gate * _sigmoid(gate) * up).astype(BF16)
    o_ref[...] += _dot(act, wd_ref[...])


def _ffn(x, g, w_gate, w_up, w_down, layer, *, tm, tf=512):
    m, d = x.shape
    ff = w_gate.shape[2]
    vmem = (2 * (2 * _nbytes((tm, d), F32) + 3 * _nbytes((d, tf), BF16))
            + _nbytes((tm, d), BF16) + 2 * _nbytes((tm, d), F32) + 4 * _nbytes((tm, tf), F32) + 4 * MIB)
    return pl.pallas_call(
        _ffn_kernel,
        out_shape=jax.ShapeDtypeStruct((m, d), F32),
        grid=(m // tm, ff // tf),
        in_specs=[pl.BlockSpec((tm, d), lambda i, f: (i, 0)),
                  pl.BlockSpec((1, d), lambda i, f: (0, 0)),
                  pl.BlockSpec((None, d, tf), lambda i, f: (layer, 0, f)),
                  pl.BlockSpec((None, d, tf), lambda i, f: (layer, 0, f)),
                  pl.BlockSpec((None, tf, d), lambda i, f: (layer, f, 0))],
        out_specs=pl.BlockSpec((tm, d), lambda i, f: (i, 0)),
        scratch_shapes=[pltpu.VMEM((tm, d), BF16)],
        compiler_params=_params(("parallel", "arbitrary"), vmem),
        name="ffn",
    )(x, g.reshape(1, d), w_gate, w_up, w_down)


def _final_norm_kernel(x_ref, g_ref, o_ref):
    o_ref[...] = _rms(x_ref[...], g_ref[...])


def _final_norm(x, g, *, tm):
    m, d = x.shape
    vmem = 2 * 2 * _nbytes((tm, d), F32) + 3 * _nbytes((tm, d), F32) + 4 * MIB
    return pl.pallas_call(
        _final_norm_kernel,
        out_shape=jax.ShapeDtypeStruct((m, d), F32),
        grid=(m // tm,),
        in_specs=[pl.BlockSpec((tm, d), lambda i: (i, 0)), pl.BlockSpec((1, d), lambda i: (0, 0))],
        out_specs=pl.BlockSpec((tm, d), lambda i: (i, 0)),
        compiler_params=_params(("parallel",), vmem),
        name="final_norm",
    )(x, g.reshape(1, d))


def _layer(x, lw, layer, *, batch, cache_t, conv_prefix, ssm_h0, tm):
    m, d = x.shape
    n = m // batch
    q, kv, rest, kvt = _in_proj(x, lw['g_mix_norm'], lw['w_in'], layer, tm=tm, seq=n)
    q3, kv3, rest3 = (a.reshape(batch, n, a.shape[-1]) for a in (q, kv, rest))
    n_heads = ATT_WIDTH // HEAD_DIM
    if kvt is None:
        kv_out = kv3.reshape(batch, n, 2, n_heads, HEAD_DIM)
    else:
        kv_out = jnp.transpose(kvt.reshape(batch, 2, n_heads, HEAD_DIM, n), (0, 4, 1, 2, 3))
    ssm_w = (lw['ab_re'], lw['ab_im'], lw['bb'], lw['c_re'], lw['c_im'], lw['ssm_d'])
    if cache_t is None:
        o_att = _attn_prompt(q3, kv3)
        y_ssm, h_re, h_im = _ssm(rest3, *ssm_h0, *ssm_w, chained=True)
        h_re, h_im = h_re[:, SCAN_SEQS - 1], h_im[:, SCAN_SEQS - 1]
    else:
        o_att = _attn_sample(q3, kv3, cache_t, layer)
        y_ssm, h_re, h_im = _ssm(rest.reshape(1, m, rest.shape[-1]), *ssm_h0, *ssm_w, chained=False)
        h_re, h_im = h_re[0], h_im[0]
    y_conv, conv_state = _conv(rest3, lw['conv_w'], conv_prefix)
    x = _out_proj(o_att.reshape(m, ATT_WIDTH), y_ssm.reshape(m, SSM_WIDTH), y_conv.reshape(m, CONV_CH),
                  lw['g_out'], lw['ssm_w_glu'], lw['ssm_b_glu'], lw['w_out'], layer, x, tm=min(tm, 512))
    x = _ffn(x, lw['g_ffn_norm'], lw['w_gate'], lw['w_up'], lw['w_down'], layer, tm=min(tm, 512))
    return x, kv_out, conv_state, h_re, h_im


def kernel(x_prompt, x_sample, cache_kv, state_conv, state_ssm_re, state_ssm_im, g_mix_norm, w_in,
           ssm_a_re, ssm_a_im, ssm_log_dt, ssm_b_re, ssm_b_im, ssm_c_re, ssm_c_im, ssm_d, ssm_w_glu,
           ssm_b_glu, conv_w, g_out, w_out, g_ffn_norm, w_gate, w_up, w_down, g_final):
    batch, seq, d_model = x_prompt.shape
    dec_batch, dec_seq, _ = x_sample.shape
    depth, n_groups, n_state = ssm_a_re.shape
    n_heads = ATT_WIDTH // HEAD_DIM
    assert dec_batch == SCAN_SEQS and seq % SCAN_SEQS == 0

    ab_re, ab_im, bb_re, bb_im = _ssm_prep(ssm_a_re, ssm_a_im, ssm_log_dt, ssm_b_re, ssm_b_im)
    stacked = {'w_in': w_in.astype(BF16), 'ssm_w_glu': ssm_w_glu.astype(BF16), 'w_out': w_out.astype(BF16),
               'w_gate': w_gate.astype(BF16), 'w_up': w_up.astype(BF16), 'w_down': w_down.astype(BF16)}
    cache_t = jnp.transpose(cache_kv, (0, 1, 3, 4, 5, 2))

    xp = x_prompt.reshape(batch * seq, d_model)
    xs = x_sample.reshape(dec_batch * dec_seq, d_model)
    zero_state = jnp.zeros((batch, SCAN_SEQS, n_groups * n_state), F32)
    zero_prefix = jnp.zeros((batch, CONV_K - 1, CONV_CH), F32)

    outs_p, outs_s = [], []
    for i in range(depth):
        bb = jnp.concatenate([_block_diag(bb_re[i]), _block_diag(bb_im[i])], axis=-1).astype(BF16)
        c_t = lambda c: _block_diag(jnp.transpose(c[i], (0, 2, 1))).astype(BF16)
        lw = dict(stacked)
        lw.update({
            'g_mix_norm': g_mix_norm[i],
            'ab_re': ab_re[i].reshape(1, -1), 'ab_im': ab_im[i].reshape(1, -1), 'bb': bb,
            'c_re': c_t(ssm_c_re), 'c_im': c_t(ssm_c_im), 'ssm_d': ssm_d[i].reshape(1, -1),
            'ssm_b_glu': ssm_b_glu[i], 'conv_w': conv_w[i], 'g_out': g_out[i],
            'g_ffn_norm': g_ffn_norm[i],
        })
        xp, kv, c_state, h_re, h_im = _layer(
            xp, lw, i, batch=batch, cache_t=None, conv_prefix=zero_prefix,
            ssm_h0=(zero_state, zero_state), tm=1024)
        n_keep = min(max(ATT_DILATIONS) * ATT_SPAN, seq)
        outs_p.append((kv[:, seq - n_keep:], c_state,
                       h_re.reshape(batch, n_groups, n_state), h_im.reshape(batch, n_groups, n_state)))

        h0 = (state_ssm_re[i].reshape(1, dec_batch, -1), state_ssm_im[i].reshape(1, dec_batch, -1))
        xs, kv, c_state, h_re, h_im = _layer(
            xs, lw, i, batch=dec_batch, cache_t=cache_t, conv_prefix=state_conv[i], ssm_h0=h0,
            tm=dec_batch * dec_seq)
        outs_s.append((kv, c_state,
                       h_re.reshape(dec_batch, n_groups, n_state),
                       h_im.reshape(dec_batch, n_groups, n_state)))

    y_prompt = _final_norm(xp, g_final, tm=1024).reshape(batch, seq, d_model)
    y_sample = _final_norm(xs, g_final, tm=dec_batch * dec_seq).reshape(dec_batch, dec_seq, d_model)
    stack = lambda outs, k: jnp.stack([o[k] for o in outs])
    return (y_prompt, y_sample,
            stack(outs_p, 0), stack(outs_p, 1), stack(outs_p, 2), stack(outs_p, 3),
            stack(outs_s, 0), stack(outs_s, 1), stack(outs_s, 2), stack(outs_s, 3))
```

```python
import functools
import math

import jax
import jax.numpy as jnp
from jax import lax
from jax.experimental import pallas as pl
from jax.experimental.pallas import tpu as pltpu

F32 = jnp.float32
BF16 = jnp.bfloat16

HEAD_DIM = 64
ATT_WIDTH = 1024
SSM_WIDTH = 512
CONV_CH = 512
CONV_K = 3
SSM_GROUP = 16
SSM_STATE = 64
ATT_DILATIONS = (1, 4, 16)
ATT_SPAN = 128
RMS_EPS = 1e-6
NEG = -1e30

LANES = 128
SUBLANES = 8
SCAN_SEQS = SUBLANES
SSM_BLOCK_GROUPS = LANES // SSM_GROUP
SSM_BLOCK_STATE = SSM_BLOCK_GROUPS * SSM_STATE

MIB = 1024 * 1024


def _params(semantics, vmem_bytes):
    return pltpu.CompilerParams(dimension_semantics=semantics, vmem_limit_bytes=int(vmem_bytes))


def _nbytes(shape, dtype):
    return math.prod(shape) * jnp.dtype(dtype).itemsize


def _rms(x, g):
    ms = jnp.mean(x * x, axis=-1, keepdims=True)
    return x * lax.rsqrt(ms + RMS_EPS) * g


def _sigmoid(x):
    return 1.0 / (1.0 + jnp.exp(-x))


def _gelu_tanh(x):
    c = math.sqrt(2.0 / math.pi)
    return 0.5 * x * (1.0 + jnp.tanh(c * (x + 0.044715 * (x * x * x))))


def _dot(a, b):
    return jnp.dot(a, b, preferred_element_type=F32)


def _dot_nt(a, b):
    return lax.dot_general(a, b, (((1,), (1,)), ((), ())), preferred_element_type=F32)


def _in_proj_kernel(x_ref, g_ref, w_ref, q_ref, kv_ref, rest_ref, *more, nq, nkv):
    kvt_ref = more[0] if len(more) == 2 else None
    xn_ref = more[-1]
    j = pl.program_id(1)

    @pl.when(j == 0)
    def _():
        xn_ref[...] = _rms(x_ref[...], g_ref[...]).astype(BF16)

    @pl.when(j < nq)
    def _():
        q_ref[...] = _dot(xn_ref[...], w_ref[...])

    @pl.when((j >= nq) & (j < nq + nkv))
    def _():
        z = _dot(xn_ref[...], w_ref[...])
        kv_ref[...] = z
        if kvt_ref is not None:
            kvt_ref[...] = z.T

    @pl.when(j >= nq + nkv)
    def _():
        rest_ref[...] = _dot(xn_ref[...], w_ref[...])


def _in_proj(x, g, w, layer, *, tm, seq, tn=512):
    m, k = x.shape
    n = w.shape[2]
    nq, nkv = ATT_WIDTH // tn, 2 * ATT_WIDTH // tn
    nrest = n // tn - nq - nkv
    with_kvt = seq % tm == 0 and tm % LANES == 0
    tiles_per_seq = seq // tm if with_kvt else 1
    vmem = (2 * (_nbytes((tm, k), F32) + _nbytes((k, tn), BF16) + 4 * _nbytes((tm, tn), F32))
            + _nbytes((tm, k), BF16) + 2 * _nbytes((tm, k), F32) + 2 * _nbytes((tm, tn), F32) + 4 * MIB)
    kv_col = lambda j: jnp.clip(j - nq, 0, nkv - 1)
    out_shape = [jax.ShapeDtypeStruct((m, nq * tn), F32),
                 jax.ShapeDtypeStruct((m, nkv * tn), F32),
                 jax.ShapeDtypeStruct((m, nrest * tn), F32)]
    out_specs = [pl.BlockSpec((tm, tn), lambda i, j: (i, jnp.minimum(j, nq - 1))),
                 pl.BlockSpec((tm, tn), lambda i, j: (i, kv_col(j))),
                 pl.BlockSpec((tm, tn), lambda i, j: (i, jnp.clip(j - nq - nkv, 0, nrest - 1)))]
    if with_kvt:
        out_shape.append(jax.ShapeDtypeStruct((m // seq, nkv * tn, seq), F32))
        out_specs.append(pl.BlockSpec(
            (None, tn, tm), lambda i, j: (i // tiles_per_seq, kv_col(j), i % tiles_per_seq)))
    outs = pl.pallas_call(
        functools.partial(_in_proj_kernel, nq=nq, nkv=nkv),
        out_shape=tuple(out_shape),
        grid=(m // tm, n // tn),
        in_specs=[pl.BlockSpec((tm, k), lambda i, j: (i, 0)),
                  pl.BlockSpec((1, k), lambda i, j: (0, 0)),
                  pl.BlockSpec((None, k, tn), lambda i, j: (layer, 0, j))],
        out_specs=tuple(out_specs),
        scratch_shapes=[pltpu.VMEM((tm, k), BF16)],
        compiler_params=_params(("parallel", "arbitrary"), vmem),
        name="in_proj",
    )(x, g.reshape(1, k), w)
    return outs if with_kvt else (*outs, None)


def _attn_prompt_kernel(q_ref, k_ref, v_ref, o_ref, ob_ref, lb_ref, bias_ref):
    s_len = q_ref.shape[0]
    blk = ATT_SPAN
    lane = lax.broadcasted_iota(jnp.int32, (blk, LANES), 1)
    head0 = lane < HEAD_DIM
    scale = HEAD_DIM ** -0.5

    qi = lax.broadcasted_iota(jnp.int32, (2 * blk, 3 * blk), 0) & (blk - 1)
    kj = lax.broadcasted_iota(jnp.int32, (2 * blk, 3 * blk), 1)
    dist = jnp.where(kj < blk, qi - kj, qi + 2 * blk - kj)
    bias_ref[...] = jnp.where((dist >= 0) & (dist <= ATT_SPAN), 0.0, NEG)

    def block(p, d, q0, k0=None):
        def rows(start, n):
            return pl.ds(start, n, stride=d) if d > 1 else pl.ds(start, n)

        qt = q_ref[rows(q0, blk), :] * scale
        qq = jnp.concatenate([jnp.where(head0, qt, 0.0), jnp.where(head0, 0.0, qt)], axis=0).astype(BF16)
        if k0 is None:
            key_rows, bias = rows(q0, blk), bias_ref[:, :blk]
        else:
            key_rows, bias = rows(k0, 2 * blk), bias_ref[:, blk:]
        k = k_ref[key_rows, :].astype(BF16)
        v = v_ref[key_rows, :].astype(BF16)
        s = _dot_nt(qq, k) + bias
        m = jnp.max(s, axis=-1, keepdims=True)
        e = jnp.exp(s - m)
        l = jnp.sum(e, axis=-1, keepdims=True)
        o = _dot(e.astype(BF16), v) * (1.0 / l)
        lse = jnp.broadcast_to(m + jnp.log(l), (2 * blk, LANES))
        ob_ref[p, rows(q0, blk), :] = jnp.where(head0, o[:blk], o[blk:])
        lb_ref[p, rows(q0, blk), :] = jnp.where(head0, lse[:blk], lse[blk:])

    for p, d in enumerate(ATT_DILATIONS):
        nblk = s_len // (d * blk)
        for r in range(d):
            for b in range(nblk):
                block(p, d, r + d * blk * b, None if b == 0 else r + d * blk * (b - 1))

    def merge(c, carry):
        rows = pl.ds(pl.multiple_of(c * blk, blk), blk)
        l0, l1, l2 = lb_ref[0, rows, :], lb_ref[1, rows, :], lb_ref[2, rows, :]
        m = jnp.maximum(jnp.maximum(l0, l1), l2)
        e0, e1, e2 = jnp.exp(l0 - m), jnp.exp(l1 - m), jnp.exp(l2 - m)
        num = e0 * ob_ref[0, rows, :] + e1 * ob_ref[1, rows, :] + e2 * ob_ref[2, rows, :]
        o_ref[rows, :] = num / (e0 + e1 + e2)
        return carry

    lax.fori_loop(0, s_len // blk, merge, 0)


def _attn_prompt(q, kv):
    b, s, _ = q.shape
    assert s % (max(ATT_DILATIONS) * ATT_SPAN) == 0
    npair = ATT_WIDTH // LANES
    blk_bytes = _nbytes((s, LANES), F32)
    vmem = 2 * 4 * blk_bytes + 6 * blk_bytes + 4 * MIB
    return pl.pallas_call(
        _attn_prompt_kernel,
        out_shape=jax.ShapeDtypeStruct((b, s, ATT_WIDTH), F32),
        grid=(b, npair),
        in_specs=[pl.BlockSpec((None, s, LANES), lambda i, j: (i, 0, j)),
                  pl.BlockSpec((None, s, LANES), lambda i, j: (i, 0, j)),
                  pl.BlockSpec((None, s, LANES), lambda i, j: (i, 0, npair + j))],
        out_specs=pl.BlockSpec((None, s, LANES), lambda i, j: (i, 0, j)),
        scratch_shapes=[pltpu.VMEM((len(ATT_DILATIONS), s, LANES), F32),
                        pltpu.VMEM((len(ATT_DILATIONS), s, LANES), F32),
                        pltpu.VMEM((2 * ATT_SPAN, 3 * ATT_SPAN), F32)],
        compiler_params=_params(("parallel", "parallel"), vmem),
        name="attn_prompt",
    )(q, kv, kv)


def _attn_sample_kernel(q_ref, kn_ref, vn_ref, kct_ref, vct_ref, o_ref):
    t_len = q_ref.shape[0]
    w_len = kct_ref.shape[-1]
    pad = LANES
    q = q_ref[...] * (HEAD_DIM ** -0.5)
    zpad = jnp.zeros((pad - t_len, LANES), F32)
    kn = jnp.concatenate([kn_ref[...], zpad], axis=0).astype(BF16)
    vn = jnp.concatenate([vn_ref[...], zpad], axis=0).astype(BF16)
    kct = kct_ref[...].reshape(LANES, w_len).astype(BF16)
    vct = vct_ref[...].reshape(LANES, w_len).astype(BF16)

    def multiplicity(dist):
        c = jnp.zeros(dist.shape, F32)
        for d in ATT_DILATIONS:
            ok = (dist >= 0) & ((dist & (d - 1)) == 0) & (dist <= ATT_SPAN * d)
            c = c + jnp.where(ok, 1.0, 0.0)
        return c

    tc = lax.broadcasted_iota(jnp.int32, (t_len, w_len), 0)
    jc = lax.broadcasted_iota(jnp.int32, (t_len, w_len), 1)
    cnt_c = multiplicity(w_len + tc - jc)
    tn = lax.broadcasted_iota(jnp.int32, (t_len, pad), 0)
    jn = lax.broadcasted_iota(jnp.int32, (t_len, pad), 1)
    cnt_n = multiplicity(jnp.where(jn < t_len, tn - jn, -1))

    lane = lax.broadcasted_iota(jnp.int32, (t_len, LANES), 1)
    head0 = lane < HEAD_DIM
    outs = []
    for h in range(2):
        hm = head0 if h == 0 else jnp.logical_not(head0)
        qh = jnp.where(hm, q, 0.0).astype(BF16)
        sc = jnp.where(cnt_c > 0, _dot(qh, kct), NEG)
        sn = jnp.where(cnt_n > 0, _dot_nt(qh, kn), NEG)
        m = jnp.maximum(jnp.max(sc, axis=-1, keepdims=True), jnp.max(sn, axis=-1, keepdims=True))
        ec = cnt_c * jnp.exp(sc - m)
        en = cnt_n * jnp.exp(sn - m)
        l = jnp.sum(ec, axis=-1, keepdims=True) + jnp.sum(en, axis=-1, keepdims=True)
        pv = _dot_nt(ec.astype(BF16), vct) + _dot(en.astype(BF16), vn)
        outs.append(pv / l)
    o_ref[...] = jnp.where(head0, outs[0], outs[1])


def _attn_sample(q, kv, cache_t, layer):
    b, t, _ = q.shape
    w = cache_t.shape[-1]
    assert all(d & (d - 1) == 0 for d in ATT_DILATIONS) and t <= LANES
    npair = ATT_WIDTH // LANES
    vmem = 2 * 2 * _nbytes((LANES, w), F32) + 10 * _nbytes((LANES, w), F32) + 4 * MIB
    new_spec = lambda off: pl.BlockSpec((None, t, LANES), lambda i, j: (i, 0, off + j))
    buf_spec = lambda c: pl.BlockSpec((None, None, None, 2, HEAD_DIM, w),
                                      lambda i, j: (layer, i, c, j, 0, 0))
    return pl.pallas_call(
        _attn_sample_kernel,
        out_shape=jax.ShapeDtypeStruct((b, t, ATT_WIDTH), F32),
        grid=(b, npair),
        in_specs=[new_spec(0), new_spec(0), new_spec(npair), buf_spec(0), buf_spec(1)],
        out_specs=new_spec(0),
        compiler_params=_params(("parallel", "parallel"), vmem),
        name="attn_sample",
    )(q, kv, kv, cache_t, cache_t)


def _ssm_prep_kernel(are_ref, aim_ref, ldt_ref, bre_ref, bim_ref, abre_ref, abim_ref, bbre_ref, bbim_ref):
    a_re, a_im = are_ref[...], aim_ref[...]
    dt = jnp.exp(ldt_ref[...])
    mag = jnp.exp(a_re * dt)
    ang = a_im * dt
    ab_re, ab_im = mag * jnp.cos(ang), mag * jnp.sin(ang)
    den = a_re * a_re + a_im * a_im
    nr, ni = ab_re - 1.0, ab_im
    cf_re = (nr * a_re + ni * a_im) / den
    cf_im = (ni * a_re - nr * a_im) / den
    abre_ref[...] = ab_re
    abim_ref[...] = ab_im
    for c in range(SSM_GROUP):
        b_re, b_im = bre_ref[c], bim_ref[c]
        bbre_ref[c] = cf_re * b_re - cf_im * b_im
        bbim_ref[c] = cf_re * b_im + cf_im * b_re


def _ssm_prep(a_re, a_im, log_dt, b_re, b_im):
    depth, g, n = a_re.shape
    rows = depth * g
    flat = lambda a: a.reshape(rows, n)
    chan_major = lambda b: jnp.transpose(b.reshape(rows, n, SSM_GROUP), (2, 0, 1))
    ab_re, ab_im, bb_re, bb_im = pl.pallas_call(
        _ssm_prep_kernel,
        out_shape=(jax.ShapeDtypeStruct((rows, n), F32), jax.ShapeDtypeStruct((rows, n), F32),
                   jax.ShapeDtypeStruct((SSM_GROUP, rows, n), F32),
                   jax.ShapeDtypeStruct((SSM_GROUP, rows, n), F32)),
        name="ssm_prep",
    )(flat(a_re), flat(a_im), log_dt.reshape(rows, 1), chan_major(b_re), chan_major(b_im))
    unflat = lambda b: jnp.transpose(b, (1, 0, 2)).reshape(depth, g, SSM_GROUP, n)
    return ab_re.reshape(depth, g, n), ab_im.reshape(depth, g, n), unflat(bb_re), unflat(bb_im)


def _block_diag(m):
    g, r, c = m.shape
    nb = g // SSM_BLOCK_GROUPS
    m = m.reshape(nb, SSM_BLOCK_GROUPS, r, c)
    eye = jnp.eye(SSM_BLOCK_GROUPS, dtype=m.dtype)
    out = m[:, :, :, None, :] * eye[None, :, None, :, None]
    return out.reshape(nb, SSM_BLOCK_GROUPS * r, SSM_BLOCK_GROUPS * c)


def _ssm_kernel(u_ref, h0re_ref, h0im_ref, are_ref, aim_ref, bb_ref, cre_ref, cim_ref, d_ref,
                y_ref, hre_ref, him_ref, x_scr, up_scr, *, steps, chained):
    ns = SSM_BLOCK_STATE
    nrows = steps * SCAN_SEQS
    chunk = min(nrows, 256)

    nchunks = nrows // chunk
    chunk_steps = chunk // SCAN_SEQS
    unroll = min(steps, 8)

    def permute_in(t, c):
        up_scr[pl.ds(pl.multiple_of(t * SCAN_SEQS, SCAN_SEQS), SCAN_SEQS), :] = (
            u_ref[pl.ds(t, SCAN_SEQS, stride=steps), :])
        return c
    lax.fori_loop(0, steps, permute_in, 0, unroll=unroll)

    a_re = jnp.broadcast_to(are_ref[...], (SCAN_SEQS, ns))
    a_im = jnp.broadcast_to(aim_ref[...], (SCAN_SEQS, ns))

    def input_map(c):
        rows = pl.ds(c * chunk, chunk)
        x_scr[rows, :] = _dot(up_scr[rows, :].astype(BF16), bb_ref[...])

    def scan_chunk(c, h, store):
        h_re, h_im = h
        for t in range(c * chunk_steps, (c + 1) * chunk_steps):
            rows = pl.ds(t * SCAN_SEQS, SCAN_SEQS)
            h_re, h_im = (a_re * h_re - a_im * h_im + x_scr[rows, :ns],
                          a_re * h_im + a_im * h_re + x_scr[rows, ns:])
            if store:
                x_scr[rows, :ns] = h_re
                x_scr[rows, ns:] = h_im
        return h_re, h_im

    def output_map(c):
        rows = pl.ds(c * chunk, chunk)
        y = (_dot(x_scr[rows, :ns].astype(BF16), cre_ref[...])
             - _dot(x_scr[rows, ns:].astype(BF16), cim_ref[...])
             + d_ref[...] * up_scr[rows, :])
        up_scr[rows, :] = _gelu_tanh(y)

    init = (h0re_ref[...], h0im_ref[...])
    input_map(0)
    if chained:
        assert steps & (steps - 1) == 0
        e_re, e_im = init
        for c in range(nchunks):
            if c + 1 < nchunks:
                input_map(c + 1)
            e_re, e_im = scan_chunk(c, (e_re, e_im), store=False)
        p_re, p_im = a_re, a_im
        for _ in range(steps.bit_length() - 1):
            p_re, p_im = p_re * p_re - p_im * p_im, 2.0 * p_re * p_im
        sub = lax.broadcasted_iota(jnp.int32, (SCAN_SEQS, ns), 0)
        shift = 1
        while shift < SCAN_SEQS:
            r_re = pltpu.roll(e_re, shift, axis=0)
            r_im = pltpu.roll(e_im, shift, axis=0)
            keep = sub >= shift
            e_re, e_im = (e_re + jnp.where(keep, p_re * r_re - p_im * r_im, 0.0),
                          e_im + jnp.where(keep, p_re * r_im + p_im * r_re, 0.0))
            p_re, p_im = p_re * p_re - p_im * p_im, 2.0 * p_re * p_im
            shift *= 2
        first = sub == 0
        init = (init[0] + jnp.where(first, 0.0, pltpu.roll(e_re, 1, axis=0)),
                init[1] + jnp.where(first, 0.0, pltpu.roll(e_im, 1, axis=0)))
    else:
        for c in range(1, nchunks):
            input_map(c)
    h = init
    for c in range(nchunks):
        h = scan_chunk(c, h, store=True)
        if c > 0:
            output_map(c - 1)
    output_map(nchunks - 1)
    hre_ref[...] = h[0]
    him_ref[...] = h[1]

    def permute_out(t, c):
        y_ref[pl.ds(t, SCAN_SEQS, stride=steps), :] = (
            up_scr[pl.ds(pl.multiple_of(t * SCAN_SEQS, SCAN_SEQS), SCAN_SEQS), :])
        return c
    lax.fori_loop(0, steps, permute_out, 0, unroll=unroll)


def _ssm(u, h0_re, h0_im, ab_re, ab_im, bb, c_re, c_im, d_skip, *, chained):
    nb, rows, _ = u.shape
    steps = rows // SCAN_SEQS
    nblk = SSM_WIDTH // LANES
    ns = SSM_BLOCK_STATE
    vmem = (2 * 2 * _nbytes((rows, LANES), F32) + _nbytes((rows, 2 * ns), F32)
            + _nbytes((rows, LANES), F32) + 4 * _nbytes((min(rows, 256), 2 * ns), F32) + 8 * MIB)
    state_spec = pl.BlockSpec((None, SCAN_SEQS, ns), lambda i, j: (i, 0, j))
    return pl.pallas_call(
        functools.partial(_ssm_kernel, steps=steps, chained=chained),
        out_shape=(jax.ShapeDtypeStruct((nb, rows, SSM_WIDTH), F32),
                   jax.ShapeDtypeStruct(h0_re.shape, F32),
                   jax.ShapeDtypeStruct(h0_im.shape, F32)),
        grid=(nb, nblk),
        in_specs=[pl.BlockSpec((None, rows, LANES), lambda i, j: (i, 0, j)),
                  state_spec, state_spec,
                  pl.BlockSpec((1, ns), lambda i, j: (0, j)),
                  pl.BlockSpec((1, ns), lambda i, j: (0, j)),
                  pl.BlockSpec((None, LANES, 2 * ns), lambda i, j: (j, 0, 0)),
                  pl.BlockSpec((None, ns, LANES), lambda i, j: (j, 0, 0)),
                  pl.BlockSpec((None, ns, LANES), lambda i, j: (j, 0, 0)),
                  pl.BlockSpec((1, LANES), lambda i, j: (0, j))],
        out_specs=(pl.BlockSpec((None, rows, LANES), lambda i, j: (i, 0, j)), state_spec, state_spec),
        scratch_shapes=[pltpu.VMEM((rows, 2 * ns), F32), pltpu.VMEM((rows, LANES), F32)],
        compiler_params=_params(("parallel", "parallel"), vmem),
        name="ssm_chained" if chained else "ssm_step",
    )(u, h0_re, h0_im, ab_re, ab_im, bb, c_re, c_im, d_skip)


def _conv_kernel(h_ref, gb_ref, gc_ref, w_ref, pre_ref, y_ref, st_ref):
    n = h_ref.shape[0]
    u = gc_ref[...] * h_ref[...]
    row = lax.broadcasted_iota(jnp.int32, u.shape, 0)
    pre = pre_ref[...]
    p0, p1 = pre[0:1, :], pre[1:2, :]
    u1 = jnp.where(row == 0, p1, pltpu.roll(u, 1, axis=0))
    u2 = jnp.where(row == 0, p0, jnp.where(row == 1, p1, pltpu.roll(u, 2, axis=0)))
    w = w_ref[...]
    y = w[0:1, :] * u2 + w[1:2, :] * u1 + w[2:3, :] * u
    y_ref[...] = gb_ref[...] * y
    st_ref[...] = u[n - (CONV_K - 1):, :]


def _conv(rest, w, prefix):
    b, n, _ = rest.shape
    assert n >= CONV_K - 1
    nblk = CONV_CH // LANES
    off = SSM_WIDTH // LANES
    col = lambda part: pl.BlockSpec((None, n, LANES), lambda i, j: (i, 0, off + part * nblk + j))
    vmem = 2 * 4 * _nbytes((n, LANES), F32) + 8 * _nbytes((n, LANES), F32) + 4 * MIB
    return pl.pallas_call(
        _conv_kernel,
        out_shape=(jax.ShapeDtypeStruct((b, n, CONV_CH), F32),
                   jax.ShapeDtypeStruct((b, CONV_K - 1, CONV_CH), F32)),
        grid=(b, nblk),
        in_specs=[col(0), col(1), col(2),
                  pl.BlockSpec((CONV_K, LANES), lambda i, j: (0, j)),
                  pl.BlockSpec((None, CONV_K - 1, LANES), lambda i, j: (i, 0, j))],
        out_specs=(pl.BlockSpec((None, n, LANES), lambda i, j: (i, 0, j)),
                   pl.BlockSpec((None, CONV_K - 1, LANES), lambda i, j: (i, 0, j))),
        compiler_params=_params(("parallel", "parallel"), vmem),
        name="conv",
    )(rest, rest, rest, w, prefix)


def _out_proj_kernel(oatt_ref, yssm_ref, yconv_ref, g_ref, wglu_ref, bglu_ref, w_ref, x_ref,
                     o_ref, mix_ref):
    j = pl.program_id(1)
    a0, a1 = ATT_WIDTH, ATT_WIDTH + SSM_WIDTH

    @pl.when(j == 0)
    def _():
        g = g_ref[...]
        mix_ref[:, :a0] = _rms(oatt_ref[...], g[:, :a0]).astype(BF16)
        ys = yssm_ref[...]
        ys = ys * _sigmoid(_dot(ys.astype(BF16), wglu_ref[...]) + bglu_ref[...])
        mix_ref[:, a0:a1] = _rms(ys, g[:, a0:a1]).astype(BF16)
        mix_ref[:, a1:] = _rms(yconv_ref[...], g[:, a1:]).astype(BF16)

    o_ref[...] = x_ref[...] + _dot(mix_ref[...], w_ref[...])


def _out_proj(o_att, y_ssm, y_conv, g, w_glu, b_glu, w, layer, x, *, tm, tn=512):
    m, d = x.shape
    k = w.shape[1]
    vmem = (2 * (_nbytes((tm, k), F32) + _nbytes((k, tn), BF16) + 2 * _nbytes((tm, tn), F32)
                 + _nbytes((SSM_WIDTH, SSM_WIDTH), BF16))
            + _nbytes((tm, k), BF16) + 2 * _nbytes((tm, k), F32) + 4 * MIB)
    row = lambda c: pl.BlockSpec((tm, c), lambda i, j: (i, 0))
    const = lambda r, c: pl.BlockSpec((r, c), lambda i, j: (0, 0))
    return pl.pallas_call(
        _out_proj_kernel,
        out_shape=jax.ShapeDtypeStruct((m, d), F32),
        grid=(m // tm, d // tn),
        in_specs=[row(ATT_WIDTH), row(SSM_WIDTH), row(CONV_CH), const(1, k),
                  pl.BlockSpec((None, SSM_WIDTH, SSM_WIDTH), lambda i, j: (layer, 0, 0)),
                  const(1, SSM_WIDTH),
                  pl.BlockSpec((None, k, tn), lambda i, j: (layer, 0, j)),
                  pl.BlockSpec((tm, tn), lambda i, j: (i, j))],
        out_specs=pl.BlockSpec((tm, tn), lambda i, j: (i, j)),
        scratch_shapes=[pltpu.VMEM((tm, k), BF16)],
        compiler_params=_params(("parallel", "arbitrary"), vmem),
        name="out_proj",
    )(o_att, y_ssm, y_conv, g.reshape(1, k), w_glu, b_glu.reshape(1, SSM_WIDTH), w, x)


def _ffn_kernel(x_ref, g_ref, wg_ref, wu_ref, wd_ref, o_ref, xn_ref):
    f = pl.program_id(1)

    @pl.when(f == 0)
    def _():
        x = x_ref[...]
        xn_ref[...] = _rms(x, g_ref[...]).astype(BF16)
        o_ref[...] = x

    xn = xn_ref[...]
    gate = _dot(xn, wg_ref[...])
    up = _dot(xn, wu_ref[...])
    act = (gate * _sigmoid(gate) * up).astype(BF16)
    o_ref[...] += _dot(act, wd_ref[...])


def _ffn(x, g, w_gate, w_up, w_down, layer, *, tm, tf=512):
    m, d = x.shape
    ff = w_gate.shape[2]
    vmem = (2 * (2 * _nbytes((tm, d), F32) + 3 * _nbytes((d, tf), BF16))
            + _nbytes((tm, d), BF16) + 2 * _nbytes((tm, d), F32) + 4 * _nbytes((tm, tf), F32) + 4 * MIB)
    return pl.pallas_call(
        _ffn_kernel,
        out_shape=jax.ShapeDtypeStruct((m, d), F32),
        grid=(m // tm, ff // tf),
        in_specs=[pl.BlockSpec((tm, d), lambda i, f: (i, 0)),
                  pl.BlockSpec((1, d), lambda i, f: (0, 0)),
                  pl.BlockSpec((None, d, tf), lambda i, f: (layer, 0, f)),
                  pl.BlockSpec((None, d, tf), lambda i, f: (layer, 0, f)),
                  pl.BlockSpec((None, tf, d), lambda i, f: (layer, f, 0))],
        out_specs=pl.BlockSpec((tm, d), lambda i, f: (i, 0)),
        scratch_shapes=[pltpu.VMEM((tm, d), BF16)],
        compiler_params=_params(("parallel", "arbitrary"), vmem),
        name="ffn",
    )(x, g.reshape(1, d), w_gate, w_up, w_down)


def _final_norm_kernel(x_ref, g_ref, o_ref):
    o_ref[...] = _rms(x_ref[...], g_ref[...])


def _final_norm(x, g, *, tm):
    m, d = x.shape
    vmem = 2 * 2 * _nbytes((tm, d), F32) + 3 * _nbytes((tm, d), F32) + 4 * MIB
    return pl.pallas_call(
        _final_norm_kernel,
        out_shape=jax.ShapeDtypeStruct((m, d), F32),
        grid=(m // tm,),
        in_specs=[pl.BlockSpec((tm, d), lambda i: (i, 0)), pl.BlockSpec((1, d), lambda i: (0, 0))],
        out_specs=pl.BlockSpec((tm, d), lambda i: (i, 0)),
        compiler_params=_params(("parallel",), vmem),
        name="final_norm",
    )(x, g.reshape(1, d))


def _layer(x, lw, layer, *, batch, cache_t, conv_prefix, ssm_h0, tm):
    m, d = x.shape
    n = m // batch
    q, kv, rest, kvt = _in_proj(x, lw['g_mix_norm'], lw['w_in'], layer, tm=tm, seq=n)
    q3, kv3, rest3 = (a.reshape(batch, n, a.shape[-1]) for a in (q, kv, rest))
    n_heads = ATT_WIDTH // HEAD_DIM
    if kvt is None:
        kv_out = kv3.reshape(batch, n, 2, n_heads, HEAD_DIM)
    else:
        kv_out = jnp.transpose(kvt.reshape(batch, 2, n_heads, HEAD_DIM, n), (0, 4, 1, 2, 3))
    ssm_w = (lw['ab_re'], lw['ab_im'], lw['bb'], lw['c_re'], lw['c_im'], lw['ssm_d'])
    if cache_t is None:
        o_att = _attn_prompt(q3, kv3)
        y_ssm, h_re, h_im = _ssm(rest3, *ssm_h0, *ssm_w, chained=True)
        h_re, h_im = h_re[:, SCAN_SEQS - 1], h_im[:, SCAN_SEQS - 1]
    else:
        o_att = _attn_sample(q3, kv3, cache_t, layer)
        y_ssm, h_re, h_im = _ssm(rest.reshape(1, m, rest.shape[-1]), *ssm_h0, *ssm_w, chained=False)
        h_re, h_im = h_re[0], h_im[0]
    y_conv, conv_state = _conv(rest3, lw['conv_w'], conv_prefix)
    x = _out_proj(o_att.reshape(m, ATT_WIDTH), y_ssm.reshape(m, SSM_WIDTH), y_conv.reshape(m, CONV_CH),
                  lw['g_out'], lw['ssm_w_glu'], lw['ssm_b_glu'], lw['w_out'], layer, x, tm=min(tm, 512))
    x = _ffn(x, lw['g_ffn_norm'], lw['w_gate'], lw['w_up'], lw['w_down'], layer, tm=min(tm, 512))
    return x, kv_out, conv_state, h_re, h_im


def kernel(x_prompt, x_sample, cache_kv, state_conv, state_ssm_re, state_ssm_im, g_mix_norm, w_in,
           ssm_a_re, ssm_a_im, ssm_log_dt, ssm_b_re, ssm_b_im, ssm_c_re, ssm_c_im, ssm_d, ssm_w_glu,
           ssm_b_glu, conv_w, g_out, w_out, g_ffn_norm, w_gate, w_up, w_down, g_final):
    batch, seq, d_model = x_prompt.shape
    dec_batch, dec_seq, _ = x_sample.shape
    depth, n_groups, n_state = ssm_a_re.shape
    n_heads = ATT_WIDTH // HEAD_DIM
    assert dec_batch == SCAN_SEQS and seq % SCAN_SEQS == 0

    ab_re, ab_im, bb_re, bb_im = _ssm_prep(ssm_a_re, ssm_a_im, ssm_log_dt, ssm_b_re, ssm_b_im)
    stacked = {'w_in': w_in.astype(BF16), 'ssm_w_glu': ssm_w_glu.astype(BF16), 'w_out': w_out.astype(BF16),
               'w_gate': w_gate.astype(BF16), 'w_up': w_up.astype(BF16), 'w_down': w_down.astype(BF16)}
    cache_t = jnp.transpose(cache_kv, (0, 1, 3, 4, 5, 2))

    xp = x_prompt.reshape(batch * seq, d_model)
    xs = x_sample.reshape(dec_batch * dec_seq, d_model)
    zero_state = jnp.zeros((batch, SCAN_SEQS, n_groups * n_state), F32)
    zero_prefix = jnp.zeros((batch, CONV_K - 1, CONV_CH), F32)

    outs_p, outs_s = [], []
    for i in range(depth):
        bb = jnp.concatenate([_block_diag(bb_re[i]), _block_diag(bb_im[i])], axis=-1).astype(BF16)
        c_t = lambda c: _block_diag(jnp.transpose(c[i], (0, 2, 1))).astype(BF16)
        lw = dict(stacked)
        lw.update({
            'g_mix_norm': g_mix_norm[i],
            'ab_re': ab_re[i].reshape(1, -1), 'ab_im': ab_im[i].reshape(1, -1), 'bb': bb,
            'c_re': c_t(ssm_c_re), 'c_im': c_t(ssm_c_im), 'ssm_d': ssm_d[i].reshape(1, -1),
            'ssm_b_glu': ssm_b_glu[i], 'conv_w': conv_w[i], 'g_out': g_out[i],
            'g_ffn_norm': g_ffn_norm[i],
        })
        xp, kv, c_state, h_re, h_im = _layer(
            xp, lw, i, batch=batch, cache_t=None, conv_prefix=zero_prefix,
            ssm_h0=(zero_state, zero_state), tm=1024)
        n_keep = min(max(ATT_DILATIONS) * ATT_SPAN, seq)
        outs_p.append((kv[:, seq - n_keep:], c_state,
                       h_re.reshape(batch, n_groups, n_state), h_im.reshape(batch, n_groups, n_state)))

        h0 = (state_ssm_re[i].reshape(1, dec_batch, -1), state_ssm_im[i].reshape(1, dec_batch, -1))
        xs, kv, c_state, h_re, h_im = _layer(
            xs, lw, i, batch=dec_batch, cache_t=cache_t, conv_prefix=state_conv[i], ssm_h0=h0,
            tm=dec_batch * dec_seq)
        outs_s.append((kv, c_state,
                       h_re.reshape(dec_batch, n_groups, n_state),
                       h_im.reshape(dec_batch, n_groups, n_state)))

    y_prompt = _final_norm(xp, g_final, tm=1024).reshape(batch, seq, d_model)
    y_sample = _final_norm(xs, g_final, tm=dec_batch * dec_seq).reshape(dec_batch, dec_seq, d_model)
    stack = lambda outs, k: jnp.stack([o[k] for o in outs])
    return (y_prompt, y_sample,
            stack(outs_p, 0), stack(outs_p, 1), stack(outs_p, 2), stack(outs_p, 3),
            stack(outs_s, 0), stack(outs_s, 1), stack(outs_s, 2), stack(outs_s, 3))
```

```python
import functools
import math

import jax
import jax.numpy as jnp
from jax import lax
from jax.experimental import pallas as pl
from jax.experimental.pallas import tpu as pltpu

F32 = jnp.float32
BF16 = jnp.bfloat16

HEAD_DIM = 64
ATT_WIDTH = 1024
SSM_WIDTH = 512
CONV_CH = 512
CONV_K = 3
SSM_GROUP = 16
SSM_STATE = 64
ATT_DILATIONS = (1, 4, 16)
ATT_SPAN = 128
RMS_EPS = 1e-6
NEG = -1e30

LANES = 128
SUBLANES = 8
SCAN_SEQS = SUBLANES
SSM_BLOCK_GROUPS = LANES // SSM_GROUP
SSM_BLOCK_STATE = SSM_BLOCK_GROUPS * SSM_STATE

MIB = 1024 * 1024


def _params(semantics, vmem_bytes):
    return pltpu.CompilerParams(dimension_semantics=semantics, vmem_limit_bytes=int(vmem_bytes))


def _nbytes(shape, dtype):
    return math.prod(shape) * jnp.dtype(dtype).itemsize


def _rms(x, g):
    ms = jnp.mean(x * x, axis=-1, keepdims=True)
    return x * lax.rsqrt(ms + RMS_EPS) * g


def _sigmoid(x):
    return 1.0 / (1.0 + jnp.exp(-x))


def _gelu_tanh(x):
    c = math.sqrt(2.0 / math.pi)
    return 0.5 * x * (1.0 + jnp.tanh(c * (x + 0.044715 * (x * x * x))))


def _dot(a, b):
    return jnp.dot(a, b, preferred_element_type=F32)


def _dot_nt(a, b):
    return lax.dot_general(a, b, (((1,), (1,)), ((), ())), preferred_element_type=F32)


def _in_proj_kernel(x_ref, g_ref, w_ref, q_ref, kv_ref, rest_ref, *more, nq, nkv):
    kvt_ref = more[0] if len(more) == 2 else None
    xn_ref = more[-1]
    j = pl.program_id(1)

    @pl.when(j == 0)
    def _():
        xn_ref[...] = _rms(x_ref[...], g_ref[...]).astype(BF16)

    @pl.when(j < nq)
    def _():
        q_ref[...] = _dot(xn_ref[...], w_ref[...])

    @pl.when((j >= nq) & (j < nq + nkv))
    def _():
        z = _dot(xn_ref[...], w_ref[...])
        kv_ref[...] = z
        if kvt_ref is not None:
            kvt_ref[...] = z.T

    @pl.when(j >= nq + nkv)
    def _():
        rest_ref[...] = _dot(xn_ref[...], w_ref[...])


def _in_proj(x, g, w, layer, *, tm, seq, tn=512):
    m, k = x.shape
    n = w.shape[2]
    nq, nkv = ATT_WIDTH // tn, 2 * ATT_WIDTH // tn
    nrest = n // tn - nq - nkv
    with_kvt = seq % tm == 0 and tm % LANES == 0
    tiles_per_seq = seq // tm if with_kvt else 1
    vmem = (2 * (_nbytes((tm, k), F32) + _nbytes((k, tn), BF16) + 4 * _nbytes((tm, tn), F32))
            + _nbytes((tm, k), BF16) + 2 * _nbytes((tm, k), F32) + 2 * _nbytes((tm, tn), F32) + 4 * MIB)
    kv_col = lambda j: jnp.clip(j - nq, 0, nkv - 1)
    out_shape = [jax.ShapeDtypeStruct((m, nq * tn), F32),
                 jax.ShapeDtypeStruct((m, nkv * tn), F32),
                 jax.ShapeDtypeStruct((m, nrest * tn), F32)]
    out_specs = [pl.BlockSpec((tm, tn), lambda i, j: (i, jnp.minimum(j, nq - 1))),
                 pl.BlockSpec((tm, tn), lambda i, j: (i, kv_col(j))),
                 pl.BlockSpec((tm, tn), lambda i, j: (i, jnp.clip(j - nq - nkv, 0, nrest - 1)))]
    if with_kvt:
        out_shape.append(jax.ShapeDtypeStruct((m // seq, nkv * tn, seq), F32))
        out_specs.append(pl.BlockSpec(
            (None, tn, tm), lambda i, j: (i // tiles_per_seq, kv_col(j), i % tiles_per_seq)))
    outs = pl.pallas_call(
        functools.partial(_in_proj_kernel, nq=nq, nkv=nkv),
        out_shape=tuple(out_shape),
        grid=(m // tm, n // tn),
        in_specs=[pl.BlockSpec((tm, k), lambda i, j: (i, 0)),
                  pl.BlockSpec((1, k), lambda i, j: (0, 0)),
                  pl.BlockSpec((None, k, tn), lambda i, j: (layer, 0, j))],
        out_specs=tuple(out_specs),
        scratch_shapes=[pltpu.VMEM((tm, k), BF16)],
        compiler_params=_params(("parallel", "arbitrary"), vmem),
        name="in_proj",
    )(x, g.reshape(1, k), w)
    return outs if with_kvt else (*outs, None)


def _attn_prompt_kernel(q_ref, k_ref, v_ref, o_ref, ob_ref, lb_ref, bias_ref):
    s_len = q_ref.shape[0]
    blk = ATT_SPAN
    lane = lax.broadcasted_iota(jnp.int32, (blk, LANES), 1)
    head0 = lane < HEAD_DIM
    scale = HEAD_DIM ** -0.5

    qi = lax.broadcasted_iota(jnp.int32, (2 * blk, 3 * blk), 0) & (blk - 1)
    kj = lax.broadcasted_iota(jnp.int32, (2 * blk, 3 * blk), 1)
    dist = jnp.where(kj < blk, qi - kj, qi + 2 * blk - kj)
    bias_ref[...] = jnp.where((dist >= 0) & (dist <= ATT_SPAN), 0.0, NEG)

    def block(p, d, q0, k0=None):
        def rows(start, n):
            return pl.ds(start, n, stride=d) if d > 1 else pl.ds(start, n)

        qt = q_ref[rows(q0, blk), :] * scale
        qq = jnp.concatenate([jnp.where(head0, qt, 0.0), jnp.where(head0, 0.0, qt)], axis=0).astype(BF16)
        if k0 is None:
            key_rows, bias = rows(q0, blk), bias_ref[:, :blk]
        else:
            key_rows, bias = rows(k0, 2 * blk), bias_ref[:, blk:]
        k = k_ref[key_rows, :].astype(BF16)
        v = v_ref[key_rows, :].astype(BF16)
        s = _dot_nt(qq, k) + bias
        m = jnp.max(s, axis=-1, keepdims=True)
        e = jnp.exp(s - m)
        l = jnp.sum(e, axis=-1, keepdims=True)
        o = _dot(e.astype(BF16), v) * (1.0 / l)
        lse = jnp.broadcast_to(m + jnp.log(l), (2 * blk, LANES))
        ob_ref[p, rows(q0, blk), :] = jnp.where(head0, o[:blk], o[blk:])
        lb_ref[p, rows(q0, blk), :] = jnp.where(head0, lse[:blk], lse[blk:])

    for p, d in enumerate(ATT_DILATIONS):
        nblk = s_len // (d * blk)
        for r in range(d):
            for b in range(nblk):
                block(p, d, r + d * blk * b, None if b == 0 else r + d * blk * (b - 1))

    def merge(c, carry):
        rows = pl.ds(pl.multiple_of(c * blk, blk), blk)
        l0, l1, l2 = lb_ref[0, rows, :], lb_ref[1, rows, :], lb_ref[2, rows, :]
        m = jnp.maximum(jnp.maximum(l0, l1), l2)
        e0, e1, e2 = jnp.exp(l0 - m), jnp.exp(l1 - m), jnp.exp(l2 - m)
        num = e0 * ob_ref[0, rows, :] + e1 * ob_ref[1, rows, :] + e2 * ob_ref[2, rows, :]
        o_ref[rows, :] = num / (e0 + e1 + e2)
        return carry

    lax.fori_loop(0, s_len // blk, merge, 0)


def _attn_prompt(q, kv):
    b, s, _ = q.shape
    assert s % (max(ATT_DILATIONS) * ATT_SPAN) == 0
    npair = ATT_WIDTH // LANES
    blk_bytes = _nbytes((s, LANES), F32)
    vmem = 2 * 4 * blk_bytes + 6 * blk_bytes + 4 * MIB
    return pl.pallas_call(
        _attn_prompt_kernel,
        out_shape=jax.ShapeDtypeStruct((b, s, ATT_WIDTH), F32),
        grid=(b, npair),
        in_specs=[pl.BlockSpec((None, s, LANES), lambda i, j: (i, 0, j)),
                  pl.BlockSpec((None, s, LANES), lambda i, j: (i, 0, j)),
                  pl.BlockSpec((None, s, LANES), lambda i, j: (i, 0, npair + j))],
        out_specs=pl.BlockSpec((None, s, LANES), lambda i, j: (i, 0, j)),
        scratch_shapes=[pltpu.VMEM((len(ATT_DILATIONS), s, LANES), F32),
                        pltpu.VMEM((len(ATT_DILATIONS), s, LANES), F32),
                        pltpu.VMEM((2 * ATT_SPAN, 3 * ATT_SPAN), F32)],
        compiler_params=_params(("parallel", "parallel"), vmem),
        name="attn_prompt",
    )(q, kv, kv)


def _attn_sample_kernel(q_ref, kn_ref, vn_ref, kct_ref, vct_ref, o_ref):
    t_len = q_ref.shape[0]
    w_len = kct_ref.shape[-1]
    pad = LANES
    q = q_ref[...] * (HEAD_DIM ** -0.5)
    zpad = jnp.zeros((pad - t_len, LANES), F32)
    kn = jnp.concatenate([kn_ref[...], zpad], axis=0).astype(BF16)
    vn = jnp.concatenate([vn_ref[...], zpad], axis=0).astype(BF16)
    kct = kct_ref[...].reshape(LANES, w_len).astype(BF16)
    vct = vct_ref[...].reshape(LANES, w_len).astype(BF16)

    def multiplicity(dist):
        c = jnp.zeros(dist.shape, F32)
        for d in ATT_DILATIONS:
            ok = (dist >= 0) & ((dist & (d - 1)) == 0) & (dist <= ATT_SPAN * d)
            c = c + jnp.where(ok, 1.0, 0.0)
        return c

    tc = lax.broadcasted_iota(jnp.int32, (t_len, w_len), 0)
    jc = lax.broadcasted_iota(jnp.int32, (t_len, w_len), 1)
    cnt_c = multiplicity(w_len + tc - jc)
    tn = lax.broadcasted_iota(jnp.int32, (t_len, pad), 0)
    jn = lax.broadcasted_iota(jnp.int32, (t_len, pad), 1)
    cnt_n = multiplicity(jnp.where(jn < t_len, tn - jn, -1))

    lane = lax.broadcasted_iota(jnp.int32, (t_len, LANES), 1)
    head0 = lane < HEAD_DIM
    outs = []
    for h in range(2):
        hm = head0 if h == 0 else jnp.logical_not(head0)
        qh = jnp.where(hm, q, 0.0).astype(BF16)
        sc = jnp.where(cnt_c > 0, _dot(qh, kct), NEG)
        sn = jnp.where(cnt_n > 0, _dot_nt(qh, kn), NEG)
        m = jnp.maximum(jnp.max(sc, axis=-1, keepdims=True), jnp.max(sn, axis=-1, keepdims=True))
        ec = cnt_c * jnp.exp(sc - m)
        en = cnt_n * jnp.exp(sn - m)
        l = jnp.sum(ec, axis=-1, keepdims=True) + jnp.sum(en, axis=-1, keepdims=True)
        pv = _dot_nt(ec.astype(BF16), vct) + _dot(en.astype(BF16), vn)
        outs.append(pv / l)
    o_ref[...] = jnp.where(head0, outs[0], outs[1])


def _attn_sample(q, kv, cache_t, layer):
    b, t, _ = q.shape
    w = cache_t.shape[-1]
    assert all(d & (d - 1) == 0 for d in ATT_DILATIONS) and t <= LANES
    npair = ATT_WIDTH // LANES
    vmem = 2 * 2 * _nbytes((LANES, w), F32) + 10 * _nbytes((LANES, w), F32) + 4 * MIB
    new_spec = lambda off: pl.BlockSpec((None, t, LANES), lambda i, j: (i, 0, off + j))
    buf_spec = lambda c: pl.BlockSpec((None, None, None, 2, HEAD_DIM, w),
                                      lambda i, j: (layer, i, c, j, 0, 0))
    return pl.pallas_call(
        _attn_sample_kernel,
        out_shape=jax.ShapeDtypeStruct((b, t, ATT_WIDTH), F32),
        grid=(b, npair),
        in_specs=[new_spec(0), new_spec(0), new_spec(npair), buf_spec(0), buf_spec(1)],
        out_specs=new_spec(0),
        compiler_params=_params(("parallel", "parallel"), vmem),
        name="attn_sample",
    )(q, kv, kv, cache_t, cache_t)


def _ssm_prep_kernel(are_ref, aim_ref, ldt_ref, bre_ref, bim_ref, abre_ref, abim_ref, bbre_ref, bbim_ref):
    a_re, a_im = are_ref[...], aim_ref[...]
    dt = jnp.exp(ldt_ref[...])
    mag = jnp.exp(a_re * dt)
    ang = a_im * dt
    ab_re, ab_im = mag * jnp.cos(ang), mag * jnp.sin(ang)
    den = a_re * a_re + a_im * a_im
    nr, ni = ab_re - 1.0, ab_im
    cf_re = (nr * a_re + ni * a_im) / den
    cf_im = (ni * a_re - nr * a_im) / den
    abre_ref[...] = ab_re
    abim_ref[...] = ab_im
    for c in range(SSM_GROUP):
        b_re, b_im = bre_ref[c], bim_ref[c]
        bbre_ref[c] = cf_re * b_re - cf_im * b_im
        bbim_ref[c] = cf_re * b_im + cf_im * b_re


def _ssm_prep(a_re, a_im, log_dt, b_re, b_im):
    depth, g, n = a_re.shape
    rows = depth * g
    flat = lambda a: a.reshape(rows, n)
    chan_major = lambda b: jnp.transpose(b.reshape(rows, n, SSM_GROUP), (2, 0, 1))
    ab_re, ab_im, bb_re, bb_im = pl.pallas_call(
        _ssm_prep_kernel,
        out_shape=(jax.ShapeDtypeStruct((rows, n), F32), jax.ShapeDtypeStruct((rows, n), F32),
                   jax.ShapeDtypeStruct((SSM_GROUP, rows, n), F32),
                   jax.ShapeDtypeStruct((SSM_GROUP, rows, n), F32)),
        name="ssm_prep",
    )(flat(a_re), flat(a_im), log_dt.reshape(rows, 1), chan_major(b_re), chan_major(b_im))
    unflat = lambda b: jnp.transpose(b, (1, 0, 2)).reshape(depth, g, SSM_GROUP, n)
    return ab_re.reshape(depth, g, n), ab_im.reshape(depth, g, n), unflat(bb_re), unflat(bb_im)


def _block_diag(m):
    g, r, c = m.shape
    nb = g // SSM_BLOCK_GROUPS
    m = m.reshape(nb, SSM_BLOCK_GROUPS, r, c)
    eye = jnp.eye(SSM_BLOCK_GROUPS, dtype=m.dtype)
    out = m[:, :, :, None, :] * eye[None, :, None, :, None]
    return out.reshape(nb, SSM_BLOCK_GROUPS * r, SSM_BLOCK_GROUPS * c)


def _ssm_kernel(u_ref, h0re_ref, h0im_ref, are_ref, aim_ref, bb_ref, cre_ref, cim_ref, d_ref,
                y_ref, hre_ref, him_ref, x_scr, up_scr, *, steps, chained):
    ns = SSM_BLOCK_STATE
    nrows = steps * SCAN_SEQS
    chunk = min(nrows, 256)

    nchunks = nrows // chunk
    chunk_steps = chunk // SCAN_SEQS
    unroll = min(steps, 8)

    def permute_in(t, c):
        up_scr[pl.ds(pl.multiple_of(t * SCAN_SEQS, SCAN_SEQS), SCAN_SEQS), :] = (
            u_ref[pl.ds(t, SCAN_SEQS, stride=steps), :])
        return c
    lax.fori_loop(0, steps, permute_in, 0, unroll=unroll)

    a_re = jnp.broadcast_to(are_ref[...], (SCAN_SEQS, ns))
    a_im = jnp.broadcast_to(aim_ref[...], (SCAN_SEQS, ns))

    def input_map(c):
        rows = pl.ds(c * chunk, chunk)
        x_scr[rows, :] = _dot(up_scr[rows, :].astype(BF16), bb_ref[...])

    def scan_chunk(c, h, store):
        h_re, h_im = h
        for t in range(c * chunk_steps, (c + 1) * chunk_steps):
            rows = pl.ds(t * SCAN_SEQS, SCAN_SEQS)
            h_re, h_im = (a_re * h_re - a_im * h_im + x_scr[rows, :ns],
                          a_re * h_im + a_im * h_re + x_scr[rows, ns:])
            if store:
                x_scr[rows, :ns] = h_re
                x_scr[rows, ns:] = h_im
        return h_re, h_im

    def output_map(c):
        rows = pl.ds(c * chunk, chunk)
        y = (_dot(x_scr[rows, :ns].astype(BF16), cre_ref[...])
             - _dot(x_scr[rows, ns:].astype(BF16), cim_ref[...])
             + d_ref[...] * up_scr[rows, :])
        up_scr[rows, :] = _gelu_tanh(y)

    init = (h0re_ref[...], h0im_ref[...])
    input_map(0)
    if chained:
        assert steps & (steps - 1) == 0
        e_re, e_im = init
        for c in range(nchunks):
            if c + 1 < nchunks:
                input_map(c + 1)
            e_re, e_im = scan_chunk(c, (e_re, e_im), store=False)
        p_re, p_im = a_re, a_im
        for _ in range(steps.bit_length() - 1):
            p_re, p_im = p_re * p_re - p_im * p_im, 2.0 * p_re * p_im
        sub = lax.broadcasted_iota(jnp.int32, (SCAN_SEQS, ns), 0)
        shift = 1
        while shift < SCAN_SEQS:
            r_re = pltpu.roll(e_re, shift, axis=0)
            r_im = pltpu.roll(e_im, shift, axis=0)
            keep = sub >= shift
            e_re, e_im = (e_re + jnp.where(keep, p_re * r_re - p_im * r_im, 0.0),
                          e_im + jnp.where(keep, p_re * r_im + p_im * r_re, 0.0))
            p_re, p_im = p_re * p_re - p_im * p_im, 2.0 * p_re * p_im
            shift *= 2
        first = sub == 0
        init = (init[0] + jnp.where(first, 0.0, pltpu.roll(e_re, 1, axis=0)),
                init[1] + jnp.where(first, 0.0, pltpu.roll(e_im, 1, axis=0)))
    else:
        for c in range(1, nchunks):
            input_map(c)
    h = init
    for c in range(nchunks):
        h = scan_chunk(c, h, store=True)
        if c > 0:
            output_map(c - 1)
    output_map(nchunks - 1)
    hre_ref[...] = h[0]
    him_ref[...] = h[1]

    def permute_out(t, c):
        y_ref[pl.ds(t, SCAN_SEQS, stride=steps), :] = (
            up_scr[pl.ds(pl.multiple_of(t * SCAN_SEQS, SCAN_SEQS), SCAN_SEQS), :])
        return c
    lax.fori_loop(0, steps, permute_out, 0, unroll=unroll)


def _ssm(u, h0_re, h0_im, ab_re, ab_im, bb, c_re, c_im, d_skip, *, chained):
    nb, rows, _ = u.shape
    steps = rows // SCAN_SEQS
    nblk = SSM_WIDTH // LANES
    ns = SSM_BLOCK_STATE
    vmem = (2 * 2 * _nbytes((rows, LANES), F32) + _nbytes((rows, 2 * ns), F32)
            + _nbytes((rows, LANES), F32) + 4 * _nbytes((min(rows, 256), 2 * ns), F32) + 8 * MIB)
    state_spec = pl.BlockSpec((None, SCAN_SEQS, ns), lambda i, j: (i, 0, j))
    return pl.pallas_call(
        functools.partial(_ssm_kernel, steps=steps, chained=chained),
        out_shape=(jax.ShapeDtypeStruct((nb, rows, SSM_WIDTH), F32),
                   jax.ShapeDtypeStruct(h0_re.shape, F32),
                   jax.ShapeDtypeStruct(h0_im.shape, F32)),
        grid=(nb, nblk),
        in_specs=[pl.BlockSpec((None, rows, LANES), lambda i, j: (i, 0, j)),
                  state_spec, state_spec,
                  pl.BlockSpec((1, ns), lambda i, j: (0, j)),
                  pl.BlockSpec((1, ns), lambda i, j: (0, j)),
                  pl.BlockSpec((None, LANES, 2 * ns), lambda i, j: (j, 0, 0)),
                  pl.BlockSpec((None, ns, LANES), lambda i, j: (j, 0, 0)),
                  pl.BlockSpec((None, ns, LANES), lambda i, j: (j, 0, 0)),
                  pl.BlockSpec((1, LANES), lambda i, j: (0, j))],
        out_specs=(pl.BlockSpec((None, rows, LANES), lambda i, j: (i, 0, j)), state_spec, state_spec),
        scratch_shapes=[pltpu.VMEM((rows, 2 * ns), F32), pltpu.VMEM((rows, LANES), F32)],
        compiler_params=_params(("parallel", "parallel"), vmem),
        name="ssm_chained" if chained else "ssm_step",
    )(u, h0_re, h0_im, ab_re, ab_im, bb, c_re, c_im, d_skip)


def _conv_kernel(h_ref, gb_ref, gc_ref, w_ref, pre_ref, y_ref, st_ref):
    n = h_ref.shape[0]
    u = gc_ref[...] * h_ref[...]
    row = lax.broadcasted_iota(jnp.int32, u.shape, 0)
    pre = pre_ref[...]
    p0, p1 = pre[0:1, :], pre[1:2, :]
    u1 = jnp.where(row == 0, p1, pltpu.roll(u, 1, axis=0))
    u2 = jnp.where(row == 0, p0, jnp.where(row == 1, p1, pltpu.roll(u, 2, axis=0)))
    w = w_ref[...]
    y = w[0:1, :] * u2 + w[1:2, :] * u1 + w[2:3, :] * u
    y_ref[...] = gb_ref[...] * y
    st_ref[...] = u[n - (CONV_K - 1):, :]


def _conv(rest, w, prefix):
    b, n, _ = rest.shape
    assert n >= CONV_K - 1
    nblk = CONV_CH // LANES
    off = SSM_WIDTH // LANES
    col = lambda part: pl.BlockSpec((None, n, LANES), lambda i, j: (i, 0, off + part * nblk + j))
    vmem = 2 * 4 * _nbytes((n, LANES), F32) + 8 * _nbytes((n, LANES), F32) + 4 * MIB
    return pl.pallas_call(
        _conv_kernel,
        out_shape=(jax.ShapeDtypeStruct((b, n, CONV_CH), F32),
                   jax.ShapeDtypeStruct((b, CONV_K - 1, CONV_CH), F32)),
        grid=(b, nblk),
        in_specs=[col(0), col(1), col(2),
                  pl.BlockSpec((CONV_K, LANES), lambda i, j: (0, j)),
                  pl.BlockSpec((None, CONV_K - 1, LANES), lambda i, j: (i, 0, j))],
        out_specs=(pl.BlockSpec((None, n, LANES), lambda i, j: (i, 0, j)),
                   pl.BlockSpec((None, CONV_K - 1, LANES), lambda i, j: (i, 0, j))),
        compiler_params=_params(("parallel", "parallel"), vmem),
        name="conv",
    )(rest, rest, rest, w, prefix)


def _out_proj_kernel(oatt_ref, yssm_ref, yconv_ref, g_ref, wglu_ref, bglu_ref, w_ref, x_ref,
                     o_ref, mix_ref):
    j = pl.program_id(1)
    a0, a1 = ATT_WIDTH, ATT_WIDTH + SSM_WIDTH

    @pl.when(j == 0)
    def _():
        g = g_ref[...]
        mix_ref[:, :a0] = _rms(oatt_ref[...], g[:, :a0]).astype(BF16)
        ys = yssm_ref[...]
        ys = ys * _sigmoid(_dot(ys.astype(BF16), wglu_ref[...]) + bglu_ref[...])
        mix_ref[:, a0:a1] = _rms(ys, g[:, a0:a1]).astype(BF16)
        mix_ref[:, a1:] = _rms(yconv_ref[...], g[:, a1:]).astype(BF16)

    o_ref[...] = x_ref[...] + _dot(mix_ref[...], w_ref[...])


def _out_proj(o_att, y_ssm, y_conv, g, w_glu, b_glu, w, layer, x, *, tm, tn=512):
    m, d = x.shape
    k = w.shape[1]
    vmem = (2 * (_nbytes((tm, k), F32) + _nbytes((k, tn), BF16) + 2 * _nbytes((tm, tn), F32)
                 + _nbytes((SSM_WIDTH, SSM_WIDTH), BF16))
            + _nbytes((tm, k), BF16) + 2 * _nbytes((tm, k), F32) + 4 * MIB)
    row = lambda c: pl.BlockSpec((tm, c), lambda i, j: (i, 0))
    const = lambda r, c: pl.BlockSpec((r, c), lambda i, j: (0, 0))
    return pl.pallas_call(
        _out_proj_kernel,
        out_shape=jax.ShapeDtypeStruct((m, d), F32),
        grid=(m // tm, d // tn),
        in_specs=[row(ATT_WIDTH), row(SSM_WIDTH), row(CONV_CH), const(1, k),
                  pl.BlockSpec((None, SSM_WIDTH, SSM_WIDTH), lambda i, j: (layer, 0, 0)),
                  const(1, SSM_WIDTH),
                  pl.BlockSpec((None, k, tn), lambda i, j: (layer, 0, j)),
                  pl.BlockSpec((tm, tn), lambda i, j: (i, j))],
        out_specs=pl.BlockSpec((tm, tn), lambda i, j: (i, j)),
        scratch_shapes=[pltpu.VMEM((tm, k), BF16)],
        compiler_params=_params(("parallel", "arbitrary"), vmem),
        name="out_proj",
    )(o_att, y_ssm, y_conv, g.reshape(1, k), w_glu, b_glu.reshape(1, SSM_WIDTH), w, x)


def _ffn_kernel(x_ref, xs_ref, g_ref, wg_ref, wu_ref, wd_ref, o_ref, os_ref, xn_ref, *, tm):
    i, f = pl.program_id(0), pl.program_id(1)

    @pl.when(f == 0)
    def _():
        x = x_ref[...]
        xn_ref[:tm] = _rms(x, g_ref[...]).astype(BF16)
        o_ref[...] = x

    @pl.when((f == 0) & (i == 0))
    def _():
        xs = xs_ref[...]
        xn_ref[tm:] = _rms(xs, g_ref[...]).astype(BF16)
        os_ref[...] = xs

    def swiglu(xn):
        gate = _dot(xn, wg_ref[...].astype(BF16))
        up = _dot(xn, wu_ref[...].astype(BF16))
        return _dot((gate * _sigmoid(gate) * up).astype(BF16), wd_ref[...].astype(BF16))

    @pl.when(i == 0)
    def _():
        y = swiglu(xn_ref[...])
        o_ref[...] += y[:tm]
        os_ref[...] += y[tm:]

    @pl.when(i > 0)
    def _():
        o_ref[...] += swiglu(xn_ref[:tm])


def _ffn(x, xs, g, w_gate, w_up, w_down, layer, *, tm, tf=256):
    m, d = x.shape
    ms = xs.shape[0]
    ff = w_gate.shape[2]
    vmem = (2 * (2 * _nbytes((tm, d), F32) + 2 * _nbytes((ms, d), F32) + 3 * _nbytes((d, tf), F32))
            + _nbytes((tm + ms, d), BF16) + 3 * _nbytes((d, tf), BF16) + 4 * _nbytes((tm + ms, tf), F32)
            + 2 * MIB)
    return pl.pallas_call(
        functools.partial(_ffn_kernel, tm=tm),
        out_shape=(jax.ShapeDtypeStruct((m, d), F32), jax.ShapeDtypeStruct((ms, d), F32)),
        grid=(m // tm, ff // tf),
        in_specs=[pl.BlockSpec((tm, d), lambda i, f: (i, 0)),
                  pl.BlockSpec((ms, d), lambda i, f: (0, 0)),
                  pl.BlockSpec((1, d), lambda i, f: (0, 0)),
                  pl.BlockSpec((None, d, tf), lambda i, f: (layer, 0, f)),
                  pl.BlockSpec((None, d, tf), lambda i, f: (layer, 0, f)),
                  pl.BlockSpec((None, tf, d), lambda i, f: (layer, f, 0))],
        out_specs=(pl.BlockSpec((tm, d), lambda i, f: (i, 0)),
                   pl.BlockSpec((ms, d), lambda i, f: (0, 0))),
        scratch_shapes=[pltpu.VMEM((tm + ms, d), BF16)],
        compiler_params=_params(("arbitrary", "arbitrary"), vmem),
        name="ffn",
    )(x, xs, g.reshape(1, d), w_gate, w_up, w_down)


def _final_norm_kernel(x_ref, g_ref, o_ref):
    o_ref[...] = _rms(x_ref[...], g_ref[...])


def _final_norm(x, g, *, tm):
    m, d = x.shape
    vmem = 2 * 2 * _nbytes((tm, d), F32) + 3 * _nbytes((tm, d), F32) + 4 * MIB
    return pl.pallas_call(
        _final_norm_kernel,
        out_shape=jax.ShapeDtypeStruct((m, d), F32),
        grid=(m // tm,),
        in_specs=[pl.BlockSpec((tm, d), lambda i: (i, 0)), pl.BlockSpec((1, d), lambda i: (0, 0))],
        out_specs=pl.BlockSpec((tm, d), lambda i: (i, 0)),
        compiler_params=_params(("parallel",), vmem),
        name="final_norm",
    )(x, g.reshape(1, d))


def _mixers(x, lw, layer, *, batch, cache_t, conv_prefix, ssm_h0, tm):
    m, d = x.shape
    n = m // batch
    q, kv, rest, kvt = _in_proj(x, lw['g_mix_norm'], lw['w_in'], layer, tm=tm, seq=n)
    q3, kv3, rest3 = (a.reshape(batch, n, a.shape[-1]) for a in (q, kv, rest))
    n_heads = ATT_WIDTH // HEAD_DIM
    if kvt is None:
        kv_out = kv3.reshape(batch, n, 2, n_heads, HEAD_DIM)
    else:
        kv_out = jnp.transpose(kvt.reshape(batch, 2, n_heads, HEAD_DIM, n), (0, 4, 1, 2, 3))
    ssm_w = (lw['ab_re'], lw['ab_im'], lw['bb'], lw['c_re'], lw['c_im'], lw['ssm_d'])
    if cache_t is None:
        o_att = _attn_prompt(q3, kv3)
        y_ssm, h_re, h_im = _ssm(rest3, *ssm_h0, *ssm_w, chained=True)
        h_re, h_im = h_re[:, SCAN_SEQS - 1], h_im[:, SCAN_SEQS - 1]
    else:
        o_att = _attn_sample(q3, kv3, cache_t, layer)
        y_ssm, h_re, h_im = _ssm(rest.reshape(1, m, rest.shape[-1]), *ssm_h0, *ssm_w, chained=False)
        h_re, h_im = h_re[0], h_im[0]
    y_conv, conv_state = _conv(rest3, lw['conv_w'], conv_prefix)
    x = _out_proj(o_att.reshape(m, ATT_WIDTH), y_ssm.reshape(m, SSM_WIDTH), y_conv.reshape(m, CONV_CH),
                  lw['g_out'], lw['ssm_w_glu'], lw['ssm_b_glu'], lw['w_out'], layer, x, tm=min(tm, 512))
    return x, kv_out, conv_state, h_re, h_im


def kernel(x_prompt, x_sample, cache_kv, state_conv, state_ssm_re, state_ssm_im, g_mix_norm, w_in,
           ssm_a_re, ssm_a_im, ssm_log_dt, ssm_b_re, ssm_b_im, ssm_c_re, ssm_c_im, ssm_d, ssm_w_glu,
           ssm_b_glu, conv_w, g_out, w_out, g_ffn_norm, w_gate, w_up, w_down, g_final):
    batch, seq, d_model = x_prompt.shape
    dec_batch, dec_seq, _ = x_sample.shape
    depth, n_groups, n_state = ssm_a_re.shape
    n_heads = ATT_WIDTH // HEAD_DIM
    assert dec_batch == SCAN_SEQS and seq % SCAN_SEQS == 0

    ab_re, ab_im, bb_re, bb_im = _ssm_prep(ssm_a_re, ssm_a_im, ssm_log_dt, ssm_b_re, ssm_b_im)
    stacked = {'w_in': w_in.astype(BF16), 'ssm_w_glu': ssm_w_glu.astype(BF16), 'w_out': w_out.astype(BF16)}
    cache_t = jnp.transpose(cache_kv, (0, 1, 3, 4, 5, 2))

    xp = x_prompt.reshape(batch * seq, d_model)
    xs = x_sample.reshape(dec_batch * dec_seq, d_model)
    zero_state = jnp.zeros((batch, SCAN_SEQS, n_groups * n_state), F32)
    zero_prefix = jnp.zeros((batch, CONV_K - 1, CONV_CH), F32)

    outs_p, outs_s = [], []
    for i in range(depth):
        bb = jnp.concatenate([_block_diag(bb_re[i]), _block_diag(bb_im[i])], axis=-1).astype(BF16)
        c_t = lambda c: _block_diag(jnp.transpose(c[i], (0, 2, 1))).astype(BF16)
        lw = dict(stacked)
        lw.update({
            'g_mix_norm': g_mix_norm[i],
            'ab_re': ab_re[i].reshape(1, -1), 'ab_im': ab_im[i].reshape(1, -1), 'bb': bb,
            'c_re': c_t(ssm_c_re), 'c_im': c_t(ssm_c_im), 'ssm_d': ssm_d[i].reshape(1, -1),
            'ssm_b_glu': ssm_b_glu[i], 'conv_w': conv_w[i], 'g_out': g_out[i],
            'g_ffn_norm': g_ffn_norm[i],
        })
        xp, kv, c_state, h_re, h_im = _mixers(
            xp, lw, i, batch=batch, cache_t=None, conv_prefix=zero_prefix,
            ssm_h0=(zero_state, zero_state), tm=1024)
        n_keep = min(max(ATT_DILATIONS) * ATT_SPAN, seq)
        outs_p.append((kv[:, seq - n_keep:], c_state,
                       h_re.reshape(batch, n_groups, n_state), h_im.reshape(batch, n_groups, n_state)))

        h0 = (state_ssm_re[i].reshape(1, dec_batch, -1), state_ssm_im[i].reshape(1, dec_batch, -1))
        xs, kv, c_state, h_re, h_im = _mixers(
            xs, lw, i, batch=dec_batch, cache_t=cache_t, conv_prefix=state_conv[i], ssm_h0=h0,
            tm=dec_batch * dec_seq)
        outs_s.append((kv, c_state,
                       h_re.reshape(dec_batch, n_groups, n_state),
                       h_im.reshape(dec_batch, n_groups, n_state)))

        xp, xs = _ffn(xp, xs, g_ffn_norm[i], w_gate, w_up, w_down, i, tm=1024)

    y_prompt = _final_norm(xp, g_final, tm=1024).reshape(batch, seq, d_model)
    y_sample = _final_norm(xs, g_final, tm=dec_batch * dec_seq).reshape(dec_batch, dec_seq, d_model)
    stack = lambda outs, k: jnp.stack([o[k] for o in outs])
    return (y_prompt, y_sample,
            stack(outs_p, 0), stack(outs_p, 1), stack(outs_p, 2), stack(outs_p, 3),
            stack(outs_s, 0), stack(outs_s, 1), stack(outs_s, 2), stack(outs_s, 3))
```

```python
import functools
import math

import jax
import jax.numpy as jnp
from jax import lax
from jax.experimental import pallas as pl
from jax.experimental.pallas import tpu as pltpu

F32 = jnp.float32
BF16 = jnp.bfloat16

HEAD_DIM = 64
ATT_WIDTH = 1024
SSM_WIDTH = 512
CONV_CH = 512
CONV_K = 3
SSM_GROUP = 16
SSM_STATE = 64
ATT_DILATIONS = (1, 4, 16)
ATT_SPAN = 128
RMS_EPS = 1e-6
NEG = -1e30

LANES = 128
SUBLANES = 8
SCAN_SEQS = SUBLANES
SSM_BLOCK_GROUPS = LANES // SSM_GROUP
SSM_BLOCK_STATE = SSM_BLOCK_GROUPS * SSM_STATE

MIB = 1024 * 1024


def _params(semantics, vmem_bytes):
    return pltpu.CompilerParams(dimension_semantics=semantics, vmem_limit_bytes=int(vmem_bytes))


def _nbytes(shape, dtype):
    return math.prod(shape) * jnp.dtype(dtype).itemsize


def _rms(x, g):
    ms = jnp.mean(x * x, axis=-1, keepdims=True)
    return x * lax.rsqrt(ms + RMS_EPS) * g


def _sigmoid(x):
    return 1.0 / (1.0 + jnp.exp(-x))


def _gelu_tanh(x):
    c = math.sqrt(2.0 / math.pi)
    return 0.5 * x * (1.0 + jnp.tanh(c * (x + 0.044715 * (x * x * x))))


def _dot(a, b):
    return jnp.dot(a, b, preferred_element_type=F32)


def _dot_nt(a, b):
    return lax.dot_general(a, b, (((1,), (1,)), ((), ())), preferred_element_type=F32)


def _in_proj_kernel(x_ref, xs_ref, g_ref, w_ref, q_ref, kv_ref, rest_ref, kvt_ref,
                    qs_ref, kvs_ref, rests_ref, xn_ref, *, tm, tn, nq, nkv):
    i, j = pl.program_id(0), pl.program_id(1)

    @pl.when(j == 0)
    def _():
        xn_ref[:tm] = _rms(x_ref[...], g_ref[...]).astype(BF16)

    @pl.when((j == 0) & (i == 0))
    def _():
        xn_ref[tm:] = _rms(xs_ref[...], g_ref[...]).astype(BF16)

    def project(out_ref, outs_ref, col, transposed_ref=None):
        @pl.when(i == 0)
        def _():
            z = _dot(xn_ref[...], w_ref[...].astype(BF16))
            out_ref[...] = z[:tm]
            outs_ref[:, pl.ds(pl.multiple_of(col * tn, tn), tn)] = z[tm:]
            if transposed_ref is not None:
                transposed_ref[...] = z[:tm].T

        @pl.when(i > 0)
        def _():
            z = _dot(xn_ref[:tm], w_ref[...].astype(BF16))
            out_ref[...] = z
            if transposed_ref is not None:
                transposed_ref[...] = z.T

    @pl.when(j < nq)
    def _():
        project(q_ref, qs_ref, j)

    @pl.when((j >= nq) & (j < nq + nkv))
    def _():
        project(kv_ref, kvs_ref, j - nq, kvt_ref)

    @pl.when(j >= nq + nkv)
    def _():
        project(rest_ref, rests_ref, j - nq - nkv)


def _in_proj(x, xs, g, w, layer, *, tm, seq, tn=512):
    m, k = x.shape
    ms = xs.shape[0]
    n = w.shape[2]
    nq, nkv = ATT_WIDTH // tn, 2 * ATT_WIDTH // tn
    nrest = n // tn - nq - nkv
    assert seq % tm == 0 and tm % LANES == 0
    tiles_per_seq = seq // tm
    vmem = (2 * (_nbytes((tm, k), F32) + _nbytes((ms, k), F32) + _nbytes((k, tn), F32)
                 + 4 * _nbytes((tm, tn), F32) + _nbytes((ms, n), F32))
            + _nbytes((tm + ms, k), BF16) + _nbytes((k, tn), BF16) + 2 * _nbytes((tm + ms, tn), F32)
            + 2 * MIB)
    kv_col = lambda j: jnp.clip(j - nq, 0, nkv - 1)
    whole = lambda cols: pl.BlockSpec((ms, cols), lambda i, j: (0, 0))
    return pl.pallas_call(
        functools.partial(_in_proj_kernel, tm=tm, tn=tn, nq=nq, nkv=nkv),
        out_shape=(jax.ShapeDtypeStruct((m, nq * tn), F32),
                   jax.ShapeDtypeStruct((m, nkv * tn), F32),
                   jax.ShapeDtypeStruct((m, nrest * tn), F32),
                   jax.ShapeDtypeStruct((m // seq, nkv * tn, seq), F32),
                   jax.ShapeDtypeStruct((ms, nq * tn), F32),
                   jax.ShapeDtypeStruct((ms, nkv * tn), F32),
                   jax.ShapeDtypeStruct((ms, nrest * tn), F32)),
        grid=(m // tm, n // tn),
        in_specs=[pl.BlockSpec((tm, k), lambda i, j: (i, 0)),
                  whole(k),
                  pl.BlockSpec((1, k), lambda i, j: (0, 0)),
                  pl.BlockSpec((None, k, tn), lambda i, j: (layer, 0, j))],
        out_specs=(pl.BlockSpec((tm, tn), lambda i, j: (i, jnp.minimum(j, nq - 1))),
                   pl.BlockSpec((tm, tn), lambda i, j: (i, kv_col(j))),
                   pl.BlockSpec((tm, tn), lambda i, j: (i, jnp.clip(j - nq - nkv, 0, nrest - 1))),
                   pl.BlockSpec((None, tn, tm),
                                lambda i, j: (i // tiles_per_seq, kv_col(j), i % tiles_per_seq)),
                   whole(nq * tn), whole(nkv * tn), whole(nrest * tn)),
        scratch_shapes=[pltpu.VMEM((tm + ms, k), BF16)],
        compiler_params=_params(("arbitrary", "arbitrary"), vmem),
        name="in_proj",
    )(x, xs, g.reshape(1, k), w)


def _attn_prompt_kernel(q_ref, k_ref, v_ref, o_ref, ob_ref, lb_ref, bias_ref):
    s_len = q_ref.shape[0]
    blk = ATT_SPAN
    lane = lax.broadcasted_iota(jnp.int32, (blk, LANES), 1)
    head0 = lane < HEAD_DIM
    scale = HEAD_DIM ** -0.5

    qi = lax.broadcasted_iota(jnp.int32, (2 * blk, 3 * blk), 0) & (blk - 1)
    kj = lax.broadcasted_iota(jnp.int32, (2 * blk, 3 * blk), 1)
    dist = jnp.where(kj < blk, qi - kj, qi + 2 * blk - kj)
    bias_ref[...] = jnp.where((dist >= 0) & (dist <= ATT_SPAN), 0.0, NEG)

    def block(p, d, q0, k0=None):
        def rows(start, n):
            return pl.ds(start, n, stride=d) if d > 1 else pl.ds(start, n)

        qt = q_ref[rows(q0, blk), :] * scale
        qq = jnp.concatenate([jnp.where(head0, qt, 0.0), jnp.where(head0, 0.0, qt)], axis=0).astype(BF16)
        if k0 is None:
            key_rows, bias = rows(q0, blk), bias_ref[:, :blk]
        else:
            key_rows, bias = rows(k0, 2 * blk), bias_ref[:, blk:]
        k = k_ref[key_rows, :].astype(BF16)
        v = v_ref[key_rows, :].astype(BF16)
        s = _dot_nt(qq, k) + bias
        m = jnp.max(s, axis=-1, keepdims=True)
        e = jnp.exp(s - m)
        l = jnp.sum(e, axis=-1, keepdims=True)
        o = _dot(e.astype(BF16), v) * (1.0 / l)
        lse = jnp.broadcast_to(m + jnp.log(l), (2 * blk, LANES))
        ob_ref[p, rows(q0, blk), :] = jnp.where(head0, o[:blk], o[blk:])
        lb_ref[p, rows(q0, blk), :] = jnp.where(head0, lse[:blk], lse[blk:])

    for p, d in enumerate(ATT_DILATIONS):
        nblk = s_len // (d * blk)
        for r in range(d):
            for b in range(nblk):
                block(p, d, r + d * blk * b, None if b == 0 else r + d * blk * (b - 1))

    def merge(c, carry):
        rows = pl.ds(pl.multiple_of(c * blk, blk), blk)
        l0, l1, l2 = lb_ref[0, rows, :], lb_ref[1, rows, :], lb_ref[2, rows, :]
        m = jnp.maximum(jnp.maximum(l0, l1), l2)
        e0, e1, e2 = jnp.exp(l0 - m), jnp.exp(l1 - m), jnp.exp(l2 - m)
        num = e0 * ob_ref[0, rows, :] + e1 * ob_ref[1, rows, :] + e2 * ob_ref[2, rows, :]
        o_ref[rows, :] = num / (e0 + e1 + e2)
        return carry

    lax.fori_loop(0, s_len // blk, merge, 0)


def _attn_prompt(q, kv):
    b, s, _ = q.shape
    assert s % (max(ATT_DILATIONS) * ATT_SPAN) == 0
    npair = ATT_WIDTH // LANES
    blk_bytes = _nbytes((s, LANES), F32)
    vmem = 2 * 4 * blk_bytes + 6 * blk_bytes + 4 * MIB
    return pl.pallas_call(
        _attn_prompt_kernel,
        out_shape=jax.ShapeDtypeStruct((b, s, ATT_WIDTH), F32),
        grid=(b, npair),
        in_specs=[pl.BlockSpec((None, s, LANES), lambda i, j: (i, 0, j)),
                  pl.BlockSpec((None, s, LANES), lambda i, j: (i, 0, j)),
                  pl.BlockSpec((None, s, LANES), lambda i, j: (i, 0, npair + j))],
        out_specs=pl.BlockSpec((None, s, LANES), lambda i, j: (i, 0, j)),
        scratch_shapes=[pltpu.VMEM((len(ATT_DILATIONS), s, LANES), F32),
                        pltpu.VMEM((len(ATT_DILATIONS), s, LANES), F32),
                        pltpu.VMEM((2 * ATT_SPAN, 3 * ATT_SPAN), F32)],
        compiler_params=_params(("parallel", "parallel"), vmem),
        name="attn_prompt",
    )(q, kv, kv)


def _attn_sample_kernel(q_ref, kn_ref, vn_ref, kct_ref, vct_ref, o_ref):
    t_len = q_ref.shape[0]
    w_len = kct_ref.shape[-1]
    pairs = q_ref.shape[1] // LANES
    pad = LANES
    zpad = jnp.zeros((pad - t_len, LANES), F32)

    def multiplicity(dist):
        c = jnp.zeros(dist.shape, F32)
        for d in ATT_DILATIONS:
            ok = (dist >= 0) & ((dist & (d - 1)) == 0) & (dist <= ATT_SPAN * d)
            c = c + jnp.where(ok, 1.0, 0.0)
        return c

    tc = lax.broadcasted_iota(jnp.int32, (t_len, w_len), 0)
    jc = lax.broadcasted_iota(jnp.int32, (t_len, w_len), 1)
    cnt_c = multiplicity(w_len + tc - jc)
    tn = lax.broadcasted_iota(jnp.int32, (t_len, pad), 0)
    jn = lax.broadcasted_iota(jnp.int32, (t_len, pad), 1)
    cnt_n = multiplicity(jnp.where(jn < t_len, tn - jn, -1))

    lane = lax.broadcasted_iota(jnp.int32, (t_len, LANES), 1)
    head0 = lane < HEAD_DIM
    for p in range(pairs):
        lanes = slice(p * LANES, (p + 1) * LANES)
        q = q_ref[:, lanes] * (HEAD_DIM ** -0.5)
        kn = jnp.concatenate([kn_ref[:, lanes], zpad], axis=0).astype(BF16)
        vn = jnp.concatenate([vn_ref[:, lanes], zpad], axis=0).astype(BF16)
        kct = kct_ref[2 * p:2 * p + 2].reshape(LANES, w_len).astype(BF16)
        vct = vct_ref[2 * p:2 * p + 2].reshape(LANES, w_len).astype(BF16)
        outs = []
        for h in range(2):
            hm = head0 if h == 0 else jnp.logical_not(head0)
            qh = jnp.where(hm, q, 0.0).astype(BF16)
            sc = jnp.where(cnt_c > 0, _dot(qh, kct), NEG)
            sn = jnp.where(cnt_n > 0, _dot_nt(qh, kn), NEG)
            m = jnp.maximum(jnp.max(sc, axis=-1, keepdims=True), jnp.max(sn, axis=-1, keepdims=True))
            ec = cnt_c * jnp.exp(sc - m)
            en = cnt_n * jnp.exp(sn - m)
            l = jnp.sum(ec, axis=-1, keepdims=True) + jnp.sum(en, axis=-1, keepdims=True)
            pv = _dot_nt(ec.astype(BF16), vct) + _dot(en.astype(BF16), vn)
            outs.append(pv / l)
        o_ref[:, lanes] = jnp.where(head0, outs[0], outs[1])


def _attn_sample(q, kv, cache_t, layer, *, pairs=4):
    b, t, _ = q.shape
    w = cache_t.shape[-1]
    assert all(d & (d - 1) == 0 for d in ATT_DILATIONS) and t <= LANES
    width = pairs * LANES
    nstep = ATT_WIDTH // width
    vmem = 2 * 2 * _nbytes((width, w), F32) + 10 * _nbytes((LANES, w), F32) * pairs + 4 * MIB
    new_spec = lambda off: pl.BlockSpec((None, t, width), lambda i, j: (i, 0, off + j))
    buf_spec = lambda c: pl.BlockSpec((None, None, None, 2 * pairs, HEAD_DIM, w),
                                      lambda i, j: (layer, i, c, j, 0, 0))
    return pl.pallas_call(
        _attn_sample_kernel,
        out_shape=jax.ShapeDtypeStruct((b, t, ATT_WIDTH), F32),
        grid=(b, nstep),
        in_specs=[new_spec(0), new_spec(0), new_spec(nstep), buf_spec(0), buf_spec(1)],
        out_specs=new_spec(0),
        compiler_params=_params(("parallel", "parallel"), vmem),
        name="attn_sample",
    )(q, kv, kv, cache_t, cache_t)


def _ssm_prep_kernel(are_ref, aim_ref, ldt_ref, bre_ref, bim_ref, abre_ref, abim_ref, bbre_ref, bbim_ref):
    a_re, a_im = are_ref[...], aim_ref[...]
    dt = jnp.exp(ldt_ref[...])
    mag = jnp.exp(a_re * dt)
    ang = a_im * dt
    ab_re, ab_im = mag * jnp.cos(ang), mag * jnp.sin(ang)
    den = a_re * a_re + a_im * a_im
    nr, ni = ab_re - 1.0, ab_im
    cf_re = (nr * a_re + ni * a_im) / den
    cf_im = (ni * a_re - nr * a_im) / den
    abre_ref[...] = ab_re
    abim_ref[...] = ab_im
    for c in range(SSM_GROUP):
        b_re, b_im = bre_ref[c], bim_ref[c]
        bbre_ref[c] = cf_re * b_re - cf_im * b_im
        bbim_ref[c] = cf_re * b_im + cf_im * b_re


def _ssm_prep(a_re, a_im, log_dt, b_re, b_im):
    depth, g, n = a_re.shape
    rows = depth * g
    flat = lambda a: a.reshape(rows, n)
    chan_major = lambda b: jnp.transpose(b.reshape(rows, n, SSM_GROUP), (2, 0, 1))
    ab_re, ab_im, bb_re, bb_im = pl.pallas_call(
        _ssm_prep_kernel,
        out_shape=(jax.ShapeDtypeStruct((rows, n), F32), jax.ShapeDtypeStruct((rows, n), F32),
                   jax.ShapeDtypeStruct((SSM_GROUP, rows, n), F32),
                   jax.ShapeDtypeStruct((SSM_GROUP, rows, n), F32)),
        name="ssm_prep",
    )(flat(a_re), flat(a_im), log_dt.reshape(rows, 1), chan_major(b_re), chan_major(b_im))
    unflat = lambda b: jnp.transpose(b, (1, 0, 2)).reshape(depth, g, SSM_GROUP, n)
    return ab_re.reshape(depth, g, n), ab_im.reshape(depth, g, n), unflat(bb_re), unflat(bb_im)


def _block_diag(m):
    g, r, c = m.shape
    nb = g // SSM_BLOCK_GROUPS
    m = m.reshape(nb, SSM_BLOCK_GROUPS, r, c)
    eye = jnp.eye(SSM_BLOCK_GROUPS, dtype=m.dtype)
    out = m[:, :, :, None, :] * eye[None, :, None, :, None]
    return out.reshape(nb, SSM_BLOCK_GROUPS * r, SSM_BLOCK_GROUPS * c)


def _ssm_kernel(u_ref, h0re_ref, h0im_ref, are_ref, aim_ref, bb_ref, cre_ref, cim_ref, d_ref,
                y_ref, hre_ref, him_ref, x_scr, up_scr, *, steps, chained):
    ns = SSM_BLOCK_STATE
    nrows = steps * SCAN_SEQS
    chunk = min(nrows, 256)

    nchunks = nrows // chunk
    chunk_steps = chunk // SCAN_SEQS
    unroll = min(steps, 8)

    def permute_in(t, c):
        up_scr[pl.ds(pl.multiple_of(t * SCAN_SEQS, SCAN_SEQS), SCAN_SEQS), :] = (
            u_ref[pl.ds(t, SCAN_SEQS, stride=steps), :])
        return c
    lax.fori_loop(0, steps, permute_in, 0, unroll=unroll)

    a_re = jnp.broadcast_to(are_ref[...], (SCAN_SEQS, ns))
    a_im = jnp.broadcast_to(aim_ref[...], (SCAN_SEQS, ns))

    def input_map(c):
        rows = pl.ds(c * chunk, chunk)
        x_scr[rows, :] = _dot(up_scr[rows, :].astype(BF16), bb_ref[...])

    def scan_chunk(c, h, store):
        h_re, h_im = h
        for t in range(c * chunk_steps, (c + 1) * chunk_steps):
            rows = pl.ds(t * SCAN_SEQS, SCAN_SEQS)
            h_re, h_im = (a_re * h_re - a_im * h_im + x_scr[rows, :ns],
                          a_re * h_im + a_im * h_re + x_scr[rows, ns:])
            if store:
                x_scr[rows, :ns] = h_re
                x_scr[rows, ns:] = h_im
        return h_re, h_im

    def output_map(c):
        rows = pl.ds(c * chunk, chunk)
        y = (_dot(x_scr[rows, :ns].astype(BF16), cre_ref[...])
             - _dot(x_scr[rows, ns:].astype(BF16), cim_ref[...])
             + d_ref[...] * up_scr[rows, :])
        up_scr[rows, :] = _gelu_tanh(y)

    init = (h0re_ref[...], h0im_ref[...])
    input_map(0)
    if chained:
        assert steps & (steps - 1) == 0
        e_re, e_im = init
        for c in range(nchunks):
            if c + 1 < nchunks:
                input_map(c + 1)
            e_re, e_im = scan_chunk(c, (e_re, e_im), store=False)
        p_re, p_im = a_re, a_im
        for _ in range(steps.bit_length() - 1):
            p_re, p_im = p_re * p_re - p_im * p_im, 2.0 * p_re * p_im
        sub = lax.broadcasted_iota(jnp.int32, (SCAN_SEQS, ns), 0)
        shift = 1
        while shift < SCAN_SEQS:
            r_re = pltpu.roll(e_re, shift, axis=0)
            r_im = pltpu.roll(e_im, shift, axis=0)
            keep = sub >= shift
            e_re, e_im = (e_re + jnp.where(keep, p_re * r_re - p_im * r_im, 0.0),
                          e_im + jnp.where(keep, p_re * r_im + p_im * r_re, 0.0))
            p_re, p_im = p_re * p_re - p_im * p_im, 2.0 * p_re * p_im
            shift *= 2
        first = sub == 0
        init = (init[0] + jnp.where(first, 0.0, pltpu.roll(e_re, 1, axis=0)),
                init[1] + jnp.where(first, 0.0, pltpu.roll(e_im, 1, axis=0)))
    else:
        for c in range(1, nchunks):
            input_map(c)
    h = init
    for c in range(nchunks):
        h = scan_chunk(c, h, store=True)
        if c > 0:
            output_map(c - 1)
    output_map(nchunks - 1)
    hre_ref[...] = h[0]
    him_ref[...] = h[1]

    def permute_out(t, c):
        y_ref[pl.ds(t, SCAN_SEQS, stride=steps), :] = (
            up_scr[pl.ds(pl.multiple_of(t * SCAN_SEQS, SCAN_SEQS), SCAN_SEQS), :])
        return c
    lax.fori_loop(0, steps, permute_out, 0, unroll=unroll)


def _ssm(u, h0_re, h0_im, ab_re, ab_im, bb, c_re, c_im, d_skip, *, chained):
    nb, rows, _ = u.shape
    steps = rows // SCAN_SEQS
    nblk = SSM_WIDTH // LANES
    ns = SSM_BLOCK_STATE
    vmem = (2 * 2 * _nbytes((rows, LANES), F32) + _nbytes((rows, 2 * ns), F32)
            + _nbytes((rows, LANES), F32) + 4 * _nbytes((min(rows, 256), 2 * ns), F32) + 8 * MIB)
    state_spec = pl.BlockSpec((None, SCAN_SEQS, ns), lambda i, j: (i, 0, j))
    return pl.pallas_call(
        functools.partial(_ssm_kernel, steps=steps, chained=chained),
        out_shape=(jax.ShapeDtypeStruct((nb, rows, SSM_WIDTH), F32),
                   jax.ShapeDtypeStruct(h0_re.shape, F32),
                   jax.ShapeDtypeStruct(h0_im.shape, F32)),
        grid=(nb, nblk),
        in_specs=[pl.BlockSpec((None, rows, LANES), lambda i, j: (i, 0, j)),
                  state_spec, state_spec,
                  pl.BlockSpec((1, ns), lambda i, j: (0, j)),
                  pl.BlockSpec((1, ns), lambda i, j: (0, j)),
                  pl.BlockSpec((None, LANES, 2 * ns), lambda i, j: (j, 0, 0)),
                  pl.BlockSpec((None, ns, LANES), lambda i, j: (j, 0, 0)),
                  pl.BlockSpec((None, ns, LANES), lambda i, j: (j, 0, 0)),
                  pl.BlockSpec((1, LANES), lambda i, j: (0, j))],
        out_specs=(pl.BlockSpec((None, rows, LANES), lambda i, j: (i, 0, j)), state_spec, state_spec),
        scratch_shapes=[pltpu.VMEM((rows, 2 * ns), F32), pltpu.VMEM((rows, LANES), F32)],
        compiler_params=_params(("parallel", "parallel"), vmem),
        name="ssm_chained" if chained else "ssm_step",
    )(u, h0_re, h0_im, ab_re, ab_im, bb, c_re, c_im, d_skip)


def _conv_kernel(h_ref, gb_ref, gc_ref, w_ref, pre_ref, y_ref, st_ref):
    n = h_ref.shape[0]
    u = gc_ref[...] * h_ref[...]
    row = lax.broadcasted_iota(jnp.int32, u.shape, 0)
    pre = pre_ref[...]
    p0, p1 = pre[0:1, :], pre[1:2, :]
    u1 = jnp.where(row == 0, p1, pltpu.roll(u, 1, axis=0))
    u2 = jnp.where(row == 0, p0, jnp.where(row == 1, p1, pltpu.roll(u, 2, axis=0)))
    w = w_ref[...]
    y = w[0:1, :] * u2 + w[1:2, :] * u1 + w[2:3, :] * u
    y_ref[...] = gb_ref[...] * y
    st_ref[...] = u[n - (CONV_K - 1):, :]


def _conv(rest, w, prefix):
    b, n, _ = rest.shape
    assert n >= CONV_K - 1
    nblk = CONV_CH // LANES
    off = SSM_WIDTH // LANES
    col = lambda part: pl.BlockSpec((None, n, LANES), lambda i, j: (i, 0, off + part * nblk + j))
    vmem = 2 * 4 * _nbytes((n, LANES), F32) + 8 * _nbytes((n, LANES), F32) + 4 * MIB
    return pl.pallas_call(
        _conv_kernel,
        out_shape=(jax.ShapeDtypeStruct((b, n, CONV_CH), F32),
                   jax.ShapeDtypeStruct((b, CONV_K - 1, CONV_CH), F32)),
        grid=(b, nblk),
        in_specs=[col(0), col(1), col(2),
                  pl.BlockSpec((CONV_K, LANES), lambda i, j: (0, j)),
                  pl.BlockSpec((None, CONV_K - 1, LANES), lambda i, j: (i, 0, j))],
        out_specs=(pl.BlockSpec((None, n, LANES), lambda i, j: (i, 0, j)),
                   pl.BlockSpec((None, CONV_K - 1, LANES), lambda i, j: (i, 0, j))),
        compiler_params=_params(("parallel", "parallel"), vmem),
        name="conv",
    )(rest, rest, rest, w, prefix)


def _out_proj_kernel(oatt_ref, yssm_ref, yconv_ref, x_ref, oatts_ref, yssms_ref, yconvs_ref, xs_ref,
                     g_ref, wglu_ref, bglu_ref, w_ref, o_ref, os_ref, mix_ref, *, tm, tn):
    i, j = pl.program_id(0), pl.program_id(1)
    a0, a1 = ATT_WIDTH, ATT_WIDTH + SSM_WIDTH

    def mixed(rows, o_att, y_ssm, y_conv):
        g = g_ref[...]
        mix_ref[rows, :a0] = _rms(o_att, g[:, :a0]).astype(BF16)
        y_ssm = y_ssm * _sigmoid(_dot(y_ssm.astype(BF16), wglu_ref[...].astype(BF16)) + bglu_ref[...])
        mix_ref[rows, a0:a1] = _rms(y_ssm, g[:, a0:a1]).astype(BF16)
        mix_ref[rows, a1:] = _rms(y_conv, g[:, a1:]).astype(BF16)

    @pl.when(j == 0)
    def _():
        mixed(slice(0, tm), oatt_ref[...], yssm_ref[...], yconv_ref[...])

    @pl.when((j == 0) & (i == 0))
    def _():
        mixed(slice(tm, None), oatts_ref[...], yssms_ref[...], yconvs_ref[...])

    @pl.when(i == 0)
    def _():
        z = _dot(mix_ref[...], w_ref[...].astype(BF16))
        o_ref[...] = x_ref[...] + z[:tm]
        cols = pl.ds(pl.multiple_of(j * tn, tn), tn)
        os_ref[:, cols] = xs_ref[:, cols] + z[tm:]

    @pl.when(i > 0)
    def _():
        o_ref[...] = x_ref[...] + _dot(mix_ref[:tm], w_ref[...].astype(BF16))


def _out_proj(prompt, sample, g, w_glu, b_glu, w, layer, *, tm, tn=512):
    m, d = prompt[3].shape
    ms = sample[3].shape[0]
    k = w.shape[1]
    vmem = (2 * (_nbytes((tm, k), F32) + 2 * _nbytes((tm, tn), F32) + 3 * _nbytes((ms, k), F32)
                 + _nbytes((k, tn), F32) + _nbytes((SSM_WIDTH, SSM_WIDTH), F32))
            + _nbytes((tm + ms, k), BF16) + _nbytes((k, tn), BF16) + 2 * _nbytes((tm + ms, k), F32)
            + 2 * MIB)
    row = lambda c: pl.BlockSpec((tm, c), lambda i, j: (i, 0))
    whole = lambda c: pl.BlockSpec((ms, c), lambda i, j: (0, 0))
    const = lambda r, c: pl.BlockSpec((r, c), lambda i, j: (0, 0))
    return pl.pallas_call(
        functools.partial(_out_proj_kernel, tm=tm, tn=tn),
        out_shape=(jax.ShapeDtypeStruct((m, d), F32), jax.ShapeDtypeStruct((ms, d), F32)),
        grid=(m // tm, d // tn),
        in_specs=[row(ATT_WIDTH), row(SSM_WIDTH), row(CONV_CH),
                  pl.BlockSpec((tm, tn), lambda i, j: (i, j)),
                  whole(ATT_WIDTH), whole(SSM_WIDTH), whole(CONV_CH), whole(d),
                  const(1, k),
                  pl.BlockSpec((None, SSM_WIDTH, SSM_WIDTH), lambda i, j: (layer, 0, 0)),
                  const(1, SSM_WIDTH),
                  pl.BlockSpec((None, k, tn), lambda i, j: (layer, 0, j))],
        out_specs=(pl.BlockSpec((tm, tn), lambda i, j: (i, j)), whole(d)),
        scratch_shapes=[pltpu.VMEM((tm + ms, k), BF16)],
        compiler_params=_params(("arbitrary", "arbitrary"), vmem),
        name="out_proj",
    )(*prompt, *sample, g.reshape(1, k), w_glu, b_glu.reshape(1, SSM_WIDTH), w)


def _ffn_kernel(x_ref, xs_ref, g_ref, wg_ref, wu_ref, wd_ref, o_ref, os_ref, xn_ref, *, tm):
    i, f = pl.program_id(0), pl.program_id(1)

    @pl.when(f == 0)
    def _():
        x = x_ref[...]
        xn_ref[:tm] = _rms(x, g_ref[...]).astype(BF16)
        o_ref[...] = x

    @pl.when((f == 0) & (i == 0))
    def _():
        xs = xs_ref[...]
        xn_ref[tm:] = _rms(xs, g_ref[...]).astype(BF16)
        os_ref[...] = xs

    def swiglu(xn):
        gate = _dot(xn, wg_ref[...].astype(BF16))
        up = _dot(xn, wu_ref[...].astype(BF16))
        return _dot((gate * _sigmoid(gate) * up).astype(BF16), wd_ref[...].astype(BF16))

    @pl.when(i == 0)
    def _():
        y = swiglu(xn_ref[...])
        o_ref[...] += y[:tm]
        os_ref[...] += y[tm:]

    @pl.when(i > 0)
    def _():
        o_ref[...] += swiglu(xn_ref[:tm])


def _ffn(x, xs, g, w_gate, w_up, w_down, layer, *, tm, tf=256):
    m, d = x.shape
    ms = xs.shape[0]
    ff = w_gate.shape[2]
    vmem = (2 * (2 * _nbytes((tm, d), F32) + 2 * _nbytes((ms, d), F32) + 3 * _nbytes((d, tf), F32))
            + _nbytes((tm + ms, d), BF16) + 3 * _nbytes((d, tf), BF16) + 4 * _nbytes((tm + ms, tf), F32)
            + 2 * MIB)
    return pl.pallas_call(
        functools.partial(_ffn_kernel, tm=tm),
        out_shape=(jax.ShapeDtypeStruct((m, d), F32), jax.ShapeDtypeStruct((ms, d), F32)),
        grid=(m // tm, ff // tf),
        in_specs=[pl.BlockSpec((tm, d), lambda i, f: (i, 0)),
                  pl.BlockSpec((ms, d), lambda i, f: (0, 0)),
                  pl.BlockSpec((1, d), lambda i, f: (0, 0)),
                  pl.BlockSpec((None, d, tf), lambda i, f: (layer, 0, f)),
                  pl.BlockSpec((None, d, tf), lambda i, f: (layer, 0, f)),
                  pl.BlockSpec((None, tf, d), lambda i, f: (layer, f, 0))],
        out_specs=(pl.BlockSpec((tm, d), lambda i, f: (i, 0)),
                   pl.BlockSpec((ms, d), lambda i, f: (0, 0))),
        scratch_shapes=[pltpu.VMEM((tm + ms, d), BF16)],
        compiler_params=_params(("arbitrary", "arbitrary"), vmem),
        name="ffn",
    )(x, xs, g.reshape(1, d), w_gate, w_up, w_down)


def _final_norm_kernel(x_ref, g_ref, o_ref):
    o_ref[...] = _rms(x_ref[...], g_ref[...])


def _final_norm(x, g, *, tm):
    m, d = x.shape
    vmem = 2 * 2 * _nbytes((tm, d), F32) + 3 * _nbytes((tm, d), F32) + 4 * MIB
    return pl.pallas_call(
        _final_norm_kernel,
        out_shape=jax.ShapeDtypeStruct((m, d), F32),
        grid=(m // tm,),
        in_specs=[pl.BlockSpec((tm, d), lambda i: (i, 0)), pl.BlockSpec((1, d), lambda i: (0, 0))],
        out_specs=pl.BlockSpec((tm, d), lambda i: (i, 0)),
        compiler_params=_params(("parallel",), vmem),
        name="final_norm",
    )(x, g.reshape(1, d))


def _mixers(q, kv, rest, lw, layer, *, batch, cache_t, conv_prefix, ssm_h0):
    m = q.shape[0]
    n = m // batch
    q3, kv3, rest3 = (a.reshape(batch, n, a.shape[-1]) for a in (q, kv, rest))
    ssm_w = (lw['ab_re'], lw['ab_im'], lw['bb'], lw['c_re'], lw['c_im'], lw['ssm_d'])
    if cache_t is None:
        o_att = _attn_prompt(q3, kv3)
        y_ssm, h_re, h_im = _ssm(rest3, *ssm_h0, *ssm_w, chained=True)
        h_re, h_im = h_re[:, SCAN_SEQS - 1], h_im[:, SCAN_SEQS - 1]
    else:
        o_att = _attn_sample(q3, kv3, cache_t, layer)
        y_ssm, h_re, h_im = _ssm(rest.reshape(1, m, rest.shape[-1]), *ssm_h0, *ssm_w, chained=False)
        h_re, h_im = h_re[0], h_im[0]
    y_conv, conv_state = _conv(rest3, lw['conv_w'], conv_prefix)
    mixed = (o_att.reshape(m, ATT_WIDTH), y_ssm.reshape(m, SSM_WIDTH), y_conv.reshape(m, CONV_CH))
    return mixed, conv_state, h_re, h_im


def kernel(x_prompt, x_sample, cache_kv, state_conv, state_ssm_re, state_ssm_im, g_mix_norm, w_in,
           ssm_a_re, ssm_a_im, ssm_log_dt, ssm_b_re, ssm_b_im, ssm_c_re, ssm_c_im, ssm_d, ssm_w_glu,
           ssm_b_glu, conv_w, g_out, w_out, g_ffn_norm, w_gate, w_up, w_down, g_final):
    batch, seq, d_model = x_prompt.shape
    dec_batch, dec_seq, _ = x_sample.shape
    depth, n_groups, n_state = ssm_a_re.shape
    n_heads = ATT_WIDTH // HEAD_DIM
    assert dec_batch == SCAN_SEQS and seq % SCAN_SEQS == 0

    ab_re, ab_im, bb_re, bb_im = _ssm_prep(ssm_a_re, ssm_a_im, ssm_log_dt, ssm_b_re, ssm_b_im)
    cache_t = jnp.transpose(cache_kv, (0, 1, 3, 4, 5, 2))

    xp = x_prompt.reshape(batch * seq, d_model)
    xs = x_sample.reshape(dec_batch * dec_seq, d_model)
    zero_state = jnp.zeros((batch, SCAN_SEQS, n_groups * n_state), F32)
    zero_prefix = jnp.zeros((batch, CONV_K - 1, CONV_CH), F32)

    outs_p, outs_s = [], []
    for i in range(depth):
        bb = jnp.concatenate([_block_diag(bb_re[i]), _block_diag(bb_im[i])], axis=-1).astype(BF16)
        c_t = lambda c: _block_diag(jnp.transpose(c[i], (0, 2, 1))).astype(BF16)
        lw = {
            'ab_re': ab_re[i].reshape(1, -1), 'ab_im': ab_im[i].reshape(1, -1), 'bb': bb,
            'c_re': c_t(ssm_c_re), 'c_im': c_t(ssm_c_im), 'ssm_d': ssm_d[i].reshape(1, -1),
            'conv_w': conv_w[i],
        }
        q, kv, rest, kvt, q_s, kv_s, rest_s = _in_proj(xp, xs, g_mix_norm[i], w_in, i, tm=1024, seq=seq)

        mixed_p, c_state, h_re, h_im = _mixers(
            q, kv, rest, lw, i, batch=batch, cache_t=None, conv_prefix=zero_prefix,
            ssm_h0=(zero_state, zero_state))
        kv_p = jnp.transpose(kvt.reshape(batch, 2, n_heads, HEAD_DIM, seq), (0, 4, 1, 2, 3))
        n_keep = min(max(ATT_DILATIONS) * ATT_SPAN, seq)
        outs_p.append((kv_p[:, seq - n_keep:], c_state,
                       h_re.reshape(batch, n_groups, n_state), h_im.reshape(batch, n_groups, n_state)))

        h0 = (state_ssm_re[i].reshape(1, dec_batch, -1), state_ssm_im[i].reshape(1, dec_batch, -1))
        mixed_s, c_state, h_re, h_im = _mixers(
            q_s, kv_s, rest_s, lw, i, batch=dec_batch, cache_t=cache_t, conv_prefix=state_conv[i],
            ssm_h0=h0)
        outs_s.append((kv_s.reshape(dec_batch, dec_seq, 2, n_heads, HEAD_DIM), c_state,
                       h_re.reshape(dec_batch, n_groups, n_state),
                       h_im.reshape(dec_batch, n_groups, n_state)))

        xp, xs = _out_proj((*mixed_p, xp), (*mixed_s, xs), g_out[i], ssm_w_glu, ssm_b_glu[i], w_out, i,
                           tm=512)
        xp, xs = _ffn(xp, xs, g_ffn_norm[i], w_gate, w_up, w_down, i, tm=1024)

    y_prompt = _final_norm(xp, g_final, tm=1024).reshape(batch, seq, d_model)
    y_sample = _final_norm(xs, g_final, tm=dec_batch * dec_seq).reshape(dec_batch, dec_seq, d_model)
    stack = lambda outs, k: jnp.stack([o[k] for o in outs])
    return (y_prompt, y_sample,
            stack(outs_p, 0), stack(outs_p, 1), stack(outs_p, 2), stack(outs_p, 3),
            stack(outs_s, 0), stack(outs_s, 1), stack(outs_s, 2), stack(outs_s, 3))
```

```python
import functools
import math

import jax
import jax.numpy as jnp
from jax import lax
from jax.experimental import pallas as pl
from jax.experimental.pallas import tpu as pltpu

F32 = jnp.float32
BF16 = jnp.bfloat16

HEAD_DIM = 64
ATT_WIDTH = 1024
SSM_WIDTH = 512
CONV_CH = 512
CONV_K = 3
SSM_GROUP = 16
SSM_STATE = 64
ATT_DILATIONS = (1, 4, 16)
ATT_SPAN = 128
RMS_EPS = 1e-6
NEG = -1e30

LANES = 128
SUBLANES = 8
SCAN_SEQS = SUBLANES
SSM_BLOCK_GROUPS = LANES // SSM_GROUP
SSM_BLOCK_STATE = SSM_BLOCK_GROUPS * SSM_STATE

MIB = 1024 * 1024


def _params(semantics, vmem_bytes):
    return pltpu.CompilerParams(dimension_semantics=semantics, vmem_limit_bytes=int(vmem_bytes))


def _nbytes(shape, dtype):
    return math.prod(shape) * jnp.dtype(dtype).itemsize


def _rms(x, g):
    ms = jnp.mean(x * x, axis=-1, keepdims=True)
    return x * lax.rsqrt(ms + RMS_EPS) * g


def _sigmoid(x):
    return 1.0 / (1.0 + jnp.exp(-x))


def _gelu_tanh(x):
    c = math.sqrt(2.0 / math.pi)
    return 0.5 * x * (1.0 + jnp.tanh(c * (x + 0.044715 * (x * x * x))))


def _dot(a, b):
    return jnp.dot(a, b, preferred_element_type=F32)


def _dot_nt(a, b):
    return lax.dot_general(a, b, (((1,), (1,)), ((), ())), preferred_element_type=F32)


def _in_proj_kernel(x_ref, xs_ref, g_ref, w_ref, q_ref, kv_ref, rest_ref, kvt_ref,
                    qs_ref, kvs_ref, rests_ref, xn_ref, *, tm, tn, nq, nkv):
    i, j = pl.program_id(0), pl.program_id(1)

    @pl.when(j == 0)
    def _():
        xn_ref[:tm] = _rms(x_ref[...], g_ref[...]).astype(BF16)

    @pl.when((j == 0) & (i == 0))
    def _():
        xn_ref[tm:] = _rms(xs_ref[...], g_ref[...]).astype(BF16)

    def project(out_ref, outs_ref, col, transposed_ref=None):
        @pl.when(i == 0)
        def _():
            z = _dot(xn_ref[...], w_ref[...].astype(BF16))
            out_ref[...] = z[:tm]
            outs_ref[:, pl.ds(pl.multiple_of(col * tn, tn), tn)] = z[tm:]
            if transposed_ref is not None:
                transposed_ref[...] = z[:tm].T

        @pl.when(i > 0)
        def _():
            z = _dot(xn_ref[:tm], w_ref[...].astype(BF16))
            out_ref[...] = z
            if transposed_ref is not None:
                transposed_ref[...] = z.T

    @pl.when(j < nq)
    def _():
        project(q_ref, qs_ref, j)

    @pl.when((j >= nq) & (j < nq + nkv))
    def _():
        project(kv_ref, kvs_ref, j - nq, kvt_ref)

    @pl.when(j >= nq + nkv)
    def _():
        project(rest_ref, rests_ref, j - nq - nkv)


def _in_proj(x, xs, g, w, layer, *, tm, seq, tn=512):
    m, k = x.shape
    ms = xs.shape[0]
    n = w.shape[2]
    nq, nkv = ATT_WIDTH // tn, 2 * ATT_WIDTH // tn
    nrest = n // tn - nq - nkv
    assert seq % tm == 0 and tm % LANES == 0
    tiles_per_seq = seq // tm
    vmem = (_nbytes((tm, k), F32)
            + 2 * (_nbytes((ms, k), F32) + _nbytes((k, tn), F32) + 4 * _nbytes((tm, tn), F32)
                   + _nbytes((ms, n), F32))
            + _nbytes((tm + ms, k), BF16) + _nbytes((k, tn), BF16) + 3 * _nbytes((tm + ms, tn), F32)
            + 2 * MIB)
    kv_col = lambda j: jnp.clip(j - nq, 0, nkv - 1)
    whole = lambda cols: pl.BlockSpec((ms, cols), lambda i, j: (0, 0))
    return pl.pallas_call(
        functools.partial(_in_proj_kernel, tm=tm, tn=tn, nq=nq, nkv=nkv),
        out_shape=(jax.ShapeDtypeStruct((m, nq * tn), F32),
                   jax.ShapeDtypeStruct((m, nkv * tn), F32),
                   jax.ShapeDtypeStruct((m, nrest * tn), F32),
                   jax.ShapeDtypeStruct((m // seq, nkv * tn, seq), F32),
                   jax.ShapeDtypeStruct((ms, nq * tn), F32),
                   jax.ShapeDtypeStruct((ms, nkv * tn), F32),
                   jax.ShapeDtypeStruct((ms, nrest * tn), F32)),
        grid=(m // tm, n // tn),
        in_specs=[pl.BlockSpec((tm, k), lambda i, j: (i, 0), pipeline_mode=pl.Buffered(1)),
                  whole(k),
                  pl.BlockSpec((1, k), lambda i, j: (0, 0)),
                  pl.BlockSpec((None, k, tn), lambda i, j: (layer, 0, j))],
        out_specs=(pl.BlockSpec((tm, tn), lambda i, j: (i, jnp.minimum(j, nq - 1))),
                   pl.BlockSpec((tm, tn), lambda i, j: (i, kv_col(j))),
                   pl.BlockSpec((tm, tn), lambda i, j: (i, jnp.clip(j - nq - nkv, 0, nrest - 1))),
                   pl.BlockSpec((None, tn, tm),
                                lambda i, j: (i // tiles_per_seq, kv_col(j), i % tiles_per_seq)),
                   whole(nq * tn), whole(nkv * tn), whole(nrest * tn)),
        scratch_shapes=[pltpu.VMEM((tm + ms, k), BF16)],
        compiler_params=_params(("arbitrary", "arbitrary"), vmem),
        name="in_proj",
    )(x, xs, g.reshape(1, k), w)


def _attn_prompt_kernel(q_ref, k_ref, v_ref, o_ref, ob_ref, lb_ref, bias_ref):
    s_len = q_ref.shape[0]
    blk = ATT_SPAN
    lane = lax.broadcasted_iota(jnp.int32, (blk, LANES), 1)
    head0 = lane < HEAD_DIM
    scale = HEAD_DIM ** -0.5

    qi = lax.broadcasted_iota(jnp.int32, (2 * blk, 3 * blk), 0) & (blk - 1)
    kj = lax.broadcasted_iota(jnp.int32, (2 * blk, 3 * blk), 1)
    dist = jnp.where(kj < blk, qi - kj, qi + 2 * blk - kj)
    bias_ref[...] = jnp.where((dist >= 0) & (dist <= ATT_SPAN), 0.0, NEG)

    def block(p, d, q0, k0=None):
        def rows(start, n):
            return pl.ds(start, n, stride=d) if d > 1 else pl.ds(start, n)

        qt = q_ref[rows(q0, blk), :] * scale
        qq = jnp.concatenate([jnp.where(head0, qt, 0.0), jnp.where(head0, 0.0, qt)], axis=0).astype(BF16)
        if k0 is None:
            key_rows, bias = rows(q0, blk), bias_ref[:, :blk]
        else:
            key_rows, bias = rows(k0, 2 * blk), bias_ref[:, blk:]
        k = k_ref[key_rows, :].astype(BF16)
        v = v_ref[key_rows, :].astype(BF16)
        s = _dot_nt(qq, k) + bias
        m = jnp.max(s, axis=-1, keepdims=True)
        e = jnp.exp(s - m)
        l = jnp.sum(e, axis=-1, keepdims=True)
        o = _dot(e.astype(BF16), v) * (1.0 / l)
        lse = jnp.broadcast_to(m + jnp.log(l), (2 * blk, LANES))
        ob_ref[p, rows(q0, blk), :] = jnp.where(head0, o[:blk], o[blk:])
        lb_ref[p, rows(q0, blk), :] = jnp.where(head0, lse[:blk], lse[blk:])

    for p, d in enumerate(ATT_DILATIONS):
        nblk = s_len // (d * blk)
        for r in range(d):
            for b in range(nblk):
                block(p, d, r + d * blk * b, None if b == 0 else r + d * blk * (b - 1))

    def merge(c, carry):
        rows = pl.ds(pl.multiple_of(c * blk, blk), blk)
        l0, l1, l2 = lb_ref[0, rows, :], lb_ref[1, rows, :], lb_ref[2, rows, :]
        m = jnp.maximum(jnp.maximum(l0, l1), l2)
        e0, e1, e2 = jnp.exp(l0 - m), jnp.exp(l1 - m), jnp.exp(l2 - m)
        num = e0 * ob_ref[0, rows, :] + e1 * ob_ref[1, rows, :] + e2 * ob_ref[2, rows, :]
        o_ref[rows, :] = num / (e0 + e1 + e2)
        return carry

    lax.fori_loop(0, s_len // blk, merge, 0)


def _attn_prompt(q, kv):
    b, s, _ = q.shape
    assert s % (max(ATT_DILATIONS) * ATT_SPAN) == 0
    npair = ATT_WIDTH // LANES
    blk_bytes = _nbytes((s, LANES), F32)
    vmem = 2 * 4 * blk_bytes + 6 * blk_bytes + 4 * MIB
    return pl.pallas_call(
        _attn_prompt_kernel,
        out_shape=jax.ShapeDtypeStruct((b, s, ATT_WIDTH), F32),
        grid=(b, npair),
        in_specs=[pl.BlockSpec((None, s, LANES), lambda i, j: (i, 0, j)),
                  pl.BlockSpec((None, s, LANES), lambda i, j: (i, 0, j)),
                  pl.BlockSpec((None, s, LANES), lambda i, j: (i, 0, npair + j))],
        out_specs=pl.BlockSpec((None, s, LANES), lambda i, j: (i, 0, j)),
        scratch_shapes=[pltpu.VMEM((len(ATT_DILATIONS), s, LANES), F32),
                        pltpu.VMEM((len(ATT_DILATIONS), s, LANES), F32),
                        pltpu.VMEM((2 * ATT_SPAN, 3 * ATT_SPAN), F32)],
        compiler_params=_params(("parallel", "parallel"), vmem),
        name="attn_prompt",
    )(q, kv, kv)


def _attn_sample_kernel(q_ref, kn_ref, vn_ref, kct_ref, vct_ref, o_ref):
    t_len = q_ref.shape[0]
    w_len = kct_ref.shape[-1]
    pairs = q_ref.shape[1] // LANES
    pad = LANES
    zpad = jnp.zeros((pad - t_len, LANES), F32)

    def multiplicity(dist):
        c = jnp.zeros(dist.shape, F32)
        for d in ATT_DILATIONS:
            ok = (dist >= 0) & ((dist & (d - 1)) == 0) & (dist <= ATT_SPAN * d)
            c = c + jnp.where(ok, 1.0, 0.0)
        return c

    tc = lax.broadcasted_iota(jnp.int32, (t_len, w_len), 0)
    jc = lax.broadcasted_iota(jnp.int32, (t_len, w_len), 1)
    cnt_c = multiplicity(w_len + tc - jc)
    tn = lax.broadcasted_iota(jnp.int32, (t_len, pad), 0)
    jn = lax.broadcasted_iota(jnp.int32, (t_len, pad), 1)
    cnt_n = multiplicity(jnp.where(jn < t_len, tn - jn, -1))

    lane = lax.broadcasted_iota(jnp.int32, (t_len, LANES), 1)
    head0 = lane < HEAD_DIM
    for p in range(pairs):
        lanes = slice(p * LANES, (p + 1) * LANES)
        q = q_ref[:, lanes] * (HEAD_DIM ** -0.5)
        kn = jnp.concatenate([kn_ref[:, lanes], zpad], axis=0).astype(BF16)
        vn = jnp.concatenate([vn_ref[:, lanes], zpad], axis=0).astype(BF16)
        kct = kct_ref[2 * p:2 * p + 2].reshape(LANES, w_len).astype(BF16)
        vct = vct_ref[2 * p:2 * p + 2].reshape(LANES, w_len).astype(BF16)
        outs = []
        for h in range(2):
            hm = head0 if h == 0 else jnp.logical_not(head0)
            qh = jnp.where(hm, q, 0.0).astype(BF16)
            sc = jnp.where(cnt_c > 0, _dot(qh, kct), NEG)
            sn = jnp.where(cnt_n > 0, _dot_nt(qh, kn), NEG)
            m = jnp.maximum(jnp.max(sc, axis=-1, keepdims=True), jnp.max(sn, axis=-1, keepdims=True))
            ec = cnt_c * jnp.exp(sc - m)
            en = cnt_n * jnp.exp(sn - m)
            l = jnp.sum(ec, axis=-1, keepdims=True) + jnp.sum(en, axis=-1, keepdims=True)
            pv = _dot_nt(ec.astype(BF16), vct) + _dot(en.astype(BF16), vn)
            outs.append(pv / l)
        o_ref[:, lanes] = jnp.where(head0, outs[0], outs[1])


def _attn_sample(q, kv, cache_t, layer, *, pairs=4):
    b, t, _ = q.shape
    w = cache_t.shape[-1]
    assert all(d & (d - 1) == 0 for d in ATT_DILATIONS) and t <= LANES
    width = pairs * LANES
    nstep = ATT_WIDTH // width
    vmem = 2 * 2 * _nbytes((width, w), F32) + 10 * _nbytes((LANES, w), F32) * pairs + 4 * MIB
    new_spec = lambda off: pl.BlockSpec((None, t, width), lambda i, j: (i, 0, off + j))
    buf_spec = lambda c: pl.BlockSpec((None, None, None, 2 * pairs, HEAD_DIM, w),
                                      lambda i, j: (layer, i, c, j, 0, 0))
    return pl.pallas_call(
        _attn_sample_kernel,
        out_shape=jax.ShapeDtypeStruct((b, t, ATT_WIDTH), F32),
        grid=(b, nstep),
        in_specs=[new_spec(0), new_spec(0), new_spec(nstep), buf_spec(0), buf_spec(1)],
        out_specs=new_spec(0),
        compiler_params=_params(("parallel", "parallel"), vmem),
        name="attn_sample",
    )(q, kv, kv, cache_t, cache_t)


def _ssm_prep_kernel(are_ref, aim_ref, ldt_ref, bre_ref, bim_ref, abre_ref, abim_ref, bbre_ref, bbim_ref):
    a_re, a_im = are_ref[...], aim_ref[...]
    dt = jnp.exp(ldt_ref[...])
    mag = jnp.exp(a_re * dt)
    ang = a_im * dt
    ab_re, ab_im = mag * jnp.cos(ang), mag * jnp.sin(ang)
    den = a_re * a_re + a_im * a_im
    nr, ni = ab_re - 1.0, ab_im
    cf_re = (nr * a_re + ni * a_im) / den
    cf_im = (ni * a_re - nr * a_im) / den
    abre_ref[...] = ab_re
    abim_ref[...] = ab_im
    for c in range(SSM_GROUP):
        b_re, b_im = bre_ref[c], bim_ref[c]
        bbre_ref[c] = cf_re * b_re - cf_im * b_im
        bbim_ref[c] = cf_re * b_im + cf_im * b_re


def _ssm_prep(a_re, a_im, log_dt, b_re, b_im):
    depth, g, n = a_re.shape
    rows = depth * g
    flat = lambda a: a.reshape(rows, n)
    chan_major = lambda b: jnp.transpose(b.reshape(rows, n, SSM_GROUP), (2, 0, 1))
    ab_re, ab_im, bb_re, bb_im = pl.pallas_call(
        _ssm_prep_kernel,
        out_shape=(jax.ShapeDtypeStruct((rows, n), F32), jax.ShapeDtypeStruct((rows, n), F32),
                   jax.ShapeDtypeStruct((SSM_GROUP, rows, n), F32),
                   jax.ShapeDtypeStruct((SSM_GROUP, rows, n), F32)),
        name="ssm_prep",
    )(flat(a_re), flat(a_im), log_dt.reshape(rows, 1), chan_major(b_re), chan_major(b_im))
    unflat = lambda b: jnp.transpose(b, (1, 0, 2)).reshape(depth, g, SSM_GROUP, n)
    return ab_re.reshape(depth, g, n), ab_im.reshape(depth, g, n), unflat(bb_re), unflat(bb_im)


def _block_diag(m):
    g, r, c = m.shape
    nb = g // SSM_BLOCK_GROUPS
    m = m.reshape(nb, SSM_BLOCK_GROUPS, r, c)
    eye = jnp.eye(SSM_BLOCK_GROUPS, dtype=m.dtype)
    out = m[:, :, :, None, :] * eye[None, :, None, :, None]
    return out.reshape(nb, SSM_BLOCK_GROUPS * r, SSM_BLOCK_GROUPS * c)


def _ssm_kernel(u_ref, h0re_ref, h0im_ref, are_ref, aim_ref, bb_ref, cre_ref, cim_ref, d_ref,
                y_ref, hre_ref, him_ref, x_scr, up_scr, *, steps, chained):
    ns = SSM_BLOCK_STATE
    nrows = steps * SCAN_SEQS
    chunk = min(nrows, 256)

    nchunks = nrows // chunk
    chunk_steps = chunk // SCAN_SEQS
    unroll = min(steps, 8)

    def permute_in(t, c):
        up_scr[pl.ds(pl.multiple_of(t * SCAN_SEQS, SCAN_SEQS), SCAN_SEQS), :] = (
            u_ref[pl.ds(t, SCAN_SEQS, stride=steps), :])
        return c
    lax.fori_loop(0, steps, permute_in, 0, unroll=unroll)

    a_re = jnp.broadcast_to(are_ref[...], (SCAN_SEQS, ns))
    a_im = jnp.broadcast_to(aim_ref[...], (SCAN_SEQS, ns))

    def input_map(c):
        rows = pl.ds(c * chunk, chunk)
        x_scr[rows, :] = _dot(up_scr[rows, :].astype(BF16), bb_ref[...])

    def scan_chunk(c, h, store):
        h_re, h_im = h
        for t in range(c * chunk_steps, (c + 1) * chunk_steps):
            rows = pl.ds(t * SCAN_SEQS, SCAN_SEQS)
            h_re, h_im = (a_re * h_re - a_im * h_im + x_scr[rows, :ns],
                          a_re * h_im + a_im * h_re + x_scr[rows, ns:])
            if store:
                x_scr[rows, :ns] = h_re
                x_scr[rows, ns:] = h_im
        return h_re, h_im

    def output_map(c):
        rows = pl.ds(c * chunk, chunk)
        y = (_dot(x_scr[rows, :ns].astype(BF16), cre_ref[...])
             - _dot(x_scr[rows, ns:].astype(BF16), cim_ref[...])
             + d_ref[...] * up_scr[rows, :])
        up_scr[rows, :] = _gelu_tanh(y)

    init = (h0re_ref[...], h0im_ref[...])
    input_map(0)
    if chained:
        assert steps & (steps - 1) == 0
        e_re, e_im = init
        for c in range(nchunks):
            if c + 1 < nchunks:
                input_map(c + 1)
            e_re, e_im = scan_chunk(c, (e_re, e_im), store=False)
        p_re, p_im = a_re, a_im
        for _ in range(steps.bit_length() - 1):
            p_re, p_im = p_re * p_re - p_im * p_im, 2.0 * p_re * p_im
        sub = lax.broadcasted_iota(jnp.int32, (SCAN_SEQS, ns), 0)
        shift = 1
        while shift < SCAN_SEQS:
            r_re = pltpu.roll(e_re, shift, axis=0)
            r_im = pltpu.roll(e_im, shift, axis=0)
            keep = sub >= shift
            e_re, e_im = (e_re + jnp.where(keep, p_re * r_re - p_im * r_im, 0.0),
                          e_im + jnp.where(keep, p_re * r_im + p_im * r_re, 0.0))
            p_re, p_im = p_re * p_re - p_im * p_im, 2.0 * p_re * p_im
            shift *= 2
        first = sub == 0
        init = (init[0] + jnp.where(first, 0.0, pltpu.roll(e_re, 1, axis=0)),
                init[1] + jnp.where(first, 0.0, pltpu.roll(e_im, 1, axis=0)))
    else:
        for c in range(1, nchunks):
            input_map(c)
    h = init
    for c in range(nchunks):
        h = scan_chunk(c, h, store=True)
        if c > 0:
            output_map(c - 1)
    output_map(nchunks - 1)
    hre_ref[...] = h[0]
    him_ref[...] = h[1]

    def permute_out(t, c):
        y_ref[pl.ds(t, SCAN_SEQS, stride=steps), :] = (
            up_scr[pl.ds(pl.multiple_of(t * SCAN_SEQS, SCAN_SEQS), SCAN_SEQS), :])
        return c
    lax.fori_loop(0, steps, permute_out, 0, unroll=unroll)


def _ssm(u, h0_re, h0_im, ab_re, ab_im, bb, c_re, c_im, d_skip, *, chained):
    nb, rows, _ = u.shape
    steps = rows // SCAN_SEQS
    nblk = SSM_WIDTH // LANES
    ns = SSM_BLOCK_STATE
    vmem = (2 * 2 * _nbytes((rows, LANES), F32) + _nbytes((rows, 2 * ns), F32)
            + _nbytes((rows, LANES), F32) + 4 * _nbytes((min(rows, 256), 2 * ns), F32) + 8 * MIB)
    state_spec = pl.BlockSpec((None, SCAN_SEQS, ns), lambda i, j: (i, 0, j))
    return pl.pallas_call(
        functools.partial(_ssm_kernel, steps=steps, chained=chained),
        out_shape=(jax.ShapeDtypeStruct((nb, rows, SSM_WIDTH), F32),
                   jax.ShapeDtypeStruct(h0_re.shape, F32),
                   jax.ShapeDtypeStruct(h0_im.shape, F32)),
        grid=(nb, nblk),
        in_specs=[pl.BlockSpec((None, rows, LANES), lambda i, j: (i, 0, j)),
                  state_spec, state_spec,
                  pl.BlockSpec((1, ns), lambda i, j: (0, j)),
                  pl.BlockSpec((1, ns), lambda i, j: (0, j)),
                  pl.BlockSpec((None, LANES, 2 * ns), lambda i, j: (j, 0, 0)),
                  pl.BlockSpec((None, ns, LANES), lambda i, j: (j, 0, 0)),
                  pl.BlockSpec((None, ns, LANES), lambda i, j: (j, 0, 0)),
                  pl.BlockSpec((1, LANES), lambda i, j: (0, j))],
        out_specs=(pl.BlockSpec((None, rows, LANES), lambda i, j: (i, 0, j)), state_spec, state_spec),
        scratch_shapes=[pltpu.VMEM((rows, 2 * ns), F32), pltpu.VMEM((rows, LANES), F32)],
        compiler_params=_params(("parallel", "parallel"), vmem),
        name="ssm_chained" if chained else "ssm_step",
    )(u, h0_re, h0_im, ab_re, ab_im, bb, c_re, c_im, d_skip)


def _conv_kernel(h_ref, gb_ref, gc_ref, w_ref, pre_ref, y_ref, st_ref):
    n = h_ref.shape[0]
    u = gc_ref[...] * h_ref[...]
    row = lax.broadcasted_iota(jnp.int32, u.shape, 0)
    pre = pre_ref[...]
    p0, p1 = pre[0:1, :], pre[1:2, :]
    u1 = jnp.where(row == 0, p1, pltpu.roll(u, 1, axis=0))
    u2 = jnp.where(row == 0, p0, jnp.where(row == 1, p1, pltpu.roll(u, 2, axis=0)))
    w = w_ref[...]
    y = w[0:1, :] * u2 + w[1:2, :] * u1 + w[2:3, :] * u
    y_ref[...] = gb_ref[...] * y
    st_ref[...] = u[n - (CONV_K - 1):, :]


def _conv(rest, w, prefix):
    b, n, _ = rest.shape
    assert n >= CONV_K - 1
    nblk = CONV_CH // LANES
    off = SSM_WIDTH // LANES
    col = lambda part: pl.BlockSpec((None, n, LANES), lambda i, j: (i, 0, off + part * nblk + j))
    vmem = 2 * 4 * _nbytes((n, LANES), F32) + 8 * _nbytes((n, LANES), F32) + 4 * MIB
    return pl.pallas_call(
        _conv_kernel,
        out_shape=(jax.ShapeDtypeStruct((b, n, CONV_CH), F32),
                   jax.ShapeDtypeStruct((b, CONV_K - 1, CONV_CH), F32)),
        grid=(b, nblk),
        in_specs=[col(0), col(1), col(2),
                  pl.BlockSpec((CONV_K, LANES), lambda i, j: (0, j)),
                  pl.BlockSpec((None, CONV_K - 1, LANES), lambda i, j: (i, 0, j))],
        out_specs=(pl.BlockSpec((None, n, LANES), lambda i, j: (i, 0, j)),
                   pl.BlockSpec((None, CONV_K - 1, LANES), lambda i, j: (i, 0, j))),
        compiler_params=_params(("parallel", "parallel"), vmem),
        name="conv",
    )(rest, rest, rest, w, prefix)


def _out_proj_kernel(oatt_ref, yssm_ref, yconv_ref, x_ref, oatts_ref, yssms_ref, yconvs_ref, xs_ref,
                     g_ref, wglu_ref, bglu_ref, w_ref, o_ref, os_ref, mix_ref, *, tm, tn):
    i, j = pl.program_id(0), pl.program_id(1)
    a0, a1 = ATT_WIDTH, ATT_WIDTH + SSM_WIDTH

    def mixed(rows, o_att, y_ssm, y_conv):
        g = g_ref[...]
        mix_ref[rows, :a0] = _rms(o_att, g[:, :a0]).astype(BF16)
        y_ssm = y_ssm * _sigmoid(_dot(y_ssm.astype(BF16), wglu_ref[...].astype(BF16)) + bglu_ref[...])
        mix_ref[rows, a0:a1] = _rms(y_ssm, g[:, a0:a1]).astype(BF16)
        mix_ref[rows, a1:] = _rms(y_conv, g[:, a1:]).astype(BF16)

    @pl.when(j == 0)
    def _():
        mixed(slice(0, tm), oatt_ref[...], yssm_ref[...], yconv_ref[...])

    @pl.when((j == 0) & (i == 0))
    def _():
        mixed(slice(tm, None), oatts_ref[...], yssms_ref[...], yconvs_ref[...])

    @pl.when(i == 0)
    def _():
        z = _dot(mix_ref[...], w_ref[...].astype(BF16))
        o_ref[...] = x_ref[...] + z[:tm]
        cols = pl.ds(pl.multiple_of(j * tn, tn), tn)
        os_ref[:, cols] = xs_ref[:, cols] + z[tm:]

    @pl.when(i > 0)
    def _():
        o_ref[...] = x_ref[...] + _dot(mix_ref[:tm], w_ref[...].astype(BF16))


def _out_proj(prompt, sample, g, w_glu, b_glu, w, layer, *, tm, tn=512):
    m, d = prompt[3].shape
    ms = sample[3].shape[0]
    k = w.shape[1]
    vmem = (2 * (_nbytes((tm, k), F32) + 2 * _nbytes((tm, tn), F32) + 3 * _nbytes((ms, k), F32)
                 + _nbytes((k, tn), F32) + _nbytes((SSM_WIDTH, SSM_WIDTH), F32))
            + _nbytes((tm + ms, k), BF16) + _nbytes((k, tn), BF16) + 2 * _nbytes((tm + ms, k), F32)
            + 2 * MIB)
    row = lambda c: pl.BlockSpec((tm, c), lambda i, j: (i, 0))
    whole = lambda c: pl.BlockSpec((ms, c), lambda i, j: (0, 0))
    const = lambda r, c: pl.BlockSpec((r, c), lambda i, j: (0, 0))
    return pl.pallas_call(
        functools.partial(_out_proj_kernel, tm=tm, tn=tn),
        out_shape=(jax.ShapeDtypeStruct((m, d), F32), jax.ShapeDtypeStruct((ms, d), F32)),
        grid=(m // tm, d // tn),
        in_specs=[row(ATT_WIDTH), row(SSM_WIDTH), row(CONV_CH),
                  pl.BlockSpec((tm, tn), lambda i, j: (i, j)),
                  whole(ATT_WIDTH), whole(SSM_WIDTH), whole(CONV_CH), whole(d),
                  const(1, k),
                  pl.BlockSpec((None, SSM_WIDTH, SSM_WIDTH), lambda i, j: (layer, 0, 0)),
                  const(1, SSM_WIDTH),
                  pl.BlockSpec((None, k, tn), lambda i, j: (layer, 0, j))],
        out_specs=(pl.BlockSpec((tm, tn), lambda i, j: (i, j)), whole(d)),
        scratch_shapes=[pltpu.VMEM((tm + ms, k), BF16)],
        compiler_params=_params(("arbitrary", "arbitrary"), vmem),
        name="out_proj",
    )(*prompt, *sample, g.reshape(1, k), w_glu, b_glu.reshape(1, SSM_WIDTH), w)


def _ffn_kernel(x_ref, xs_ref, g_ref, gfin_ref, wg_ref, wu_ref, wd_ref, o_ref, os_ref, xn_ref, *,
                tm, final):
    i, f = pl.program_id(0), pl.program_id(1)
    last = f == pl.num_programs(1) - 1

    @pl.when(f == 0)
    def _():
        x = x_ref[...]
        xn_ref[:tm] = _rms(x, g_ref[...]).astype(BF16)
        o_ref[...] = x

    @pl.when((f == 0) & (i == 0))
    def _():
        xs = xs_ref[...]
        xn_ref[tm:] = _rms(xs, g_ref[...]).astype(BF16)
        os_ref[...] = xs

    def swiglu(xn):
        gate = _dot(xn, wg_ref[...].astype(BF16))
        up = _dot(xn, wu_ref[...].astype(BF16))
        return _dot((gate * _sigmoid(gate) * up).astype(BF16), wd_ref[...].astype(BF16))

    @pl.when(i == 0)
    def _():
        y = swiglu(xn_ref[...])
        o_ref[...] += y[:tm]
        os_ref[...] += y[tm:]

    @pl.when(i > 0)
    def _():
        o_ref[...] += swiglu(xn_ref[:tm])

    if final:
        @pl.when(last)
        def _():
            o_ref[...] = _rms(o_ref[...], gfin_ref[...])

        @pl.when(last & (i == 0))
        def _():
            os_ref[...] = _rms(os_ref[...], gfin_ref[...])


def _ffn(x, xs, g, w_gate, w_up, w_down, layer, *, tm, g_final=None, tf=256):
    final = g_final is not None
    if not final:
        g_final = g
    m, d = x.shape
    ms = xs.shape[0]
    ff = w_gate.shape[2]
    vmem = (2 * (2 * _nbytes((tm, d), F32) + 2 * _nbytes((ms, d), F32) + 3 * _nbytes((d, tf), F32))
            + _nbytes((tm + ms, d), BF16) + 3 * _nbytes((d, tf), BF16) + 4 * _nbytes((tm + ms, tf), F32)
            + 2 * MIB)
    return pl.pallas_call(
        functools.partial(_ffn_kernel, tm=tm, final=final),
        out_shape=(jax.ShapeDtypeStruct((m, d), F32), jax.ShapeDtypeStruct((ms, d), F32)),
        grid=(m // tm, ff // tf),
        in_specs=[pl.BlockSpec((tm, d), lambda i, f: (i, 0)),
                  pl.BlockSpec((ms, d), lambda i, f: (0, 0)),
                  pl.BlockSpec((1, d), lambda i, f: (0, 0)),
                  pl.BlockSpec((1, d), lambda i, f: (0, 0)),
                  pl.BlockSpec((None, d, tf), lambda i, f: (layer, 0, f)),
                  pl.BlockSpec((None, d, tf), lambda i, f: (layer, 0, f)),
                  pl.BlockSpec((None, tf, d), lambda i, f: (layer, f, 0))],
        out_specs=(pl.BlockSpec((tm, d), lambda i, f: (i, 0)),
                   pl.BlockSpec((ms, d), lambda i, f: (0, 0))),
        scratch_shapes=[pltpu.VMEM((tm + ms, d), BF16)],
        compiler_params=_params(("arbitrary", "arbitrary"), vmem),
        name="ffn",
    )(x, xs, g.reshape(1, d), g_final.reshape(1, d), w_gate, w_up, w_down)


def _mixers(q, kv, rest, lw, layer, *, batch, cache_t, conv_prefix, ssm_h0):
    m = q.shape[0]
    n = m // batch
    q3, kv3, rest3 = (a.reshape(batch, n, a.shape[-1]) for a in (q, kv, rest))
    ssm_w = (lw['ab_re'], lw['ab_im'], lw['bb'], lw['c_re'], lw['c_im'], lw['ssm_d'])
    if cache_t is None:
        o_att = _attn_prompt(q3, kv3)
        y_ssm, h_re, h_im = _ssm(rest3, *ssm_h0, *ssm_w, chained=True)
        h_re, h_im = h_re[:, SCAN_SEQS - 1], h_im[:, SCAN_SEQS - 1]
    else:
        o_att = _attn_sample(q3, kv3, cache_t, layer)
        y_ssm, h_re, h_im = _ssm(rest.reshape(1, m, rest.shape[-1]), *ssm_h0, *ssm_w, chained=False)
        h_re, h_im = h_re[0], h_im[0]
    y_conv, conv_state = _conv(rest3, lw['conv_w'], conv_prefix)
    mixed = (o_att.reshape(m, ATT_WIDTH), y_ssm.reshape(m, SSM_WIDTH), y_conv.reshape(m, CONV_CH))
    return mixed, conv_state, h_re, h_im


def kernel(x_prompt, x_sample, cache_kv, state_conv, state_ssm_re, state_ssm_im, g_mix_norm, w_in,
           ssm_a_re, ssm_a_im, ssm_log_dt, ssm_b_re, ssm_b_im, ssm_c_re, ssm_c_im, ssm_d, ssm_w_glu,
           ssm_b_glu, conv_w, g_out, w_out, g_ffn_norm, w_gate, w_up, w_down, g_final):
    batch, seq, d_model = x_prompt.shape
    dec_batch, dec_seq, _ = x_sample.shape
    depth, n_groups, n_state = ssm_a_re.shape
    n_heads = ATT_WIDTH // HEAD_DIM
    assert dec_batch == SCAN_SEQS and seq % SCAN_SEQS == 0

    ab_re, ab_im, bb_re, bb_im = _ssm_prep(ssm_a_re, ssm_a_im, ssm_log_dt, ssm_b_re, ssm_b_im)
    cache_t = jnp.transpose(cache_kv, (0, 1, 3, 4, 5, 2))

    xp = x_prompt.reshape(batch * seq, d_model)
    xs = x_sample.reshape(dec_batch * dec_seq, d_model)
    zero_state = jnp.zeros((batch, SCAN_SEQS, n_groups * n_state), F32)
    zero_prefix = jnp.zeros((batch, CONV_K - 1, CONV_CH), F32)

    outs_p, outs_s = [], []
    for i in range(depth):
        bb = jnp.concatenate([_block_diag(bb_re[i]), _block_diag(bb_im[i])], axis=-1).astype(BF16)
        c_t = lambda c: _block_diag(jnp.transpose(c[i], (0, 2, 1))).astype(BF16)
        lw = {
            'ab_re': ab_re[i].reshape(1, -1), 'ab_im': ab_im[i].reshape(1, -1), 'bb': bb,
            'c_re': c_t(ssm_c_re), 'c_im': c_t(ssm_c_im), 'ssm_d': ssm_d[i].reshape(1, -1),
            'conv_w': conv_w[i],
        }
        q, kv, rest, kvt, q_s, kv_s, rest_s = _in_proj(xp, xs, g_mix_norm[i], w_in, i, tm=2048, seq=seq,
                                                       tn=256)

        mixed_p, c_state, h_re, h_im = _mixers(
            q, kv, rest, lw, i, batch=batch, cache_t=None, conv_prefix=zero_prefix,
            ssm_h0=(zero_state, zero_state))
        kv_p = jnp.transpose(kvt.reshape(batch, 2, n_heads, HEAD_DIM, seq), (0, 4, 1, 2, 3))
        n_keep = min(max(ATT_DILATIONS) * ATT_SPAN, seq)
        outs_p.append((kv_p[:, seq - n_keep:], c_state,
                       h_re.reshape(batch, n_groups, n_state), h_im.reshape(batch, n_groups, n_state)))

        h0 = (state_ssm_re[i].reshape(1, dec_batch, -1), state_ssm_im[i].reshape(1, dec_batch, -1))
        mixed_s, c_state, h_re, h_im = _mixers(
            q_s, kv_s, rest_s, lw, i, batch=dec_batch, cache_t=cache_t, conv_prefix=state_conv[i],
            ssm_h0=h0)
        outs_s.append((kv_s.reshape(dec_batch, dec_seq, 2, n_heads, HEAD_DIM), c_state,
                       h_re.reshape(dec_batch, n_groups, n_state),
                       h_im.reshape(dec_batch, n_groups, n_state)))

        xp, xs = _out_proj((*mixed_p, xp), (*mixed_s, xs), g_out[i], ssm_w_glu, ssm_b_glu[i], w_out, i,
                           tm=1024)
        xp, xs = _ffn(xp, xs, g_ffn_norm[i], w_gate, w_up, w_down, i, tm=1024,
                      g_final=g_final if i == depth - 1 else None)

    y_prompt = xp.reshape(batch, seq, d_model)
    y_sample = xs.reshape(dec_batch, dec_seq, d_model)
    stack = lambda outs, k: jnp.stack([o[k] for o in outs])
    return (y_prompt, y_sample,
            stack(outs_p, 0), stack(outs_p, 1), stack(outs_p, 2), stack(outs_p, 3),
            stack(outs_s, 0), stack(outs_s, 1), stack(outs_s, 2), stack(outs_s, 3))
```

```python
import functools
import math

import jax
import jax.numpy as jnp
from jax import lax
from jax.experimental import pallas as pl
from jax.experimental.pallas import tpu as pltpu

F32 = jnp.float32
BF16 = jnp.bfloat16

HEAD_DIM = 64
ATT_WIDTH = 1024
SSM_WIDTH = 512
CONV_CH = 512
CONV_K = 3
SSM_GROUP = 16
SSM_STATE = 64
ATT_DILATIONS = (1, 4, 16)
ATT_SPAN = 128
RMS_EPS = 1e-6
NEG = -1e30

LANES = 128
SUBLANES = 8
SCAN_SEQS = SUBLANES
SSM_BLOCK_GROUPS = LANES // SSM_GROUP
SSM_BLOCK_STATE = SSM_BLOCK_GROUPS * SSM_STATE

MIB = 1024 * 1024
V7X_VMEM_BYTES = 64 * MIB


def _params(semantics, vmem_bytes, hbm_result_bytes=None):
    if hbm_result_bytes is not None and hbm_result_bytes >= V7X_VMEM_BYTES // 8:
        vmem_bytes = max(vmem_bytes, V7X_VMEM_BYTES - hbm_result_bytes + MIB)
    return pltpu.CompilerParams(dimension_semantics=semantics, vmem_limit_bytes=int(vmem_bytes))


def _nbytes(shape, dtype):
    return math.prod(shape) * jnp.dtype(dtype).itemsize


def _rms(x, g):
    ms = jnp.mean(x * x, axis=-1, keepdims=True)
    return x * lax.rsqrt(ms + RMS_EPS) * g


def _sigmoid(x):
    return 1.0 / (1.0 + jnp.exp(-x))


def _gelu_tanh(x):
    c = math.sqrt(2.0 / math.pi)
    return 0.5 * x * (1.0 + jnp.tanh(c * (x + 0.044715 * (x * x * x))))


def _dot(a, b):
    return jnp.dot(a, b, preferred_element_type=F32)


def _dot_nt(a, b):
    return lax.dot_general(a, b, (((1,), (1,)), ((), ())), preferred_element_type=F32)


def _in_proj_kernel(x_ref, xs_ref, g_ref, w_ref, q_ref, kv_ref, rest_ref, kvt_ref,
                    qs_ref, kvs_ref, rests_ref, xn_ref, *, tm, tn, nq, nkv):
    i, j = pl.program_id(0), pl.program_id(1)

    @pl.when(j == 0)
    def _():
        xn_ref[:tm] = _rms(x_ref[...], g_ref[...]).astype(BF16)

    @pl.when((j == 0) & (i == 0))
    def _():
        xn_ref[tm:] = _rms(xs_ref[...], g_ref[...]).astype(BF16)

    def project(out_ref, outs_ref, col, transposed_ref=None):
        @pl.when(i == 0)
        def _():
            z = _dot(xn_ref[...], w_ref[...].astype(BF16))
            out_ref[...] = z[:tm]
            outs_ref[:, pl.ds(pl.multiple_of(col * tn, tn), tn)] = z[tm:]
            if transposed_ref is not None:
                transposed_ref[...] = z[:tm].T

        @pl.when(i > 0)
        def _():
            z = _dot(xn_ref[:tm], w_ref[...].astype(BF16))
            out_ref[...] = z
            if transposed_ref is not None:
                transposed_ref[...] = z.T

    @pl.when(j < nq)
    def _():
        project(q_ref, qs_ref, j)

    @pl.when((j >= nq) & (j < nq + nkv))
    def _():
        project(kv_ref, kvs_ref, j - nq, kvt_ref)

    @pl.when(j >= nq + nkv)
    def _():
        project(rest_ref, rests_ref, j - nq - nkv)


def _in_proj(x, xs, g, w, layer, *, tm, seq, tn=512):
    m, k = x.shape
    ms = xs.shape[0]
    n = w.shape[2]
    nq, nkv = ATT_WIDTH // tn, 2 * ATT_WIDTH // tn
    nrest = n // tn - nq - nkv
    assert seq % tm == 0 and tm % LANES == 0
    tiles_per_seq = seq // tm
    vmem = (_nbytes((tm, k), F32)
            + 2 * (_nbytes((ms, k), F32) + _nbytes((k, tn), F32) + 4 * _nbytes((tm, tn), F32)
                   + _nbytes((ms, n), F32))
            + _nbytes((tm + ms, k), BF16) + _nbytes((k, tn), BF16) + 3 * _nbytes((tm + ms, tn), F32)
            + 2 * MIB)
    kv_col = lambda j: jnp.clip(j - nq, 0, nkv - 1)
    whole = lambda cols: pl.BlockSpec((ms, cols), lambda i, j: (0, 0))
    return pl.pallas_call(
        functools.partial(_in_proj_kernel, tm=tm, tn=tn, nq=nq, nkv=nkv),
        out_shape=(jax.ShapeDtypeStruct((m, nq * tn), F32),
                   jax.ShapeDtypeStruct((m, nkv * tn), F32),
                   jax.ShapeDtypeStruct((m, nrest * tn), F32),
                   jax.ShapeDtypeStruct((m // seq, nkv * tn, seq), F32),
                   jax.ShapeDtypeStruct((ms, nq * tn), F32),
                   jax.ShapeDtypeStruct((ms, nkv * tn), F32),
                   jax.ShapeDtypeStruct((ms, nrest * tn), F32)),
        grid=(m // tm, n // tn),
        in_specs=[pl.BlockSpec((tm, k), lambda i, j: (i, 0), pipeline_mode=pl.Buffered(1)),
                  whole(k),
                  pl.BlockSpec((1, k), lambda i, j: (0, 0)),
                  pl.BlockSpec((None, k, tn), lambda i, j: (layer, 0, j))],
        out_specs=(pl.BlockSpec((tm, tn), lambda i, j: (i, jnp.minimum(j, nq - 1))),
                   pl.BlockSpec((tm, tn), lambda i, j: (i, kv_col(j))),
                   pl.BlockSpec((tm, tn), lambda i, j: (i, jnp.clip(j - nq - nkv, 0, nrest - 1))),
                   pl.BlockSpec((None, tn, tm),
                                lambda i, j: (i // tiles_per_seq, kv_col(j), i % tiles_per_seq)),
                   whole(nq * tn), whole(nkv * tn), whole(nrest * tn)),
        scratch_shapes=[pltpu.VMEM((tm + ms, k), BF16)],
        compiler_params=_params(("arbitrary", "arbitrary"), vmem),
        name="in_proj",
    )(x, xs, g.reshape(1, k), w)


def _attn_prompt_kernel(q_ref, k_ref, v_ref, o_ref, ob_ref, lb_ref, bias_ref):
    s_len = q_ref.shape[0]
    blk = ATT_SPAN
    lane = lax.broadcasted_iota(jnp.int32, (blk, LANES), 1)
    head0 = lane < HEAD_DIM
    scale = HEAD_DIM ** -0.5

    qi = lax.broadcasted_iota(jnp.int32, (2 * blk, 3 * blk), 0) & (blk - 1)
    kj = lax.broadcasted_iota(jnp.int32, (2 * blk, 3 * blk), 1)
    dist = jnp.where(kj < blk, qi - kj, qi + 2 * blk - kj)
    bias_ref[...] = jnp.where((dist >= 0) & (dist <= ATT_SPAN), 0.0, NEG)

    def block(p, d, q0, k0=None):
        def rows(start, n):
            return pl.ds(start, n, stride=d) if d > 1 else pl.ds(start, n)

        qt = q_ref[rows(q0, blk), :] * scale
        qq = jnp.concatenate([jnp.where(head0, qt, 0.0), jnp.where(head0, 0.0, qt)], axis=0).astype(BF16)
        if k0 is None:
            key_rows, bias = rows(q0, blk), bias_ref[:, :blk]
        else:
            key_rows, bias = rows(k0, 2 * blk), bias_ref[:, blk:]
        k = k_ref[key_rows, :].astype(BF16)
        v = v_ref[key_rows, :].astype(BF16)
        s = _dot_nt(qq, k) + bias
        m = jnp.max(s, axis=-1, keepdims=True)
        e = jnp.exp(s - m)
        l = jnp.sum(e, axis=-1, keepdims=True)
        o = _dot(e.astype(BF16), v) * (1.0 / l)
        lse = jnp.broadcast_to(m + jnp.log(l), (2 * blk, LANES))
        ob_ref[p, rows(q0, blk), :] = jnp.where(head0, o[:blk], o[blk:])
        lb_ref[p, rows(q0, blk), :] = jnp.where(head0, lse[:blk], lse[blk:])

    for p, d in enumerate(ATT_DILATIONS):
        nblk = s_len // (d * blk)
        for r in range(d):
            for b in range(nblk):
                block(p, d, r + d * blk * b, None if b == 0 else r + d * blk * (b - 1))

    def merge(c, carry):
        rows = pl.ds(pl.multiple_of(c * blk, blk), blk)
        l0, l1, l2 = lb_ref[0, rows, :], lb_ref[1, rows, :], lb_ref[2, rows, :]
        m = jnp.maximum(jnp.maximum(l0, l1), l2)
        e0, e1, e2 = jnp.exp(l0 - m), jnp.exp(l1 - m), jnp.exp(l2 - m)
        num = e0 * ob_ref[0, rows, :] + e1 * ob_ref[1, rows, :] + e2 * ob_ref[2, rows, :]
        o_ref[rows, :] = num / (e0 + e1 + e2)
        return carry

    lax.fori_loop(0, s_len // blk, merge, 0)


def _attn_prompt(q, kv):
    b, s, _ = q.shape
    assert s % (max(ATT_DILATIONS) * ATT_SPAN) == 0
    npair = ATT_WIDTH // LANES
    blk_bytes = _nbytes((s, LANES), F32)
    vmem = 2 * 4 * blk_bytes + 6 * blk_bytes + 4 * MIB
    return pl.pallas_call(
        _attn_prompt_kernel,
        out_shape=jax.ShapeDtypeStruct((b, s, ATT_WIDTH), F32),
        grid=(b, npair),
        in_specs=[pl.BlockSpec((None, s, LANES), lambda i, j: (i, 0, j)),
                  pl.BlockSpec((None, s, LANES), lambda i, j: (i, 0, j)),
                  pl.BlockSpec((None, s, LANES), lambda i, j: (i, 0, npair + j))],
        out_specs=pl.BlockSpec((None, s, LANES), lambda i, j: (i, 0, j)),
        scratch_shapes=[pltpu.VMEM((len(ATT_DILATIONS), s, LANES), F32),
                        pltpu.VMEM((len(ATT_DILATIONS), s, LANES), F32),
                        pltpu.VMEM((2 * ATT_SPAN, 3 * ATT_SPAN), F32)],
        compiler_params=_params(("parallel", "parallel"), vmem, _nbytes((b, s, ATT_WIDTH), F32)),
        name="attn_prompt",
    )(q, kv, kv)


def _attn_sample_kernel(q_ref, kn_ref, vn_ref, kct_ref, vct_ref, o_ref):
    t_len = q_ref.shape[0]
    w_len = kct_ref.shape[-1]
    pairs = q_ref.shape[1] // LANES
    pad = LANES
    zpad = jnp.zeros((pad - t_len, LANES), F32)

    def multiplicity(dist):
        c = jnp.zeros(dist.shape, F32)
        for d in ATT_DILATIONS:
            ok = (dist >= 0) & ((dist & (d - 1)) == 0) & (dist <= ATT_SPAN * d)
            c = c + jnp.where(ok, 1.0, 0.0)
        return c

    tc = lax.broadcasted_iota(jnp.int32, (t_len, w_len), 0)
    jc = lax.broadcasted_iota(jnp.int32, (t_len, w_len), 1)
    cnt_c = multiplicity(w_len + tc - jc)
    tn = lax.broadcasted_iota(jnp.int32, (t_len, pad), 0)
    jn = lax.broadcasted_iota(jnp.int32, (t_len, pad), 1)
    cnt_n = multiplicity(jnp.where(jn < t_len, tn - jn, -1))

    lane = lax.broadcasted_iota(jnp.int32, (t_len, LANES), 1)
    head0 = lane < HEAD_DIM
    for p in range(pairs):
        lanes = slice(p * LANES, (p + 1) * LANES)
        q = q_ref[:, lanes] * (HEAD_DIM ** -0.5)
        kn = jnp.concatenate([kn_ref[:, lanes], zpad], axis=0).astype(BF16)
        vn = jnp.concatenate([vn_ref[:, lanes], zpad], axis=0).astype(BF16)
        kct = kct_ref[2 * p:2 * p + 2].reshape(LANES, w_len).astype(BF16)
        vct = vct_ref[2 * p:2 * p + 2].reshape(LANES, w_len).astype(BF16)
        outs = []
        for h in range(2):
            hm = head0 if h == 0 else jnp.logical_not(head0)
            qh = jnp.where(hm, q, 0.0).astype(BF16)
            sc = jnp.where(cnt_c > 0, _dot(qh, kct), NEG)
            sn = jnp.where(cnt_n > 0, _dot_nt(qh, kn), NEG)
            m = jnp.maximum(jnp.max(sc, axis=-1, keepdims=True), jnp.max(sn, axis=-1, keepdims=True))
            ec = cnt_c * jnp.exp(sc - m)
            en = cnt_n * jnp.exp(sn - m)
            l = jnp.sum(ec, axis=-1, keepdims=True) + jnp.sum(en, axis=-1, keepdims=True)
            pv = _dot_nt(ec.astype(BF16), vct) + _dot(en.astype(BF16), vn)
            outs.append(pv / l)
        o_ref[:, lanes] = jnp.where(head0, outs[0], outs[1])


def _attn_sample(q, kv, cache_t, layer, *, pairs=4):
    b, t, _ = q.shape
    w = cache_t.shape[-1]
    assert all(d & (d - 1) == 0 for d in ATT_DILATIONS) and t <= LANES
    width = pairs * LANES
    nstep = ATT_WIDTH // width
    vmem = 2 * 2 * _nbytes((width, w), F32) + 10 * _nbytes((LANES, w), F32) * pairs + 4 * MIB
    new_spec = lambda off: pl.BlockSpec((None, t, width), lambda i, j: (i, 0, off + j))
    buf_spec = lambda c: pl.BlockSpec((None, None, None, 2 * pairs, HEAD_DIM, w),
                                      lambda i, j: (layer, i, c, j, 0, 0))
    return pl.pallas_call(
        _attn_sample_kernel,
        out_shape=jax.ShapeDtypeStruct((b, t, ATT_WIDTH), F32),
        grid=(b, nstep),
        in_specs=[new_spec(0), new_spec(0), new_spec(nstep), buf_spec(0), buf_spec(1)],
        out_specs=new_spec(0),
        compiler_params=_params(("parallel", "parallel"), vmem),
        name="attn_sample",
    )(q, kv, kv, cache_t, cache_t)


def _ssm_prep_kernel(are_ref, aim_ref, ldt_ref, bre_ref, bim_ref, abre_ref, abim_ref, bbre_ref, bbim_ref):
    a_re, a_im = are_ref[...], aim_ref[...]
    dt = jnp.exp(ldt_ref[...])
    mag = jnp.exp(a_re * dt)
    ang = a_im * dt
    ab_re, ab_im = mag * jnp.cos(ang), mag * jnp.sin(ang)
    den = a_re * a_re + a_im * a_im
    nr, ni = ab_re - 1.0, ab_im
    cf_re = (nr * a_re + ni * a_im) / den
    cf_im = (ni * a_re - nr * a_im) / den
    abre_ref[...] = ab_re
    abim_ref[...] = ab_im
    for c in range(SSM_GROUP):
        b_re, b_im = bre_ref[c], bim_ref[c]
        bbre_ref[c] = cf_re * b_re - cf_im * b_im
        bbim_ref[c] = cf_re * b_im + cf_im * b_re


def _ssm_prep(a_re, a_im, log_dt, b_re, b_im):
    depth, g, n = a_re.shape
    rows = depth * g
    flat = lambda a: a.reshape(rows, n)
    chan_major = lambda b: jnp.transpose(b.reshape(rows, n, SSM_GROUP), (2, 0, 1))
    ab_re, ab_im, bb_re, bb_im = pl.pallas_call(
        _ssm_prep_kernel,
        out_shape=(jax.ShapeDtypeStruct((rows, n), F32), jax.ShapeDtypeStruct((rows, n), F32),
                   jax.ShapeDtypeStruct((SSM_GROUP, rows, n), F32),
                   jax.ShapeDtypeStruct((SSM_GROUP, rows, n), F32)),
        name="ssm_prep",
    )(flat(a_re), flat(a_im), log_dt.reshape(rows, 1), chan_major(b_re), chan_major(b_im))
    unflat = lambda b: jnp.transpose(b, (1, 0, 2)).reshape(depth, g, SSM_GROUP, n)
    return ab_re.reshape(depth, g, n), ab_im.reshape(depth, g, n), unflat(bb_re), unflat(bb_im)


def _block_diag(m):
    g, r, c = m.shape
    nb = g // SSM_BLOCK_GROUPS
    m = m.reshape(nb, SSM_BLOCK_GROUPS, r, c)
    eye = jnp.eye(SSM_BLOCK_GROUPS, dtype=m.dtype)
    out = m[:, :, :, None, :] * eye[None, :, None, :, None]
    return out.reshape(nb, SSM_BLOCK_GROUPS * r, SSM_BLOCK_GROUPS * c)


def _ssm_kernel(u_ref, h0re_ref, h0im_ref, are_ref, aim_ref, bb_ref, cre_ref, cim_ref, d_ref,
                y_ref, hre_ref, him_ref, x_scr, up_scr, *, steps, chained):
    ns = SSM_BLOCK_STATE
    nrows = steps * SCAN_SEQS
    chunk = min(nrows, 256)

    nchunks = nrows // chunk
    chunk_steps = chunk // SCAN_SEQS
    unroll = min(steps, 8)

    def permute_in(t, c):
        up_scr[pl.ds(pl.multiple_of(t * SCAN_SEQS, SCAN_SEQS), SCAN_SEQS), :] = (
            u_ref[pl.ds(t, SCAN_SEQS, stride=steps), :])
        return c
    lax.fori_loop(0, steps, permute_in, 0, unroll=unroll)

    a_re = jnp.broadcast_to(are_ref[...], (SCAN_SEQS, ns))
    a_im = jnp.broadcast_to(aim_ref[...], (SCAN_SEQS, ns))

    def input_map(c):
        rows = pl.ds(c * chunk, chunk)
        x_scr[rows, :] = _dot(up_scr[rows, :].astype(BF16), bb_ref[...])

    def scan_chunk(c, h, store):
        h_re, h_im = h
        for t in range(c * chunk_steps, (c + 1) * chunk_steps):
            rows = pl.ds(t * SCAN_SEQS, SCAN_SEQS)
            h_re, h_im = (a_re * h_re - a_im * h_im + x_scr[rows, :ns],
                          a_re * h_im + a_im * h_re + x_scr[rows, ns:])
            if store:
                x_scr[rows, :ns] = h_re
                x_scr[rows, ns:] = h_im
        return h_re, h_im

    def output_map(c):
        rows = pl.ds(c * chunk, chunk)
        y = (_dot(x_scr[rows, :ns].astype(BF16), cre_ref[...])
             - _dot(x_scr[rows, ns:].astype(BF16), cim_ref[...])
             + d_ref[...] * up_scr[rows, :])
        up_scr[rows, :] = _gelu_tanh(y)

    init = (h0re_ref[...], h0im_ref[...])
    input_map(0)
    if chained:
        assert steps & (steps - 1) == 0
        e_re, e_im = init
        for c in range(nchunks):
            if c + 1 < nchunks:
                input_map(c + 1)
            e_re, e_im = scan_chunk(c, (e_re, e_im), store=False)
        p_re, p_im = a_re, a_im
        for _ in range(steps.bit_length() - 1):
            p_re, p_im = p_re * p_re - p_im * p_im, 2.0 * p_re * p_im
        sub = lax.broadcasted_iota(jnp.int32, (SCAN_SEQS, ns), 0)
        shift = 1
        while shift < SCAN_SEQS:
            r_re = pltpu.roll(e_re, shift, axis=0)
            r_im = pltpu.roll(e_im, shift, axis=0)
            keep = sub >= shift
            e_re, e_im = (e_re + jnp.where(keep, p_re * r_re - p_im * r_im, 0.0),
                          e_im + jnp.where(keep, p_re * r_im + p_im * r_re, 0.0))
            p_re, p_im = p_re * p_re - p_im * p_im, 2.0 * p_re * p_im
            shift *= 2
        first = sub == 0
        init = (init[0] + jnp.where(first, 0.0, pltpu.roll(e_re, 1, axis=0)),
                init[1] + jnp.where(first, 0.0, pltpu.roll(e_im, 1, axis=0)))
    else:
        for c in range(1, nchunks):
            input_map(c)
    h = init
    for c in range(nchunks):
        h = scan_chunk(c, h, store=True)
        if c > 0:
            output_map(c - 1)
    output_map(nchunks - 1)
    hre_ref[...] = h[0]
    him_ref[...] = h[1]

    def permute_out(t, c):
        y_ref[pl.ds(t, SCAN_SEQS, stride=steps), :] = (
            up_scr[pl.ds(pl.multiple_of(t * SCAN_SEQS, SCAN_SEQS), SCAN_SEQS), :])
        return c
    lax.fori_loop(0, steps, permute_out, 0, unroll=unroll)


def _ssm(u, h0_re, h0_im, ab_re, ab_im, bb, c_re, c_im, d_skip, *, chained):
    nb, rows, _ = u.shape
    steps = rows // SCAN_SEQS
    nblk = SSM_WIDTH // LANES
    ns = SSM_BLOCK_STATE
    vmem = (2 * 2 * _nbytes((rows, LANES), F32) + _nbytes((rows, 2 * ns), F32)
            + _nbytes((rows, LANES), F32) + 4 * _nbytes((min(rows, 256), 2 * ns), F32) + 8 * MIB)
    state_spec = pl.BlockSpec((None, SCAN_SEQS, ns), lambda i, j: (i, 0, j))
    return pl.pallas_call(
        functools.partial(_ssm_kernel, steps=steps, chained=chained),
        out_shape=(jax.ShapeDtypeStruct((nb, rows, SSM_WIDTH), F32),
                   jax.ShapeDtypeStruct(h0_re.shape, F32),
                   jax.ShapeDtypeStruct(h0_im.shape, F32)),
        grid=(nb, nblk),
        in_specs=[pl.BlockSpec((None, rows, LANES), lambda i, j: (i, 0, j)),
                  state_spec, state_spec,
                  pl.BlockSpec((1, ns), lambda i, j: (0, j)),
                  pl.BlockSpec((1, ns), lambda i, j: (0, j)),
                  pl.BlockSpec((None, LANES, 2 * ns), lambda i, j: (j, 0, 0)),
                  pl.BlockSpec((None, ns, LANES), lambda i, j: (j, 0, 0)),
                  pl.BlockSpec((None, ns, LANES), lambda i, j: (j, 0, 0)),
                  pl.BlockSpec((1, LANES), lambda i, j: (0, j))],
        out_specs=(pl.BlockSpec((None, rows, LANES), lambda i, j: (i, 0, j)), state_spec, state_spec),
        scratch_shapes=[pltpu.VMEM((rows, 2 * ns), F32), pltpu.VMEM((rows, LANES), F32)],
        compiler_params=_params(("parallel", "parallel"), vmem,
                                _nbytes((nb, rows, SSM_WIDTH), F32)),
        name="ssm_chained" if chained else "ssm_step",
    )(u, h0_re, h0_im, ab_re, ab_im, bb, c_re, c_im, d_skip)


def _conv_kernel(h_ref, gb_ref, gc_ref, w_ref, pre_ref, y_ref, st_ref):
    n = h_ref.shape[0]
    u = gc_ref[...] * h_ref[...]
    row = lax.broadcasted_iota(jnp.int32, u.shape, 0)
    pre = pre_ref[...]
    p0, p1 = pre[0:1, :], pre[1:2, :]
    u1 = jnp.where(row == 0, p1, pltpu.roll(u, 1, axis=0))
    u2 = jnp.where(row == 0, p0, jnp.where(row == 1, p1, pltpu.roll(u, 2, axis=0)))
    w = w_ref[...]
    y = w[0:1, :] * u2 + w[1:2, :] * u1 + w[2:3, :] * u
    y_ref[...] = gb_ref[...] * y
    st_ref[...] = u[n - (CONV_K - 1):, :]


def _conv(rest, w, prefix):
    b, n, _ = rest.shape
    assert n >= CONV_K - 1
    nblk = CONV_CH // LANES
    off = SSM_WIDTH // LANES
    col = lambda part: pl.BlockSpec((None, n, LANES), lambda i, j: (i, 0, off + part * nblk + j))
    vmem = 2 * 4 * _nbytes((n, LANES), F32) + 8 * _nbytes((n, LANES), F32) + 4 * MIB
    return pl.pallas_call(
        _conv_kernel,
        out_shape=(jax.ShapeDtypeStruct((b, n, CONV_CH), F32),
                   jax.ShapeDtypeStruct((b, CONV_K - 1, CONV_CH), F32)),
        grid=(b, nblk),
        in_specs=[col(0), col(1), col(2),
                  pl.BlockSpec((CONV_K, LANES), lambda i, j: (0, j)),
                  pl.BlockSpec((None, CONV_K - 1, LANES), lambda i, j: (i, 0, j))],
        out_specs=(pl.BlockSpec((None, n, LANES), lambda i, j: (i, 0, j)),
                   pl.BlockSpec((None, CONV_K - 1, LANES), lambda i, j: (i, 0, j))),
        compiler_params=_params(("parallel", "parallel"), vmem,
                                _nbytes((b, n, CONV_CH), F32)),
        name="conv",
    )(rest, rest, rest, w, prefix)


def _out_proj_kernel(oatt_ref, yssm_ref, yconv_ref, x_ref, oatts_ref, yssms_ref, yconvs_ref, xs_ref,
                     g_ref, wglu_ref, bglu_ref, w_ref, o_ref, os_ref, mix_ref, *, tm, tn):
    i, j = pl.program_id(0), pl.program_id(1)
    a0, a1 = ATT_WIDTH, ATT_WIDTH + SSM_WIDTH

    def mixed(rows, o_att, y_ssm, y_conv):
        g = g_ref[...]
        mix_ref[rows, :a0] = _rms(o_att, g[:, :a0]).astype(BF16)
        y_ssm = y_ssm * _sigmoid(_dot(y_ssm.astype(BF16), wglu_ref[...].astype(BF16)) + bglu_ref[...])
        mix_ref[rows, a0:a1] = _rms(y_ssm, g[:, a0:a1]).astype(BF16)
        mix_ref[rows, a1:] = _rms(y_conv, g[:, a1:]).astype(BF16)

    @pl.when(j == 0)
    def _():
        mixed(slice(0, tm), oatt_ref[...], yssm_ref[...], yconv_ref[...])

    @pl.when((j == 0) & (i == 0))
    def _():
        mixed(slice(tm, None), oatts_ref[...], yssms_ref[...], yconvs_ref[...])

    @pl.when(i == 0)
    def _():
        z = _dot(mix_ref[...], w_ref[...].astype(BF16))
        o_ref[...] = x_ref[...] + z[:tm]
        cols = pl.ds(pl.multiple_of(j * tn, tn), tn)
        os_ref[:, cols] = xs_ref[:, cols] + z[tm:]

    @pl.when(i > 0)
    def _():
        o_ref[...] = x_ref[...] + _dot(mix_ref[:tm], w_ref[...].astype(BF16))


def _out_proj(prompt, sample, g, w_glu, b_glu, w, layer, *, tm, tn=512):
    m, d = prompt[3].shape
    ms = sample[3].shape[0]
    k = w.shape[1]
    vmem = (2 * (_nbytes((tm, k), F32) + 2 * _nbytes((tm, tn), F32) + 3 * _nbytes((ms, k), F32)
                 + _nbytes((k, tn), F32) + _nbytes((SSM_WIDTH, SSM_WIDTH), F32))
            + _nbytes((tm + ms, k), BF16) + _nbytes((k, tn), BF16) + 2 * _nbytes((tm + ms, k), F32)
            + 2 * MIB)
    row = lambda c: pl.BlockSpec((tm, c), lambda i, j: (i, 0))
    whole = lambda c: pl.BlockSpec((ms, c), lambda i, j: (0, 0))
    const = lambda r, c: pl.BlockSpec((r, c), lambda i, j: (0, 0))
    return pl.pallas_call(
        functools.partial(_out_proj_kernel, tm=tm, tn=tn),
        out_shape=(jax.ShapeDtypeStruct((m, d), F32), jax.ShapeDtypeStruct((ms, d), F32)),
        grid=(m // tm, d // tn),
        in_specs=[row(ATT_WIDTH), row(SSM_WIDTH), row(CONV_CH),
                  pl.BlockSpec((tm, tn), lambda i, j: (i, j)),
                  whole(ATT_WIDTH), whole(SSM_WIDTH), whole(CONV_CH), whole(d),
                  const(1, k),
                  pl.BlockSpec((None, SSM_WIDTH, SSM_WIDTH), lambda i, j: (layer, 0, 0)),
                  const(1, SSM_WIDTH),
                  pl.BlockSpec((None, k, tn), lambda i, j: (layer, 0, j))],
        out_specs=(pl.BlockSpec((tm, tn), lambda i, j: (i, j)), whole(d)),
        scratch_shapes=[pltpu.VMEM((tm + ms, k), BF16)],
        compiler_params=_params(("arbitrary", "arbitrary"), vmem),
        name="out_proj",
    )(*prompt, *sample, g.reshape(1, k), w_glu, b_glu.reshape(1, SSM_WIDTH), w)


def _ffn_kernel(x_ref, xs_ref, g_ref, gfin_ref, wg_ref, wu_ref, wd_ref, o_ref, os_ref, xn_ref, *,
                tm, final):
    i, f = pl.program_id(0), pl.program_id(1)
    last = f == pl.num_programs(1) - 1

    @pl.when(f == 0)
    def _():
        x = x_ref[...]
        xn_ref[:tm] = _rms(x, g_ref[...]).astype(BF16)
        o_ref[...] = x

    @pl.when((f == 0) & (i == 0))
    def _():
        xs = xs_ref[...]
        xn_ref[tm:] = _rms(xs, g_ref[...]).astype(BF16)
        os_ref[...] = xs

    def swiglu(xn):
        gate = _dot(xn, wg_ref[...].astype(BF16))
        up = _dot(xn, wu_ref[...].astype(BF16))
        return _dot((gate * _sigmoid(gate) * up).astype(BF16), wd_ref[...].astype(BF16))

    @pl.when(i == 0)
    def _():
        y = swiglu(xn_ref[...])
        o_ref[...] += y[:tm]
        os_ref[...] += y[tm:]

    @pl.when(i > 0)
    def _():
        o_ref[...] += swiglu(xn_ref[:tm])

    if final:
        @pl.when(last)
        def _():
            o_ref[...] = _rms(o_ref[...], gfin_ref[...])

        @pl.when(last & (i == 0))
        def _():
            os_ref[...] = _rms(os_ref[...], gfin_ref[...])


def _ffn(x, xs, g, w_gate, w_up, w_down, layer, *, tm, g_final=None, tf=256):
    final = g_final is not None
    if not final:
        g_final = g
    m, d = x.shape
    ms = xs.shape[0]
    ff = w_gate.shape[2]
    vmem = (2 * (2 * _nbytes((tm, d), F32) + 2 * _nbytes((ms, d), F32) + 3 * _nbytes((d, tf), F32))
            + _nbytes((tm + ms, d), BF16) + 3 * _nbytes((d, tf), BF16) + 4 * _nbytes((tm + ms, tf), F32)
            + 2 * MIB)
    return pl.pallas_call(
        functools.partial(_ffn_kernel, tm=tm, final=final),
        out_shape=(jax.ShapeDtypeStruct((m, d), F32), jax.ShapeDtypeStruct((ms, d), F32)),
        grid=(m // tm, ff // tf),
        in_specs=[pl.BlockSpec((tm, d), lambda i, f: (i, 0)),
                  pl.BlockSpec((ms, d), lambda i, f: (0, 0)),
                  pl.BlockSpec((1, d), lambda i, f: (0, 0)),
                  pl.BlockSpec((1, d), lambda i, f: (0, 0)),
                  pl.BlockSpec((None, d, tf), lambda i, f: (layer, 0, f)),
                  pl.BlockSpec((None, d, tf), lambda i, f: (layer, 0, f)),
                  pl.BlockSpec((None, tf, d), lambda i, f: (layer, f, 0))],
        out_specs=(pl.BlockSpec((tm, d), lambda i, f: (i, 0)),
                   pl.BlockSpec((ms, d), lambda i, f: (0, 0))),
        scratch_shapes=[pltpu.VMEM((tm + ms, d), BF16)],
        compiler_params=_params(("arbitrary", "arbitrary"), vmem),
        name="ffn",
    )(x, xs, g.reshape(1, d), g_final.reshape(1, d), w_gate, w_up, w_down)


def _mixers(q, kv, rest, lw, layer, *, batch, cache_t, conv_prefix, ssm_h0):
    m = q.shape[0]
    n = m // batch
    q3, kv3, rest3 = (a.reshape(batch, n, a.shape[-1]) for a in (q, kv, rest))
    ssm_w = (lw['ab_re'], lw['ab_im'], lw['bb'], lw['c_re'], lw['c_im'], lw['ssm_d'])
    if cache_t is None:
        o_att = _attn_prompt(q3, kv3)
        y_ssm, h_re, h_im = _ssm(rest3, *ssm_h0, *ssm_w, chained=True)
        h_re, h_im = h_re[:, SCAN_SEQS - 1], h_im[:, SCAN_SEQS - 1]
    else:
        o_att = _attn_sample(q3, kv3, cache_t, layer)
        y_ssm, h_re, h_im = _ssm(rest.reshape(1, m, rest.shape[-1]), *ssm_h0, *ssm_w, chained=False)
        h_re, h_im = h_re[0], h_im[0]
    y_conv, conv_state = _conv(rest3, lw['conv_w'], conv_prefix)
    mixed = (o_att.reshape(m, ATT_WIDTH), y_ssm.reshape(m, SSM_WIDTH), y_conv.reshape(m, CONV_CH))
    return mixed, conv_state, h_re, h_im


def kernel(x_prompt, x_sample, cache_kv, state_conv, state_ssm_re, state_ssm_im, g_mix_norm, w_in,
           ssm_a_re, ssm_a_im, ssm_log_dt, ssm_b_re, ssm_b_im, ssm_c_re, ssm_c_im, ssm_d, ssm_w_glu,
           ssm_b_glu, conv_w, g_out, w_out, g_ffn_norm, w_gate, w_up, w_down, g_final):
    batch, seq, d_model = x_prompt.shape
    dec_batch, dec_seq, _ = x_sample.shape
    depth, n_groups, n_state = ssm_a_re.shape
    n_heads = ATT_WIDTH // HEAD_DIM
    assert dec_batch == SCAN_SEQS and seq % SCAN_SEQS == 0

    ab_re, ab_im, bb_re, bb_im = _ssm_prep(ssm_a_re, ssm_a_im, ssm_log_dt, ssm_b_re, ssm_b_im)
    cache_t = jnp.transpose(cache_kv, (0, 1, 3, 4, 5, 2))

    xp = x_prompt.reshape(batch * seq, d_model)
    xs = x_sample.reshape(dec_batch * dec_seq, d_model)
    zero_state = jnp.zeros((batch, SCAN_SEQS, n_groups * n_state), F32)
    zero_prefix = jnp.zeros((batch, CONV_K - 1, CONV_CH), F32)

    outs_p, outs_s = [], []
    for i in range(depth):
        bb = jnp.concatenate([_block_diag(bb_re[i]), _block_diag(bb_im[i])], axis=-1).astype(BF16)
        c_t = lambda c: _block_diag(jnp.transpose(c[i], (0, 2, 1))).astype(BF16)
        lw = {
            'ab_re': ab_re[i].reshape(1, -1), 'ab_im': ab_im[i].reshape(1, -1), 'bb': bb,
            'c_re': c_t(ssm_c_re), 'c_im': c_t(ssm_c_im), 'ssm_d': ssm_d[i].reshape(1, -1),
            'conv_w': conv_w[i],
        }
        q, kv, rest, kvt, q_s, kv_s, rest_s = _in_proj(xp, xs, g_mix_norm[i], w_in, i, tm=2048, seq=seq,
                                                       tn=256)

        mixed_p, c_state, h_re, h_im = _mixers(
            q, kv, rest, lw, i, batch=batch, cache_t=None, conv_prefix=zero_prefix,
            ssm_h0=(zero_state, zero_state))
        kv_p = jnp.transpose(kvt.reshape(batch, 2, n_heads, HEAD_DIM, seq), (0, 4, 1, 2, 3))
        n_keep = min(max(ATT_DILATIONS) * ATT_SPAN, seq)
        outs_p.append((kv_p[:, seq - n_keep:], c_state,
                       h_re.reshape(batch, n_groups, n_state), h_im.reshape(batch, n_groups, n_state)))

        h0 = (state_ssm_re[i].reshape(1, dec_batch, -1), state_ssm_im[i].reshape(1, dec_batch, -1))
        mixed_s, c_state, h_re, h_im = _mixers(
            q_s, kv_s, rest_s, lw, i, batch=dec_batch, cache_t=cache_t, conv_prefix=state_conv[i],
            ssm_h0=h0)
        outs_s.append((kv_s.reshape(dec_batch, dec_seq, 2, n_heads, HEAD_DIM), c_state,
                       h_re.reshape(dec_batch, n_groups, n_state),
                       h_im.reshape(dec_batch, n_groups, n_state)))

        xp, xs = _out_proj((*mixed_p, xp), (*mixed_s, xs), g_out[i], ssm_w_glu, ssm_b_glu[i], w_out, i,
                           tm=1024)
        xp, xs = _ffn(xp, xs, g_ffn_norm[i], w_gate, w_up, w_down, i, tm=1024,
                      g_final=g_final if i == depth - 1 else None)

    y_prompt = xp.reshape(batch, seq, d_model)
    y_sample = xs.reshape(dec_batch, dec_seq, d_model)
    stack = lambda outs, k: jnp.stack([o[k] for o in outs])
    return (y_prompt, y_sample,
            stack(outs_p, 0), stack(outs_p, 1), stack(outs_p, 2), stack(outs_p, 3),
            stack(outs_s, 0), stack(outs_s, 1), stack(outs_s, 2), stack(outs_s, 3))
```

```python
import functools
import math

import jax
import jax.numpy as jnp
from jax import lax
from jax.experimental import pallas as pl
from jax.experimental.pallas import tpu as pltpu

F32 = jnp.float32
BF16 = jnp.bfloat16

HEAD_DIM = 64
ATT_WIDTH = 1024
SSM_WIDTH = 512
CONV_CH = 512
CONV_K = 3
SSM_GROUP = 16
SSM_STATE = 64
ATT_DILATIONS = (1, 4, 16)
ATT_SPAN = 128
RMS_EPS = 1e-6
NEG = -1e30

LANES = 128
SUBLANES = 8
SCAN_SEQS = SUBLANES
SSM_BLOCK_GROUPS = LANES // SSM_GROUP
SSM_BLOCK_STATE = SSM_BLOCK_GROUPS * SSM_STATE

MIB = 1024 * 1024
V7X_VMEM_BYTES = 64 * MIB


def _params(semantics, vmem_bytes, hbm_result_bytes=None):
    if hbm_result_bytes is not None and hbm_result_bytes >= V7X_VMEM_BYTES // 8:
        vmem_bytes = max(vmem_bytes, V7X_VMEM_BYTES - hbm_result_bytes + MIB)
    return pltpu.CompilerParams(dimension_semantics=semantics, vmem_limit_bytes=int(vmem_bytes))


def _nbytes(shape, dtype):
    return math.prod(shape) * jnp.dtype(dtype).itemsize


def _rms(x, g):
    ms = jnp.mean(x * x, axis=-1, keepdims=True)
    return x * lax.rsqrt(ms + RMS_EPS) * g


def _sigmoid(x):
    return 1.0 / (1.0 + jnp.exp(-x))


def _gelu_tanh(x):
    c = math.sqrt(2.0 / math.pi)
    return 0.5 * x * (1.0 + jnp.tanh(c * (x + 0.044715 * (x * x * x))))


def _dot(a, b):
    return jnp.dot(a, b, preferred_element_type=F32)


def _dot_nt(a, b):
    return lax.dot_general(a, b, (((1,), (1,)), ((), ())), preferred_element_type=F32)


def _in_proj_kernel(x_ref, xs_ref, g_ref, w_ref, kvt_in_ref, q_ref, kv_ref, rest_ref, kvt_ref,
                    qs_ref, kvs_ref, rests_ref, xn_ref, *, tm, tn, nq, nkv):
    del kvt_in_ref
    i, j = pl.program_id(0), pl.program_id(1)

    @pl.when(j == 0)
    def _():
        xn_ref[:tm] = _rms(x_ref[...], g_ref[...]).astype(BF16)

    @pl.when((j == 0) & (i == 0))
    def _():
        xn_ref[tm:] = _rms(xs_ref[...], g_ref[...]).astype(BF16)

    def project(out_ref, outs_ref, col, transposed_ref=None):
        @pl.when(i == 0)
        def _():
            z = _dot(xn_ref[...], w_ref[...].astype(BF16))
            out_ref[...] = z[:tm]
            outs_ref[:, pl.ds(pl.multiple_of(col * tn, tn), tn)] = z[tm:]
            if transposed_ref is not None:
                transposed_ref[...] = z[:tm].T

        @pl.when(i > 0)
        def _():
            z = _dot(xn_ref[:tm], w_ref[...].astype(BF16))
            out_ref[...] = z
            if transposed_ref is not None:
                transposed_ref[...] = z.T

    @pl.when(j < nq)
    def _():
        project(q_ref, qs_ref, j)

    @pl.when((j >= nq) & (j < nq + nkv))
    def _():
        project(kv_ref, kvs_ref, j - nq, kvt_ref)

    @pl.when(j >= nq + nkv)
    def _():
        project(rest_ref, rests_ref, j - nq - nkv)


def _in_proj(x, xs, g, w, kvt_stack, layer, *, tm, seq, tn=512):
    m, k = x.shape
    ms = xs.shape[0]
    n = w.shape[2]
    nq, nkv = ATT_WIDTH // tn, 2 * ATT_WIDTH // tn
    nrest = n // tn - nq - nkv
    assert seq % tm == 0 and tm % LANES == 0
    assert kvt_stack.shape[1:] == (m // seq, nkv * tn, seq)
    tiles_per_seq = seq // tm
    vmem = (_nbytes((tm, k), F32)
            + 2 * (_nbytes((ms, k), F32) + _nbytes((k, tn), F32) + 4 * _nbytes((tm, tn), F32)
                   + _nbytes((ms, n), F32))
            + _nbytes((tm + ms, k), BF16) + _nbytes((k, tn), BF16) + 3 * _nbytes((tm + ms, tn), F32)
            + 2 * MIB)
    kv_col = lambda j: jnp.clip(j - nq, 0, nkv - 1)
    whole = lambda cols: pl.BlockSpec((ms, cols), lambda i, j: (0, 0))
    return pl.pallas_call(
        functools.partial(_in_proj_kernel, tm=tm, tn=tn, nq=nq, nkv=nkv),
        out_shape=(jax.ShapeDtypeStruct((m, nq * tn), F32),
                   jax.ShapeDtypeStruct((m, nkv * tn), F32),
                   jax.ShapeDtypeStruct((m, nrest * tn), F32),
                   jax.ShapeDtypeStruct(kvt_stack.shape, F32),
                   jax.ShapeDtypeStruct((ms, nq * tn), F32),
                   jax.ShapeDtypeStruct((ms, nkv * tn), F32),
                   jax.ShapeDtypeStruct((ms, nrest * tn), F32)),
        grid=(m // tm, n // tn),
        in_specs=[pl.BlockSpec((tm, k), lambda i, j: (i, 0), pipeline_mode=pl.Buffered(1)),
                  whole(k),
                  pl.BlockSpec((None, 1, k), lambda i, j: (layer, 0, 0)),
                  pl.BlockSpec((None, k, tn), lambda i, j: (layer, 0, j)),
                  pl.BlockSpec(memory_space=pl.ANY)],
        out_specs=(pl.BlockSpec((tm, tn), lambda i, j: (i, jnp.minimum(j, nq - 1))),
                   pl.BlockSpec((tm, tn), lambda i, j: (i, kv_col(j))),
                   pl.BlockSpec((tm, tn), lambda i, j: (i, jnp.clip(j - nq - nkv, 0, nrest - 1))),
                   pl.BlockSpec((None, None, tn, tm),
                                lambda i, j: (layer, i // tiles_per_seq, kv_col(j), i % tiles_per_seq)),
                   whole(nq * tn), whole(nkv * tn), whole(nrest * tn)),
        scratch_shapes=[pltpu.VMEM((tm + ms, k), BF16)],
        input_output_aliases={4: 3},
        compiler_params=_params(("arbitrary", "arbitrary"), vmem),
        name="in_proj",
    )(x, xs, g, w, kvt_stack)


def _attn_prompt_kernel(q_ref, k_ref, v_ref, o_ref, ob_ref, lb_ref, bias_ref):
    s_len = q_ref.shape[0]
    blk = ATT_SPAN
    lane = lax.broadcasted_iota(jnp.int32, (blk, LANES), 1)
    head0 = lane < HEAD_DIM
    scale = HEAD_DIM ** -0.5

    qi = lax.broadcasted_iota(jnp.int32, (2 * blk, 3 * blk), 0) & (blk - 1)
    kj = lax.broadcasted_iota(jnp.int32, (2 * blk, 3 * blk), 1)
    dist = jnp.where(kj < blk, qi - kj, qi + 2 * blk - kj)
    bias_ref[...] = jnp.where((dist >= 0) & (dist <= ATT_SPAN), 0.0, NEG)

    def block(p, d, q0, k0=None):
        def rows(start, n):
            return pl.ds(start, n, stride=d) if d > 1 else pl.ds(start, n)

        qt = q_ref[rows(q0, blk), :] * scale
        qq = jnp.concatenate([jnp.where(head0, qt, 0.0), jnp.where(head0, 0.0, qt)], axis=0).astype(BF16)
        if k0 is None:
            key_rows, bias = rows(q0, blk), bias_ref[:, :blk]
        else:
            key_rows, bias = rows(k0, 2 * blk), bias_ref[:, blk:]
        k = k_ref[key_rows, :].astype(BF16)
        v = v_ref[key_rows, :].astype(BF16)
        s = _dot_nt(qq, k) + bias
        m = jnp.max(s, axis=-1, keepdims=True)
        e = jnp.exp(s - m)
        l = jnp.sum(e, axis=-1, keepdims=True)
        o = _dot(e.astype(BF16), v) * (1.0 / l)
        lse = jnp.broadcast_to(m + jnp.log(l), (2 * blk, LANES))
        ob_ref[p, rows(q0, blk), :] = jnp.where(head0, o[:blk], o[blk:])
        lb_ref[p, rows(q0, blk), :] = jnp.where(head0, lse[:blk], lse[blk:])

    for p, d in enumerate(ATT_DILATIONS):
        nblk = s_len // (d * blk)
        for r in range(d):
            for b in range(nblk):
                block(p, d, r + d * blk * b, None if b == 0 else r + d * blk * (b - 1))

    def merge(c, carry):
        rows = pl.ds(pl.multiple_of(c * blk, blk), blk)
        l0, l1, l2 = lb_ref[0, rows, :], lb_ref[1, rows, :], lb_ref[2, rows, :]
        m = jnp.maximum(jnp.maximum(l0, l1), l2)
        e0, e1, e2 = jnp.exp(l0 - m), jnp.exp(l1 - m), jnp.exp(l2 - m)
        num = e0 * ob_ref[0, rows, :] + e1 * ob_ref[1, rows, :] + e2 * ob_ref[2, rows, :]
        o_ref[rows, :] = num / (e0 + e1 + e2)
        return carry

    lax.fori_loop(0, s_len // blk, merge, 0)


def _attn_prompt(q, kv):
    b, s, _ = q.shape
    assert s % (max(ATT_DILATIONS) * ATT_SPAN) == 0
    npair = ATT_WIDTH // LANES
    blk_bytes = _nbytes((s, LANES), F32)
    vmem = 2 * 4 * blk_bytes + 6 * blk_bytes + 4 * MIB
    return pl.pallas_call(
        _attn_prompt_kernel,
        out_shape=jax.ShapeDtypeStruct((b, s, ATT_WIDTH), F32),
        grid=(b, npair),
        in_specs=[pl.BlockSpec((None, s, LANES), lambda i, j: (i, 0, j)),
                  pl.BlockSpec((None, s, LANES), lambda i, j: (i, 0, j)),
                  pl.BlockSpec((None, s, LANES), lambda i, j: (i, 0, npair + j))],
        out_specs=pl.BlockSpec((None, s, LANES), lambda i, j: (i, 0, j)),
        scratch_shapes=[pltpu.VMEM((len(ATT_DILATIONS), s, LANES), F32),
                        pltpu.VMEM((len(ATT_DILATIONS), s, LANES), F32),
                        pltpu.VMEM((2 * ATT_SPAN, 3 * ATT_SPAN), F32)],
        compiler_params=_params(("parallel", "parallel"), vmem, _nbytes((b, s, ATT_WIDTH), F32)),
        name="attn_prompt",
    )(q, kv, kv)


def _attn_sample_kernel(q_ref, kn_ref, vn_ref, kct_ref, vct_ref, o_ref):
    t_len = q_ref.shape[0]
    w_len = kct_ref.shape[-1]
    pairs = q_ref.shape[1] // LANES
    pad = LANES
    zpad = jnp.zeros((pad - t_len, LANES), F32)

    def multiplicity(dist):
        c = jnp.zeros(dist.shape, F32)
        for d in ATT_DILATIONS:
            ok = (dist >= 0) & ((dist & (d - 1)) == 0) & (dist <= ATT_SPAN * d)
            c = c + jnp.where(ok, 1.0, 0.0)
        return c

    tc = lax.broadcasted_iota(jnp.int32, (t_len, w_len), 0)
    jc = lax.broadcasted_iota(jnp.int32, (t_len, w_len), 1)
    cnt_c = multiplicity(w_len + tc - jc)
    tn = lax.broadcasted_iota(jnp.int32, (t_len, pad), 0)
    jn = lax.broadcasted_iota(jnp.int32, (t_len, pad), 1)
    cnt_n = multiplicity(jnp.where(jn < t_len, tn - jn, -1))

    lane = lax.broadcasted_iota(jnp.int32, (t_len, LANES), 1)
    head0 = lane < HEAD_DIM
    for p in range(pairs):
        lanes = slice(p * LANES, (p + 1) * LANES)
        q = q_ref[:, lanes] * (HEAD_DIM ** -0.5)
        kn = jnp.concatenate([kn_ref[:, lanes], zpad], axis=0).astype(BF16)
        vn = jnp.concatenate([vn_ref[:, lanes], zpad], axis=0).astype(BF16)
        kct = kct_ref[2 * p:2 * p + 2].reshape(LANES, w_len).astype(BF16)
        vct = vct_ref[2 * p:2 * p + 2].reshape(LANES, w_len).astype(BF16)
        outs = []
        for h in range(2):
            hm = head0 if h == 0 else jnp.logical_not(head0)
            qh = jnp.where(hm, q, 0.0).astype(BF16)
            sc = jnp.where(cnt_c > 0, _dot(qh, kct), NEG)
            sn = jnp.where(cnt_n > 0, _dot_nt(qh, kn), NEG)
            m = jnp.maximum(jnp.max(sc, axis=-1, keepdims=True), jnp.max(sn, axis=-1, keepdims=True))
            ec = cnt_c * jnp.exp(sc - m)
            en = cnt_n * jnp.exp(sn - m)
            l = jnp.sum(ec, axis=-1, keepdims=True) + jnp.sum(en, axis=-1, keepdims=True)
            pv = _dot_nt(ec.astype(BF16), vct) + _dot(en.astype(BF16), vn)
            outs.append(pv / l)
        o_ref[:, lanes] = jnp.where(head0, outs[0], outs[1])


def _attn_sample(q, kv, cache_t, layer, *, pairs=4):
    b, t, _ = q.shape
    w = cache_t.shape[-1]
    assert all(d & (d - 1) == 0 for d in ATT_DILATIONS) and t <= LANES
    width = pairs * LANES
    nstep = ATT_WIDTH // width
    vmem = 2 * 2 * _nbytes((width, w), F32) + 10 * _nbytes((LANES, w), F32) * pairs + 4 * MIB
    new_spec = lambda off: pl.BlockSpec((None, t, width), lambda i, j: (i, 0, off + j))
    buf_spec = lambda c: pl.BlockSpec((None, None, None, 2 * pairs, HEAD_DIM, w),
                                      lambda i, j: (layer, i, c, j, 0, 0))
    return pl.pallas_call(
        _attn_sample_kernel,
        out_shape=jax.ShapeDtypeStruct((b, t, ATT_WIDTH), F32),
        grid=(b, nstep),
        in_specs=[new_spec(0), new_spec(0), new_spec(nstep), buf_spec(0), buf_spec(1)],
        out_specs=new_spec(0),
        compiler_params=_params(("parallel", "parallel"), vmem),
        name="attn_sample",
    )(q, kv, kv, cache_t, cache_t)


def _ssm_prep_kernel(are_ref, aim_ref, ldt_ref, bre_ref, bim_ref, abre_ref, abim_ref, bbre_ref, bbim_ref):
    a_re, a_im = are_ref[...], aim_ref[...]
    dt = jnp.exp(ldt_ref[...])
    mag = jnp.exp(a_re * dt)
    ang = a_im * dt
    ab_re, ab_im = mag * jnp.cos(ang), mag * jnp.sin(ang)
    den = a_re * a_re + a_im * a_im
    nr, ni = ab_re - 1.0, ab_im
    cf_re = (nr * a_re + ni * a_im) / den
    cf_im = (ni * a_re - nr * a_im) / den
    abre_ref[...] = ab_re
    abim_ref[...] = ab_im
    for c in range(SSM_GROUP):
        b_re, b_im = bre_ref[c], bim_ref[c]
        bbre_ref[c] = cf_re * b_re - cf_im * b_im
        bbim_ref[c] = cf_re * b_im + cf_im * b_re


def _ssm_prep(a_re, a_im, log_dt, b_re, b_im):
    depth, g, n = a_re.shape
    rows = depth * g
    flat = lambda a: a.reshape(rows, n)
    chan_major = lambda b: jnp.transpose(b.reshape(rows, n, SSM_GROUP), (2, 0, 1))
    ab_re, ab_im, bb_re, bb_im = pl.pallas_call(
        _ssm_prep_kernel,
        out_shape=(jax.ShapeDtypeStruct((rows, n), F32), jax.ShapeDtypeStruct((rows, n), F32),
                   jax.ShapeDtypeStruct((SSM_GROUP, rows, n), F32),
                   jax.ShapeDtypeStruct((SSM_GROUP, rows, n), F32)),
        name="ssm_prep",
    )(flat(a_re), flat(a_im), log_dt.reshape(rows, 1), chan_major(b_re), chan_major(b_im))
    unflat = lambda b: jnp.transpose(b, (1, 0, 2)).reshape(depth, g, SSM_GROUP, n)
    return ab_re.reshape(depth, g, n), ab_im.reshape(depth, g, n), unflat(bb_re), unflat(bb_im)


def _block_diag(m):
    *lead, g, r, c = m.shape
    nb = g // SSM_BLOCK_GROUPS
    m = m.reshape(*lead, nb, SSM_BLOCK_GROUPS, r, c)
    eye = jnp.eye(SSM_BLOCK_GROUPS, dtype=m.dtype)
    out = m[..., :, :, None, :] * eye[:, None, :, None]
    return out.reshape(*lead, nb, SSM_BLOCK_GROUPS * r, SSM_BLOCK_GROUPS * c)


def _ssm_kernel(u_ref, h0re_ref, h0im_ref, are_ref, aim_ref, bb_ref, cre_ref, cim_ref, d_ref,
                y_ref, hre_ref, him_ref, x_scr, up_scr, *, steps, chained):
    ns = SSM_BLOCK_STATE
    nrows = steps * SCAN_SEQS
    chunk = min(nrows, 256)

    nchunks = nrows // chunk
    chunk_steps = chunk // SCAN_SEQS
    unroll = min(steps, 8)

    def permute_in(t, c):
        up_scr[pl.ds(pl.multiple_of(t * SCAN_SEQS, SCAN_SEQS), SCAN_SEQS), :] = (
            u_ref[pl.ds(t, SCAN_SEQS, stride=steps), :])
        return c
    lax.fori_loop(0, steps, permute_in, 0, unroll=unroll)

    a_re = jnp.broadcast_to(are_ref[...], (SCAN_SEQS, ns))
    a_im = jnp.broadcast_to(aim_ref[...], (SCAN_SEQS, ns))

    def input_map(c):
        rows = pl.ds(c * chunk, chunk)
        x_scr[rows, :] = _dot(up_scr[rows, :].astype(BF16), bb_ref[...])

    def scan_chunk(c, h, store):
        h_re, h_im = h
        for t in range(c * chunk_steps, (c + 1) * chunk_steps):
            rows = pl.ds(t * SCAN_SEQS, SCAN_SEQS)
            h_re, h_im = (a_re * h_re - a_im * h_im + x_scr[rows, :ns],
                          a_re * h_im + a_im * h_re + x_scr[rows, ns:])
            if store:
                x_scr[rows, :ns] = h_re
                x_scr[rows, ns:] = h_im
        return h_re, h_im

    def output_map(c):
        rows = pl.ds(c * chunk, chunk)
        y = (_dot(x_scr[rows, :ns].astype(BF16), cre_ref[...])
             - _dot(x_scr[rows, ns:].astype(BF16), cim_ref[...])
             + d_ref[...] * up_scr[rows, :])
        up_scr[rows, :] = _gelu_tanh(y)

    init = (h0re_ref[...], h0im_ref[...])
    input_map(0)
    if chained:
        assert steps & (steps - 1) == 0
        e_re, e_im = init
        for c in range(nchunks):
            if c + 1 < nchunks:
                input_map(c + 1)
            e_re, e_im = scan_chunk(c, (e_re, e_im), store=False)
        p_re, p_im = a_re, a_im
        for _ in range(steps.bit_length() - 1):
            p_re, p_im = p_re * p_re - p_im * p_im, 2.0 * p_re * p_im
        sub = lax.broadcasted_iota(jnp.int32, (SCAN_SEQS, ns), 0)
        shift = 1
        while shift < SCAN_SEQS:
            r_re = pltpu.roll(e_re, shift, axis=0)
            r_im = pltpu.roll(e_im, shift, axis=0)
            keep = sub >= shift
            e_re, e_im = (e_re + jnp.where(keep, p_re * r_re - p_im * r_im, 0.0),
                          e_im + jnp.where(keep, p_re * r_im + p_im * r_re, 0.0))
            p_re, p_im = p_re * p_re - p_im * p_im, 2.0 * p_re * p_im
            shift *= 2
        first = sub == 0
        init = (init[0] + jnp.where(first, 0.0, pltpu.roll(e_re, 1, axis=0)),
                init[1] + jnp.where(first, 0.0, pltpu.roll(e_im, 1, axis=0)))
    else:
        for c in range(1, nchunks):
            input_map(c)
    h = init
    for c in range(nchunks):
        h = scan_chunk(c, h, store=True)
        if c > 0:
            output_map(c - 1)
    output_map(nchunks - 1)
    hre_ref[...] = h[0]
    him_ref[...] = h[1]

    def permute_out(t, c):
        y_ref[pl.ds(t, SCAN_SEQS, stride=steps), :] = (
            up_scr[pl.ds(pl.multiple_of(t * SCAN_SEQS, SCAN_SEQS), SCAN_SEQS), :])
        return c
    lax.fori_loop(0, steps, permute_out, 0, unroll=unroll)


def _ssm(u, h0_re, h0_im, h0_layer, ssm_w, layer, *, chained):
    nb, rows, _ = u.shape
    steps = rows // SCAN_SEQS
    nblk = SSM_WIDTH // LANES
    ns = SSM_BLOCK_STATE
    vmem = (2 * 2 * _nbytes((rows, LANES), F32) + _nbytes((rows, 2 * ns), F32)
            + _nbytes((rows, LANES), F32) + 4 * _nbytes((min(rows, 256), 2 * ns), F32) + 8 * MIB)
    h0_spec = pl.BlockSpec((None, None, SCAN_SEQS, ns), lambda i, j: (h0_layer, i, 0, j))
    state_spec = pl.BlockSpec((None, SCAN_SEQS, ns), lambda i, j: (i, 0, j))
    return pl.pallas_call(
        functools.partial(_ssm_kernel, steps=steps, chained=chained),
        out_shape=(jax.ShapeDtypeStruct((nb, rows, SSM_WIDTH), F32),
                   jax.ShapeDtypeStruct(h0_re.shape[1:], F32),
                   jax.ShapeDtypeStruct(h0_im.shape[1:], F32)),
        grid=(nb, nblk),
        in_specs=[pl.BlockSpec((None, rows, LANES), lambda i, j: (i, 0, j)),
                  h0_spec, h0_spec,
                  pl.BlockSpec((None, 1, ns), lambda i, j: (layer, 0, j)),
                  pl.BlockSpec((None, 1, ns), lambda i, j: (layer, 0, j)),
                  pl.BlockSpec((None, None, LANES, 2 * ns), lambda i, j: (layer, j, 0, 0)),
                  pl.BlockSpec((None, None, ns, LANES), lambda i, j: (layer, j, 0, 0)),
                  pl.BlockSpec((None, None, ns, LANES), lambda i, j: (layer, j, 0, 0)),
                  pl.BlockSpec((None, 1, LANES), lambda i, j: (layer, 0, j))],
        out_specs=(pl.BlockSpec((None, rows, LANES), lambda i, j: (i, 0, j)), state_spec, state_spec),
        scratch_shapes=[pltpu.VMEM((rows, 2 * ns), F32), pltpu.VMEM((rows, LANES), F32)],
        compiler_params=_params(("parallel", "parallel"), vmem,
                                _nbytes((nb, rows, SSM_WIDTH), F32)),
        name="ssm_chained" if chained else "ssm_step",
    )(u, h0_re, h0_im, *ssm_w)


def _conv_kernel(h_ref, gb_ref, gc_ref, w_ref, pre_ref, y_ref, st_ref):
    n = h_ref.shape[0]
    u = gc_ref[...] * h_ref[...]
    row = lax.broadcasted_iota(jnp.int32, u.shape, 0)
    pre = pre_ref[...]
    p0, p1 = pre[0:1, :], pre[1:2, :]
    u1 = jnp.where(row == 0, p1, pltpu.roll(u, 1, axis=0))
    u2 = jnp.where(row == 0, p0, jnp.where(row == 1, p1, pltpu.roll(u, 2, axis=0)))
    w = w_ref[...]
    y = w[0:1, :] * u2 + w[1:2, :] * u1 + w[2:3, :] * u
    y_ref[...] = gb_ref[...] * y
    st_ref[...] = u[n - (CONV_K - 1):, :]


def _conv(rest, w, layer, prefix, prefix_layer):
    b, n, _ = rest.shape
    assert n >= CONV_K - 1
    nblk = CONV_CH // LANES
    off = SSM_WIDTH // LANES
    col = lambda part: pl.BlockSpec((None, n, LANES), lambda i, j: (i, 0, off + part * nblk + j))
    vmem = 2 * 4 * _nbytes((n, LANES), F32) + 8 * _nbytes((n, LANES), F32) + 4 * MIB
    return pl.pallas_call(
        _conv_kernel,
        out_shape=(jax.ShapeDtypeStruct((b, n, CONV_CH), F32),
                   jax.ShapeDtypeStruct((b, CONV_K - 1, CONV_CH), F32)),
        grid=(b, nblk),
        in_specs=[col(0), col(1), col(2),
                  pl.BlockSpec((None, CONV_K, LANES), lambda i, j: (layer, 0, j)),
                  pl.BlockSpec((None, None, CONV_K - 1, LANES), lambda i, j: (prefix_layer, i, 0, j))],
        out_specs=(pl.BlockSpec((None, n, LANES), lambda i, j: (i, 0, j)),
                   pl.BlockSpec((None, CONV_K - 1, LANES), lambda i, j: (i, 0, j))),
        compiler_params=_params(("parallel", "parallel"), vmem,
                                _nbytes((b, n, CONV_CH), F32)),
        name="conv",
    )(rest, rest, rest, w, prefix)


def _out_proj_kernel(oatt_ref, yssm_ref, yconv_ref, x_ref, oatts_ref, yssms_ref, yconvs_ref, xs_ref,
                     g_ref, wglu_ref, bglu_ref, w_ref, o_ref, os_ref, mix_ref, *, tm, tn):
    i, j = pl.program_id(0), pl.program_id(1)
    a0, a1 = ATT_WIDTH, ATT_WIDTH + SSM_WIDTH

    def mixed(rows, o_att, y_ssm, y_conv):
        g = g_ref[...]
        mix_ref[rows, :a0] = _rms(o_att, g[:, :a0]).astype(BF16)
        y_ssm = y_ssm * _sigmoid(_dot(y_ssm.astype(BF16), wglu_ref[...].astype(BF16)) + bglu_ref[...])
        mix_ref[rows, a0:a1] = _rms(y_ssm, g[:, a0:a1]).astype(BF16)
        mix_ref[rows, a1:] = _rms(y_conv, g[:, a1:]).astype(BF16)

    @pl.when(j == 0)
    def _():
        mixed(slice(0, tm), oatt_ref[...], yssm_ref[...], yconv_ref[...])

    @pl.when((j == 0) & (i == 0))
    def _():
        mixed(slice(tm, None), oatts_ref[...], yssms_ref[...], yconvs_ref[...])

    @pl.when(i == 0)
    def _():
        z = _dot(mix_ref[...], w_ref[...].astype(BF16))
        o_ref[...] = x_ref[...] + z[:tm]
        cols = pl.ds(pl.multiple_of(j * tn, tn), tn)
        os_ref[:, cols] = xs_ref[:, cols] + z[tm:]

    @pl.when(i > 0)
    def _():
        o_ref[...] = x_ref[...] + _dot(mix_ref[:tm], w_ref[...].astype(BF16))


def _out_proj(prompt, sample, g, w_glu, b_glu, w, layer, *, tm, tn=512):
    m, d = prompt[3].shape
    ms = sample[3].shape[0]
    k = w.shape[1]
    vmem = (2 * (_nbytes((tm, k), F32) + 2 * _nbytes((tm, tn), F32) + 3 * _nbytes((ms, k), F32)
                 + _nbytes((k, tn), F32) + _nbytes((SSM_WIDTH, SSM_WIDTH), F32))
            + _nbytes((tm + ms, k), BF16) + _nbytes((k, tn), BF16) + 2 * _nbytes((tm + ms, k), F32)
            + 2 * MIB)
    row = lambda c: pl.BlockSpec((tm, c), lambda i, j: (i, 0))
    whole = lambda c: pl.BlockSpec((ms, c), lambda i, j: (0, 0))
    layered = lambda r, c: pl.BlockSpec((None, r, c), lambda i, j: (layer, 0, 0))
    return pl.pallas_call(
        functools.partial(_out_proj_kernel, tm=tm, tn=tn),
        out_shape=(jax.ShapeDtypeStruct((m, d), F32), jax.ShapeDtypeStruct((ms, d), F32)),
        grid=(m // tm, d // tn),
        in_specs=[row(ATT_WIDTH), row(SSM_WIDTH), row(CONV_CH),
                  pl.BlockSpec((tm, tn), lambda i, j: (i, j)),
                  whole(ATT_WIDTH), whole(SSM_WIDTH), whole(CONV_CH), whole(d),
                  layered(1, k),
                  layered(SSM_WIDTH, SSM_WIDTH),
                  layered(1, SSM_WIDTH),
                  pl.BlockSpec((None, k, tn), lambda i, j: (layer, 0, j))],
        out_specs=(pl.BlockSpec((tm, tn), lambda i, j: (i, j)), whole(d)),
        scratch_shapes=[pltpu.VMEM((tm + ms, k), BF16)],
        compiler_params=_params(("arbitrary", "arbitrary"), vmem),
        name="out_proj",
    )(*prompt, *sample, g, w_glu, b_glu, w)


def _ffn_kernel(x_ref, xs_ref, g_ref, gfin_ref, wg_ref, wu_ref, wd_ref, o_ref, os_ref, xn_ref, *,
                tm, final):
    i, f = pl.program_id(0), pl.program_id(1)
    last = f == pl.num_programs(1) - 1

    @pl.when(f == 0)
    def _():
        x = x_ref[...]
        xn_ref[:tm] = _rms(x, g_ref[...]).astype(BF16)
        o_ref[...] = x

    @pl.when((f == 0) & (i == 0))
    def _():
        xs = xs_ref[...]
        xn_ref[tm:] = _rms(xs, g_ref[...]).astype(BF16)
        os_ref[...] = xs

    def swiglu(xn):
        gate = _dot(xn, wg_ref[...].astype(BF16))
        up = _dot(xn, wu_ref[...].astype(BF16))
        return _dot((gate * _sigmoid(gate) * up).astype(BF16), wd_ref[...].astype(BF16))

    @pl.when(i == 0)
    def _():
        y = swiglu(xn_ref[...])
        o_ref[...] += y[:tm]
        os_ref[...] += y[tm:]

    @pl.when(i > 0)
    def _():
        o_ref[...] += swiglu(xn_ref[:tm])

    if final:
        @pl.when(last)
        def _():
            o_ref[...] = _rms(o_ref[...], gfin_ref[...])

        @pl.when(last & (i == 0))
        def _():
            os_ref[...] = _rms(os_ref[...], gfin_ref[...])


def _ffn(x, xs, g, w_gate, w_up, w_down, layer, *, tm, g_final=None, tf=256):
    final = g_final is not None
    m, d = x.shape
    g_last = g_final.reshape(1, d) if final else g[layer]
    ms = xs.shape[0]
    ff = w_gate.shape[2]
    vmem = (2 * (2 * _nbytes((tm, d), F32) + 2 * _nbytes((ms, d), F32) + 3 * _nbytes((d, tf), F32))
            + _nbytes((tm + ms, d), BF16) + 3 * _nbytes((d, tf), BF16) + 4 * _nbytes((tm + ms, tf), F32)
            + 2 * MIB)
    return pl.pallas_call(
        functools.partial(_ffn_kernel, tm=tm, final=final),
        out_shape=(jax.ShapeDtypeStruct((m, d), F32), jax.ShapeDtypeStruct((ms, d), F32)),
        grid=(m // tm, ff // tf),
        in_specs=[pl.BlockSpec((tm, d), lambda i, f: (i, 0)),
                  pl.BlockSpec((ms, d), lambda i, f: (0, 0)),
                  pl.BlockSpec((None, 1, d), lambda i, f: (layer, 0, 0)),
                  pl.BlockSpec((1, d), lambda i, f: (0, 0)),
                  pl.BlockSpec((None, d, tf), lambda i, f: (layer, 0, f)),
                  pl.BlockSpec((None, d, tf), lambda i, f: (layer, 0, f)),
                  pl.BlockSpec((None, tf, d), lambda i, f: (layer, f, 0))],
        out_specs=(pl.BlockSpec((tm, d), lambda i, f: (i, 0)),
                   pl.BlockSpec((ms, d), lambda i, f: (0, 0))),
        scratch_shapes=[pltpu.VMEM((tm + ms, d), BF16)],
        compiler_params=_params(("arbitrary", "arbitrary"), vmem),
        name="ffn",
    )(x, xs, g, g_last, w_gate, w_up, w_down)


def _mixers(q, kv, rest, ssm_w, conv_w, layer, *, batch, cache_t, conv_prefix, ssm_h0, state_layer):
    m = q.shape[0]
    n = m // batch
    q3, kv3, rest3 = (a.reshape(batch, n, a.shape[-1]) for a in (q, kv, rest))
    if cache_t is None:
        o_att = _attn_prompt(q3, kv3)
        y_ssm, h_re, h_im = _ssm(rest3, *ssm_h0, state_layer, ssm_w, layer, chained=True)
        h_re, h_im = h_re[:, SCAN_SEQS - 1], h_im[:, SCAN_SEQS - 1]
    else:
        o_att = _attn_sample(q3, kv3, cache_t, layer)
        y_ssm, h_re, h_im = _ssm(rest.reshape(1, m, rest.shape[-1]), *ssm_h0, state_layer, ssm_w, layer,
                                 chained=False)
        h_re, h_im = h_re[0], h_im[0]
    y_conv, conv_state = _conv(rest3, conv_w, layer, conv_prefix, state_layer)
    mixed = (o_att.reshape(m, ATT_WIDTH), y_ssm.reshape(m, SSM_WIDTH), y_conv.reshape(m, CONV_CH))
    return mixed, conv_state, h_re, h_im


def kernel(x_prompt, x_sample, cache_kv, state_conv, state_ssm_re, state_ssm_im, g_mix_norm, w_in,
           ssm_a_re, ssm_a_im, ssm_log_dt, ssm_b_re, ssm_b_im, ssm_c_re, ssm_c_im, ssm_d, ssm_w_glu,
           ssm_b_glu, conv_w, g_out, w_out, g_ffn_norm, w_gate, w_up, w_down, g_final):
    batch, seq, d_model = x_prompt.shape
    dec_batch, dec_seq, _ = x_sample.shape
    depth, n_groups, n_state = ssm_a_re.shape
    n_heads = ATT_WIDTH // HEAD_DIM
    assert dec_batch == SCAN_SEQS and seq % SCAN_SEQS == 0

    ab_re, ab_im, bb_re, bb_im = _ssm_prep(ssm_a_re, ssm_a_im, ssm_log_dt, ssm_b_re, ssm_b_im)
    cache_t = jnp.transpose(cache_kv, (0, 1, 3, 4, 5, 2))

    row = lambda a: a.reshape(depth, 1, -1)
    ssm_w = (row(ab_re), row(ab_im),
             jnp.concatenate([_block_diag(bb_re), _block_diag(bb_im)], axis=-1).astype(BF16),
             _block_diag(jnp.swapaxes(ssm_c_re, -1, -2)).astype(BF16),
             _block_diag(jnp.swapaxes(ssm_c_im, -1, -2)).astype(BF16),
             row(ssm_d))
    g_mix, g_mixed, g_ffn, b_glu = row(g_mix_norm), row(g_out), row(g_ffn_norm), row(ssm_b_glu)
    state_ssm = (state_ssm_re.reshape(depth, 1, dec_batch, -1), state_ssm_im.reshape(depth, 1, dec_batch, -1))

    xp = x_prompt.reshape(batch * seq, d_model)
    xs = x_sample.reshape(dec_batch * dec_seq, d_model)
    zero_state = jnp.zeros((1, batch, SCAN_SEQS, n_groups * n_state), F32)
    zero_prefix = jnp.zeros((1, batch, CONV_K - 1, CONV_CH), F32)
    kvt = jnp.zeros((depth, batch, 2 * ATT_WIDTH, seq), F32)

    outs_p, outs_s = [], []
    for i in range(depth):
        q, kv, rest, kvt, q_s, kv_s, rest_s = _in_proj(xp, xs, g_mix, w_in, kvt, i, tm=2048, seq=seq, tn=256)

        mixed_p, c_state, h_re, h_im = _mixers(
            q, kv, rest, ssm_w, conv_w, i, batch=batch, cache_t=None, conv_prefix=zero_prefix,
            ssm_h0=(zero_state, zero_state), state_layer=0)
        outs_p.append((c_state, h_re.reshape(batch, n_groups, n_state),
                       h_im.reshape(batch, n_groups, n_state)))

        mixed_s, c_state, h_re, h_im = _mixers(
            q_s, kv_s, rest_s, ssm_w, conv_w, i, batch=dec_batch, cache_t=cache_t, conv_prefix=state_conv,
            ssm_h0=state_ssm, state_layer=i)
        outs_s.append((kv_s.reshape(dec_batch, dec_seq, 2, n_heads, HEAD_DIM), c_state,
                       h_re.reshape(dec_batch, n_groups, n_state),
                       h_im.reshape(dec_batch, n_groups, n_state)))

        xp, xs = _out_proj((*mixed_p, xp), (*mixed_s, xs), g_mixed, ssm_w_glu, b_glu, w_out, i, tm=1024)
        xp, xs = _ffn(xp, xs, g_ffn, w_gate, w_up, w_down, i, tm=1024,
                      g_final=g_final if i == depth - 1 else None)

    y_prompt = xp.reshape(batch, seq, d_model)
    y_sample = xs.reshape(dec_batch, dec_seq, d_model)
    kv_prompt = jnp.transpose(kvt.reshape(depth, batch, 2, n_heads, HEAD_DIM, seq), (0, 1, 5, 2, 3, 4))
    n_keep = min(max(ATT_DILATIONS) * ATT_SPAN, seq)
    stack = lambda outs, k: jnp.stack([o[k] for o in outs])
    return (y_prompt, y_sample,
            kv_prompt[:, :, seq - n_keep:], stack(outs_p, 0), stack(outs_p, 1), stack(outs_p, 2),
            stack(outs_s, 0), stack(outs_s, 1), stack(outs_s, 2), stack(outs_s, 3))
```

```python
import functools
import math

import jax
import jax.numpy as jnp
from jax import lax
from jax.experimental import pallas as pl
from jax.experimental.pallas import tpu as pltpu

F32 = jnp.float32
BF16 = jnp.bfloat16

HEAD_DIM = 64
ATT_WIDTH = 1024
SSM_WIDTH = 512
CONV_CH = 512
CONV_K = 3
SSM_GROUP = 16
SSM_STATE = 64
ATT_DILATIONS = (1, 4, 16)
ATT_SPAN = 128
RMS_EPS = 1e-6
NEG = -1e30

LANES = 128
SUBLANES = 8
SCAN_SEQS = SUBLANES
SSM_BLOCK_GROUPS = LANES // SSM_GROUP
SSM_BLOCK_STATE = SSM_BLOCK_GROUPS * SSM_STATE

MIB = 1024 * 1024
V7X_VMEM_BYTES = 64 * MIB


def _params(semantics, vmem_bytes, hbm_result_bytes=None):
    if hbm_result_bytes is not None and hbm_result_bytes >= V7X_VMEM_BYTES // 8:
        vmem_bytes = max(vmem_bytes, V7X_VMEM_BYTES - hbm_result_bytes + MIB)
    return pltpu.CompilerParams(dimension_semantics=semantics, vmem_limit_bytes=int(vmem_bytes))


def _nbytes(shape, dtype):
    return math.prod(shape) * jnp.dtype(dtype).itemsize


def _rms(x, g):
    ms = jnp.mean(x * x, axis=-1, keepdims=True)
    return x * lax.rsqrt(ms + RMS_EPS) * g


def _sigmoid(x):
    return 1.0 / (1.0 + jnp.exp(-x))


def _gelu_tanh(x):
    c = math.sqrt(2.0 / math.pi)
    return 0.5 * x * (1.0 + jnp.tanh(c * (x + 0.044715 * (x * x * x))))


def _dot(a, b):
    return jnp.dot(a, b, preferred_element_type=F32)


def _dot_nt(a, b):
    return lax.dot_general(a, b, (((1,), (1,)), ((), ())), preferred_element_type=F32)


def _in_proj_kernel(x_ref, xs_ref, g_ref, w_ref, kvt_in_ref, q_ref, kv_ref, rest_ref, kvt_ref,
                    qs_ref, kvs_ref, rests_ref, xn_ref, *, tm, tn, nq, nkv):
    del kvt_in_ref
    i, j = pl.program_id(0), pl.program_id(1)

    @pl.when(j == 0)
    def _():
        xn_ref[:tm] = _rms(x_ref[...], g_ref[...]).astype(BF16)

    @pl.when((j == 0) & (i == 0))
    def _():
        xn_ref[tm:] = _rms(xs_ref[...], g_ref[...]).astype(BF16)

    def project(out_ref, outs_ref, col, transposed_ref=None):
        @pl.when(i == 0)
        def _():
            z = _dot(xn_ref[...], w_ref[...].astype(BF16))
            out_ref[...] = z[:tm]
            outs_ref[:, pl.ds(pl.multiple_of(col * tn, tn), tn)] = z[tm:]
            if transposed_ref is not None:
                transposed_ref[...] = z[:tm].T

        @pl.when(i > 0)
        def _():
            z = _dot(xn_ref[:tm], w_ref[...].astype(BF16))
            out_ref[...] = z
            if transposed_ref is not None:
                transposed_ref[...] = z.T

    @pl.when(j < nq)
    def _():
        project(q_ref, qs_ref, j)

    @pl.when((j >= nq) & (j < nq + nkv))
    def _():
        project(kv_ref, kvs_ref, j - nq, kvt_ref)

    @pl.when(j >= nq + nkv)
    def _():
        project(rest_ref, rests_ref, j - nq - nkv)


def _in_proj(x, xs, g, w, kvt_stack, layer, *, tm, seq, tn=512):
    m, k = x.shape
    ms = xs.shape[0]
    n = w.shape[2]
    nq, nkv = ATT_WIDTH // tn, 2 * ATT_WIDTH // tn
    nrest = n // tn - nq - nkv
    assert seq % tm == 0 and tm % LANES == 0
    assert kvt_stack.shape[1:] == (m // seq, nkv * tn, seq)
    tiles_per_seq = seq // tm
    vmem = (_nbytes((tm, k), F32)
            + 2 * (_nbytes((ms, k), F32) + _nbytes((k, tn), F32) + 4 * _nbytes((tm, tn), F32)
                   + _nbytes((ms, n), F32))
            + _nbytes((tm + ms, k), BF16) + _nbytes((k, tn), BF16) + 3 * _nbytes((tm + ms, tn), F32)
            + 2 * MIB)
    kv_col = lambda j: jnp.clip(j - nq, 0, nkv - 1)
    whole = lambda cols: pl.BlockSpec((ms, cols), lambda i, j: (0, 0))
    return pl.pallas_call(
        functools.partial(_in_proj_kernel, tm=tm, tn=tn, nq=nq, nkv=nkv),
        out_shape=(jax.ShapeDtypeStruct((m, nq * tn), F32),
                   jax.ShapeDtypeStruct((m, nkv * tn), F32),
                   jax.ShapeDtypeStruct((m, nrest * tn), F32),
                   jax.ShapeDtypeStruct(kvt_stack.shape, F32),
                   jax.ShapeDtypeStruct((ms, nq * tn), F32),
                   jax.ShapeDtypeStruct((ms, nkv * tn), F32),
                   jax.ShapeDtypeStruct((ms, nrest * tn), F32)),
        grid=(m // tm, n // tn),
        in_specs=[pl.BlockSpec((tm, k), lambda i, j: (i, 0), pipeline_mode=pl.Buffered(1)),
                  whole(k),
                  pl.BlockSpec((None, 1, k), lambda i, j: (layer, 0, 0)),
                  pl.BlockSpec((None, k, tn), lambda i, j: (layer, 0, j)),
                  pl.BlockSpec(memory_space=pl.ANY)],
        out_specs=(pl.BlockSpec((tm, tn), lambda i, j: (i, jnp.minimum(j, nq - 1))),
                   pl.BlockSpec((tm, tn), lambda i, j: (i, kv_col(j))),
                   pl.BlockSpec((tm, tn), lambda i, j: (i, jnp.clip(j - nq - nkv, 0, nrest - 1))),
                   pl.BlockSpec((None, None, tn, tm),
                                lambda i, j: (layer, i // tiles_per_seq, kv_col(j), i % tiles_per_seq)),
                   whole(nq * tn), whole(nkv * tn), whole(nrest * tn)),
        scratch_shapes=[pltpu.VMEM((tm + ms, k), BF16)],
        input_output_aliases={4: 3},
        compiler_params=_params(("arbitrary", "arbitrary"), vmem),
        name="in_proj",
    )(x, xs, g, w, kvt_stack)


def _attn_prompt_kernel(q_ref, k_ref, v_ref, o_ref, ob_ref, mb_ref, lb_ref, bias_ref):
    s_len = q_ref.shape[0]
    blk = ATT_SPAN
    lane = lax.broadcasted_iota(jnp.int32, (blk, LANES), 1)
    head0 = lane < HEAD_DIM
    scale = HEAD_DIM ** -0.5

    qi = lax.broadcasted_iota(jnp.int32, (2 * blk, 3 * blk), 0) & (blk - 1)
    kj = lax.broadcasted_iota(jnp.int32, (2 * blk, 3 * blk), 1)
    dist = jnp.where(kj < blk, qi - kj, qi + 2 * blk - kj)
    bias_ref[...] = jnp.where((dist >= 0) & (dist <= ATT_SPAN), 0.0, NEG)

    def block(p, d, q0, k0=None):
        def rows(start, n):
            return pl.ds(start, n, stride=d) if d > 1 else pl.ds(start, n)

        qt = q_ref[rows(q0, blk), :] * scale
        qq = jnp.concatenate([jnp.where(head0, qt, 0.0), jnp.where(head0, 0.0, qt)], axis=0).astype(BF16)
        if k0 is None:
            key_rows, bias = rows(q0, blk), bias_ref[:, :blk]
        else:
            key_rows, bias = rows(k0, 2 * blk), bias_ref[:, blk:]
        k = k_ref[key_rows, :].astype(BF16)
        v = v_ref[key_rows, :].astype(BF16)
        s = _dot_nt(qq, k) + bias
        m = jnp.max(s, axis=-1, keepdims=True)
        e = jnp.exp(s - m)
        l = jnp.sum(e, axis=-1, keepdims=True)
        o = _dot(e.astype(BF16), v)
        m = jnp.broadcast_to(m, (2 * blk, LANES))
        l = jnp.broadcast_to(l, (2 * blk, LANES))
        ob_ref[p, rows(q0, blk), :] = jnp.where(head0, o[:blk], o[blk:])
        mb_ref[p, rows(q0, blk), :] = jnp.where(head0, m[:blk], m[blk:])
        lb_ref[p, rows(q0, blk), :] = jnp.where(head0, l[:blk], l[blk:])

    for p, d in enumerate(ATT_DILATIONS):
        nblk = s_len // (d * blk)
        for r in range(d):
            for b in range(nblk):
                block(p, d, r + d * blk * b, None if b == 0 else r + d * blk * (b - 1))

    def merge(c, carry):
        rows = pl.ds(pl.multiple_of(c * blk, blk), blk)
        m0, m1, m2 = mb_ref[0, rows, :], mb_ref[1, rows, :], mb_ref[2, rows, :]
        m = jnp.maximum(jnp.maximum(m0, m1), m2)
        e0, e1, e2 = jnp.exp(m0 - m), jnp.exp(m1 - m), jnp.exp(m2 - m)
        num = e0 * ob_ref[0, rows, :] + e1 * ob_ref[1, rows, :] + e2 * ob_ref[2, rows, :]
        den = e0 * lb_ref[0, rows, :] + e1 * lb_ref[1, rows, :] + e2 * lb_ref[2, rows, :]
        o_ref[rows, :] = num / den
        return carry

    lax.fori_loop(0, s_len // blk, merge, 0)


def _attn_prompt(q, kv):
    b, s, _ = q.shape
    assert s % (max(ATT_DILATIONS) * ATT_SPAN) == 0
    npair = ATT_WIDTH // LANES
    blk_bytes = _nbytes((s, LANES), F32)
    vmem = 2 * 4 * blk_bytes + 3 * len(ATT_DILATIONS) * blk_bytes + 4 * MIB
    return pl.pallas_call(
        _attn_prompt_kernel,
        out_shape=jax.ShapeDtypeStruct((b, s, ATT_WIDTH), F32),
        grid=(b, npair),
        in_specs=[pl.BlockSpec((None, s, LANES), lambda i, j: (i, 0, j)),
                  pl.BlockSpec((None, s, LANES), lambda i, j: (i, 0, j)),
                  pl.BlockSpec((None, s, LANES), lambda i, j: (i, 0, npair + j))],
        out_specs=pl.BlockSpec((None, s, LANES), lambda i, j: (i, 0, j)),
        scratch_shapes=[pltpu.VMEM((len(ATT_DILATIONS), s, LANES), F32),
                        pltpu.VMEM((len(ATT_DILATIONS), s, LANES), F32),
                        pltpu.VMEM((len(ATT_DILATIONS), s, LANES), F32),
                        pltpu.VMEM((2 * ATT_SPAN, 3 * ATT_SPAN), F32)],
        compiler_params=_params(("parallel", "parallel"), vmem, _nbytes((b, s, ATT_WIDTH), F32)),
        name="attn_prompt",
    )(q, kv, kv)


def _attn_sample_kernel(q_ref, kn_ref, vn_ref, kct_ref, vct_ref, o_ref):
    t_len = q_ref.shape[0]
    w_len = kct_ref.shape[-1]
    pairs = q_ref.shape[1] // LANES
    pad = LANES
    zpad = jnp.zeros((pad - t_len, LANES), F32)

    def multiplicity(dist):
        c = jnp.zeros(dist.shape, F32)
        for d in ATT_DILATIONS:
            ok = (dist >= 0) & ((dist & (d - 1)) == 0) & (dist <= ATT_SPAN * d)
            c = c + jnp.where(ok, 1.0, 0.0)
        return c

    tc = lax.broadcasted_iota(jnp.int32, (t_len, w_len), 0)
    jc = lax.broadcasted_iota(jnp.int32, (t_len, w_len), 1)
    cnt_c = multiplicity(w_len + tc - jc)
    tn = lax.broadcasted_iota(jnp.int32, (t_len, pad), 0)
    jn = lax.broadcasted_iota(jnp.int32, (t_len, pad), 1)
    cnt_n = multiplicity(jnp.where(jn < t_len, tn - jn, -1))

    lane = lax.broadcasted_iota(jnp.int32, (t_len, LANES), 1)
    head0 = lane < HEAD_DIM
    for p in range(pairs):
        lanes = slice(p * LANES, (p + 1) * LANES)
        q = q_ref[:, lanes] * (HEAD_DIM ** -0.5)
        kn = jnp.concatenate([kn_ref[:, lanes], zpad], axis=0).astype(BF16)
        vn = jnp.concatenate([vn_ref[:, lanes], zpad], axis=0).astype(BF16)
        kct = kct_ref[2 * p:2 * p + 2].reshape(LANES, w_len).astype(BF16)
        vct = vct_ref[2 * p:2 * p + 2].reshape(LANES, w_len).astype(BF16)
        outs = []
        for h in range(2):
            hm = head0 if h == 0 else jnp.logical_not(head0)
            qh = jnp.where(hm, q, 0.0).astype(BF16)
            sc = jnp.where(cnt_c > 0, _dot(qh, kct), NEG)
            sn = jnp.where(cnt_n > 0, _dot_nt(qh, kn), NEG)
            m = jnp.maximum(jnp.max(sc, axis=-1, keepdims=True), jnp.max(sn, axis=-1, keepdims=True))
            ec = cnt_c * jnp.exp(sc - m)
            en = cnt_n * jnp.exp(sn - m)
            l = jnp.sum(ec, axis=-1, keepdims=True) + jnp.sum(en, axis=-1, keepdims=True)
            pv = _dot_nt(ec.astype(BF16), vct) + _dot(en.astype(BF16), vn)
            outs.append(pv / l)
        o_ref[:, lanes] = jnp.where(head0, outs[0], outs[1])


def _attn_sample(q, kv, cache_t, layer, *, pairs=4):
    b, t, _ = q.shape
    w = cache_t.shape[-1]
    assert all(d & (d - 1) == 0 for d in ATT_DILATIONS) and t <= LANES
    width = pairs * LANES
    nstep = ATT_WIDTH // width
    vmem = 2 * 2 * _nbytes((width, w), F32) + 10 * _nbytes((LANES, w), F32) * pairs + 4 * MIB
    new_spec = lambda off: pl.BlockSpec((None, t, width), lambda i, j: (i, 0, off + j))
    buf_spec = lambda c: pl.BlockSpec((None, None, None, 2 * pairs, HEAD_DIM, w),
                                      lambda i, j: (layer, i, c, j, 0, 0))
    return pl.pallas_call(
        _attn_sample_kernel,
        out_shape=jax.ShapeDtypeStruct((b, t, ATT_WIDTH), F32),
        grid=(b, nstep),
        in_specs=[new_spec(0), new_spec(0), new_spec(nstep), buf_spec(0), buf_spec(1)],
        out_specs=new_spec(0),
        compiler_params=_params(("parallel", "parallel"), vmem),
        name="attn_sample",
    )(q, kv, kv, cache_t, cache_t)


def _ssm_prep_kernel(are_ref, aim_ref, ldt_ref, bre_ref, bim_ref, abre_ref, abim_ref, bbre_ref, bbim_ref):
    a_re, a_im = are_ref[...], aim_ref[...]
    dt = jnp.exp(ldt_ref[...])
    mag = jnp.exp(a_re * dt)
    ang = a_im * dt
    ab_re, ab_im = mag * jnp.cos(ang), mag * jnp.sin(ang)
    den = a_re * a_re + a_im * a_im
    nr, ni = ab_re - 1.0, ab_im
    cf_re = (nr * a_re + ni * a_im) / den
    cf_im = (ni * a_re - nr * a_im) / den
    abre_ref[...] = ab_re
    abim_ref[...] = ab_im
    for c in range(SSM_GROUP):
        b_re, b_im = bre_ref[c], bim_ref[c]
        bbre_ref[c] = cf_re * b_re - cf_im * b_im
        bbim_ref[c] = cf_re * b_im + cf_im * b_re


def _ssm_prep(a_re, a_im, log_dt, b_re, b_im):
    depth, g, n = a_re.shape
    rows = depth * g
    flat = lambda a: a.reshape(rows, n)
    chan_major = lambda b: jnp.transpose(b.reshape(rows, n, SSM_GROUP), (2, 0, 1))
    ab_re, ab_im, bb_re, bb_im = pl.pallas_call(
        _ssm_prep_kernel,
        out_shape=(jax.ShapeDtypeStruct((rows, n), F32), jax.ShapeDtypeStruct((rows, n), F32),
                   jax.ShapeDtypeStruct((SSM_GROUP, rows, n), F32),
                   jax.ShapeDtypeStruct((SSM_GROUP, rows, n), F32)),
        name="ssm_prep",
    )(flat(a_re), flat(a_im), log_dt.reshape(rows, 1), chan_major(b_re), chan_major(b_im))
    unflat = lambda b: jnp.transpose(b, (1, 0, 2)).reshape(depth, g, SSM_GROUP, n)
    return ab_re.reshape(depth, g, n), ab_im.reshape(depth, g, n), unflat(bb_re), unflat(bb_im)


def _block_diag(m):
    *lead, g, r, c = m.shape
    nb = g // SSM_BLOCK_GROUPS
    m = m.reshape(*lead, nb, SSM_BLOCK_GROUPS, r, c)
    eye = jnp.eye(SSM_BLOCK_GROUPS, dtype=m.dtype)
    out = m[..., :, :, None, :] * eye[:, None, :, None]
    return out.reshape(*lead, nb, SSM_BLOCK_GROUPS * r, SSM_BLOCK_GROUPS * c)


def _ssm_kernel(u_ref, h0re_ref, h0im_ref, are_ref, aim_ref, bb_ref, cre_ref, cim_ref, d_ref,
                y_ref, hre_ref, him_ref, x_scr, up_scr, *, steps, chained):
    ns = SSM_BLOCK_STATE
    nrows = steps * SCAN_SEQS
    chunk = min(nrows, 256)

    nchunks = nrows // chunk
    chunk_steps = chunk // SCAN_SEQS
    unroll = min(steps, 8)

    def permute_in(t, c):
        up_scr[pl.ds(pl.multiple_of(t * SCAN_SEQS, SCAN_SEQS), SCAN_SEQS), :] = (
            u_ref[pl.ds(t, SCAN_SEQS, stride=steps), :])
        return c
    lax.fori_loop(0, steps, permute_in, 0, unroll=unroll)

    a_re = jnp.broadcast_to(are_ref[...], (SCAN_SEQS, ns))
    a_im = jnp.broadcast_to(aim_ref[...], (SCAN_SEQS, ns))

    def input_map(c):
        rows = pl.ds(c * chunk, chunk)
        x_scr[rows, :] = _dot(up_scr[rows, :].astype(BF16), bb_ref[...])

    def scan_chunk(c, h, store):
        h_re, h_im = h
        for t in range(c * chunk_steps, (c + 1) * chunk_steps):
            rows = pl.ds(t * SCAN_SEQS, SCAN_SEQS)
            h_re, h_im = (a_re * h_re - a_im * h_im + x_scr[rows, :ns],
                          a_re * h_im + a_im * h_re + x_scr[rows, ns:])
            if store:
                x_scr[rows, :ns] = h_re
                x_scr[rows, ns:] = h_im
        return h_re, h_im

    def output_map(c):
        rows = pl.ds(c * chunk, chunk)
        y = (_dot(x_scr[rows, :ns].astype(BF16), cre_ref[...])
             - _dot(x_scr[rows, ns:].astype(BF16), cim_ref[...])
             + d_ref[...] * up_scr[rows, :])
        up_scr[rows, :] = _gelu_tanh(y)

    init = (h0re_ref[...], h0im_ref[...])
    input_map(0)
    if chained:
        assert steps & (steps - 1) == 0
        e_re, e_im = init
        for c in range(nchunks):
            if c + 1 < nchunks:
                input_map(c + 1)
            e_re, e_im = scan_chunk(c, (e_re, e_im), store=False)
        p_re, p_im = a_re, a_im
        for _ in range(steps.bit_length() - 1):
            p_re, p_im = p_re * p_re - p_im * p_im, 2.0 * p_re * p_im
        sub = lax.broadcasted_iota(jnp.int32, (SCAN_SEQS, ns), 0)
        shift = 1
        while shift < SCAN_SEQS:
            r_re = pltpu.roll(e_re, shift, axis=0)
            r_im = pltpu.roll(e_im, shift, axis=0)
            keep = sub >= shift
            e_re, e_im = (e_re + jnp.where(keep, p_re * r_re - p_im * r_im, 0.0),
                          e_im + jnp.where(keep, p_re * r_im + p_im * r_re, 0.0))
            p_re, p_im = p_re * p_re - p_im * p_im, 2.0 * p_re * p_im
            shift *= 2
        first = sub == 0
        init = (init[0] + jnp.where(first, 0.0, pltpu.roll(e_re, 1, axis=0)),
                init[1] + jnp.where(first, 0.0, pltpu.roll(e_im, 1, axis=0)))
    else:
        for c in range(1, nchunks):
            input_map(c)
    h = init
    for c in range(nchunks):
        h = scan_chunk(c, h, store=True)
        if c > 0:
            output_map(c - 1)
    output_map(nchunks - 1)
    hre_ref[...] = h[0]
    him_ref[...] = h[1]

    def permute_out(t, c):
        y_ref[pl.ds(t, SCAN_SEQS, stride=steps), :] = (
            up_scr[pl.ds(pl.multiple_of(t * SCAN_SEQS, SCAN_SEQS), SCAN_SEQS), :])
        return c
    lax.fori_loop(0, steps, permute_out, 0, unroll=unroll)


def _ssm(u, h0_re, h0_im, h0_layer, ssm_w, layer, *, chained):
    nb, rows, _ = u.shape
    steps = rows // SCAN_SEQS
    nblk = SSM_WIDTH // LANES
    ns = SSM_BLOCK_STATE
    vmem = (2 * 2 * _nbytes((rows, LANES), F32) + _nbytes((rows, 2 * ns), F32)
            + _nbytes((rows, LANES), F32) + 4 * _nbytes((min(rows, 256), 2 * ns), F32) + 8 * MIB)
    h0_spec = pl.BlockSpec((None, None, SCAN_SEQS, ns), lambda i, j: (h0_layer, i, 0, j))
    state_spec = pl.BlockSpec((None, SCAN_SEQS, ns), lambda i, j: (i, 0, j))
    return pl.pallas_call(
        functools.partial(_ssm_kernel, steps=steps, chained=chained),
        out_shape=(jax.ShapeDtypeStruct((nb, rows, SSM_WIDTH), F32),
                   jax.ShapeDtypeStruct(h0_re.shape[1:], F32),
                   jax.ShapeDtypeStruct(h0_im.shape[1:], F32)),
        grid=(nb, nblk),
        in_specs=[pl.BlockSpec((None, rows, LANES), lambda i, j: (i, 0, j)),
                  h0_spec, h0_spec,
                  pl.BlockSpec((None, 1, ns), lambda i, j: (layer, 0, j)),
                  pl.BlockSpec((None, 1, ns), lambda i, j: (layer, 0, j)),
                  pl.BlockSpec((None, None, LANES, 2 * ns), lambda i, j: (layer, j, 0, 0)),
                  pl.BlockSpec((None, None, ns, LANES), lambda i, j: (layer, j, 0, 0)),
                  pl.BlockSpec((None, None, ns, LANES), lambda i, j: (layer, j, 0, 0)),
                  pl.BlockSpec((None, 1, LANES), lambda i, j: (layer, 0, j))],
        out_specs=(pl.BlockSpec((None, rows, LANES), lambda i, j: (i, 0, j)), state_spec, state_spec),
        scratch_shapes=[pltpu.VMEM((rows, 2 * ns), F32), pltpu.VMEM((rows, LANES), F32)],
        compiler_params=_params(("parallel", "parallel"), vmem,
                                _nbytes((nb, rows, SSM_WIDTH), F32)),
        name="ssm_chained" if chained else "ssm_step",
    )(u, h0_re, h0_im, *ssm_w)


def _conv_kernel(h_ref, gb_ref, gc_ref, w_ref, pre_ref, y_ref, st_ref):
    n = h_ref.shape[0]
    u = gc_ref[...] * h_ref[...]
    row = lax.broadcasted_iota(jnp.int32, u.shape, 0)
    pre = pre_ref[...]
    p0, p1 = pre[0:1, :], pre[1:2, :]
    u1 = jnp.where(row == 0, p1, pltpu.roll(u, 1, axis=0))
    u2 = jnp.where(row == 0, p0, jnp.where(row == 1, p1, pltpu.roll(u, 2, axis=0)))
    w = w_ref[...]
    y = w[0:1, :] * u2 + w[1:2, :] * u1 + w[2:3, :] * u
    y_ref[...] = gb_ref[...] * y
    st_ref[...] = u[n - (CONV_K - 1):, :]


def _conv(rest, w, layer, prefix, prefix_layer):
    b, n, _ = rest.shape
    assert n >= CONV_K - 1
    nblk = CONV_CH // LANES
    off = SSM_WIDTH // LANES
    col = lambda part: pl.BlockSpec((None, n, LANES), lambda i, j: (i, 0, off + part * nblk + j))
    vmem = 2 * 4 * _nbytes((n, LANES), F32) + 8 * _nbytes((n, LANES), F32) + 4 * MIB
    return pl.pallas_call(
        _conv_kernel,
        out_shape=(jax.ShapeDtypeStruct((b, n, CONV_CH), F32),
                   jax.ShapeDtypeStruct((b, CONV_K - 1, CONV_CH), F32)),
        grid=(b, nblk),
        in_specs=[col(0), col(1), col(2),
                  pl.BlockSpec((None, CONV_K, LANES), lambda i, j: (layer, 0, j)),
                  pl.BlockSpec((None, None, CONV_K - 1, LANES), lambda i, j: (prefix_layer, i, 0, j))],
        out_specs=(pl.BlockSpec((None, n, LANES), lambda i, j: (i, 0, j)),
                   pl.BlockSpec((None, CONV_K - 1, LANES), lambda i, j: (i, 0, j))),
        compiler_params=_params(("parallel", "parallel"), vmem,
                                _nbytes((b, n, CONV_CH), F32)),
        name="conv",
    )(rest, rest, rest, w, prefix)


def _out_proj_kernel(oatt_ref, yssm_ref, yconv_ref, x_ref, oatts_ref, yssms_ref, yconvs_ref, xs_ref,
                     g_ref, wglu_ref, bglu_ref, w_ref, o_ref, os_ref, mix_ref, *, tm, tn):
    i, j = pl.program_id(0), pl.program_id(1)
    a0, a1 = ATT_WIDTH, ATT_WIDTH + SSM_WIDTH

    def mixed(rows, o_att, y_ssm, y_conv):
        g = g_ref[...]
        mix_ref[rows, :a0] = _rms(o_att, g[:, :a0]).astype(BF16)
        y_ssm = y_ssm * _sigmoid(_dot(y_ssm.astype(BF16), wglu_ref[...].astype(BF16)) + bglu_ref[...])
        mix_ref[rows, a0:a1] = _rms(y_ssm, g[:, a0:a1]).astype(BF16)
        mix_ref[rows, a1:] = _rms(y_conv, g[:, a1:]).astype(BF16)

    @pl.when(j == 0)
    def _():
        mixed(slice(0, tm), oatt_ref[...], yssm_ref[...], yconv_ref[...])

    @pl.when((j == 0) & (i == 0))
    def _():
        mixed(slice(tm, None), oatts_ref[...], yssms_ref[...], yconvs_ref[...])

    @pl.when(i == 0)
    def _():
        z = _dot(mix_ref[...], w_ref[...].astype(BF16))
        o_ref[...] = x_ref[...] + z[:tm]
        cols = pl.ds(pl.multiple_of(j * tn, tn), tn)
        os_ref[:, cols] = xs_ref[:, cols] + z[tm:]

    @pl.when(i > 0)
    def _():
        o_ref[...] = x_ref[...] + _dot(mix_ref[:tm], w_ref[...].astype(BF16))


def _out_proj(prompt, sample, g, w_glu, b_glu, w, layer, *, tm, tn=512):
    m, d = prompt[3].shape
    ms = sample[3].shape[0]
    k = w.shape[1]
    vmem = (2 * (_nbytes((tm, k), F32) + 2 * _nbytes((tm, tn), F32) + 3 * _nbytes((ms, k), F32)
                 + _nbytes((k, tn), F32) + _nbytes((SSM_WIDTH, SSM_WIDTH), F32))
            + _nbytes((tm + ms, k), BF16) + _nbytes((k, tn), BF16) + 2 * _nbytes((tm + ms, k), F32)
            + 2 * MIB)
    row = lambda c: pl.BlockSpec((tm, c), lambda i, j: (i, 0))
    whole = lambda c: pl.BlockSpec((ms, c), lambda i, j: (0, 0))
    layered = lambda r, c: pl.BlockSpec((None, r, c), lambda i, j: (layer, 0, 0))
    return pl.pallas_call(
        functools.partial(_out_proj_kernel, tm=tm, tn=tn),
        out_shape=(jax.ShapeDtypeStruct((m, d), F32), jax.ShapeDtypeStruct((ms, d), F32)),
        grid=(m // tm, d // tn),
        in_specs=[row(ATT_WIDTH), row(SSM_WIDTH), row(CONV_CH),
                  pl.BlockSpec((tm, tn), lambda i, j: (i, j)),
                  whole(ATT_WIDTH), whole(SSM_WIDTH), whole(CONV_CH), whole(d),
                  layered(1, k),
                  layered(SSM_WIDTH, SSM_WIDTH),
                  layered(1, SSM_WIDTH),
                  pl.BlockSpec((None, k, tn), lambda i, j: (layer, 0, j))],
        out_specs=(pl.BlockSpec((tm, tn), lambda i, j: (i, j)), whole(d)),
        scratch_shapes=[pltpu.VMEM((tm + ms, k), BF16)],
        compiler_params=_params(("arbitrary", "arbitrary"), vmem),
        name="out_proj",
    )(*prompt, *sample, g, w_glu, b_glu, w)


def _ffn_kernel(x_ref, xs_ref, g_ref, gfin_ref, wg_ref, wu_ref, wd_ref, o_ref, os_ref, xn_ref, *,
                tm, final):
    i, f = pl.program_id(0), pl.program_id(1)
    last = f == pl.num_programs(1) - 1

    @pl.when(f == 0)
    def _():
        x = x_ref[...]
        xn_ref[:tm] = _rms(x, g_ref[...]).astype(BF16)
        o_ref[...] = x

    @pl.when((f == 0) & (i == 0))
    def _():
        xs = xs_ref[...]
        xn_ref[tm:] = _rms(xs, g_ref[...]).astype(BF16)
        os_ref[...] = xs

    def swiglu(xn):
        gate = _dot(xn, wg_ref[...].astype(BF16))
        up = _dot(xn, wu_ref[...].astype(BF16))
        return _dot((gate * _sigmoid(gate) * up).astype(BF16), wd_ref[...].astype(BF16))

    @pl.when(i == 0)
    def _():
        y = swiglu(xn_ref[...])
        o_ref[...] += y[:tm]
        os_ref[...] += y[tm:]

    @pl.when(i > 0)
    def _():
        o_ref[...] += swiglu(xn_ref[:tm])

    if final:
        @pl.when(last)
        def _():
            o_ref[...] = _rms(o_ref[...], gfin_ref[...])

        @pl.when(last & (i == 0))
        def _():
            os_ref[...] = _rms(os_ref[...], gfin_ref[...])


def _ffn(x, xs, g, w_gate, w_up, w_down, layer, *, tm, g_final=None, tf=256):
    final = g_final is not None
    m, d = x.shape
    g_last = g_final.reshape(1, d) if final else g[layer]
    ms = xs.shape[0]
    ff = w_gate.shape[2]
    vmem = (2 * (2 * _nbytes((tm, d), F32) + 2 * _nbytes((ms, d), F32) + 3 * _nbytes((d, tf), F32))
            + _nbytes((tm + ms, d), BF16) + 3 * _nbytes((d, tf), BF16) + 4 * _nbytes((tm + ms, tf), F32)
            + 2 * MIB)
    return pl.pallas_call(
        functools.partial(_ffn_kernel, tm=tm, final=final),
        out_shape=(jax.ShapeDtypeStruct((m, d), F32), jax.ShapeDtypeStruct((ms, d), F32)),
        grid=(m // tm, ff // tf),
        in_specs=[pl.BlockSpec((tm, d), lambda i, f: (i, 0)),
                  pl.BlockSpec((ms, d), lambda i, f: (0, 0)),
                  pl.BlockSpec((None, 1, d), lambda i, f: (layer, 0, 0)),
                  pl.BlockSpec((1, d), lambda i, f: (0, 0)),
                  pl.BlockSpec((None, d, tf), lambda i, f: (layer, 0, f)),
                  pl.BlockSpec((None, d, tf), lambda i, f: (layer, 0, f)),
                  pl.BlockSpec((None, tf, d), lambda i, f: (layer, f, 0))],
        out_specs=(pl.BlockSpec((tm, d), lambda i, f: (i, 0)),
                   pl.BlockSpec((ms, d), lambda i, f: (0, 0))),
        scratch_shapes=[pltpu.VMEM((tm + ms, d), BF16)],
        compiler_params=_params(("arbitrary", "arbitrary"), vmem),
        name="ffn",
    )(x, xs, g, g_last, w_gate, w_up, w_down)


def _mixers(q, kv, rest, ssm_w, conv_w, layer, *, batch, cache_t, conv_prefix, ssm_h0, state_layer):
    m = q.shape[0]
    n = m // batch
    q3, kv3, rest3 = (a.reshape(batch, n, a.shape[-1]) for a in (q, kv, rest))
    if cache_t is None:
        o_att = _attn_prompt(q3, kv3)
        y_ssm, h_re, h_im = _ssm(rest3, *ssm_h0, state_layer, ssm_w, layer, chained=True)
        h_re, h_im = h_re[:, SCAN_SEQS - 1], h_im[:, SCAN_SEQS - 1]
    else:
        o_att = _attn_sample(q3, kv3, cache_t, layer)
        y_ssm, h_re, h_im = _ssm(rest.reshape(1, m, rest.shape[-1]), *ssm_h0, state_layer, ssm_w, layer,
                                 chained=False)
        h_re, h_im = h_re[0], h_im[0]
    y_conv, conv_state = _conv(rest3, conv_w, layer, conv_prefix, state_layer)
    mixed = (o_att.reshape(m, ATT_WIDTH), y_ssm.reshape(m, SSM_WIDTH), y_conv.reshape(m, CONV_CH))
    return mixed, conv_state, h_re, h_im


def kernel(x_prompt, x_sample, cache_kv, state_conv, state_ssm_re, state_ssm_im, g_mix_norm, w_in,
           ssm_a_re, ssm_a_im, ssm_log_dt, ssm_b_re, ssm_b_im, ssm_c_re, ssm_c_im, ssm_d, ssm_w_glu,
           ssm_b_glu, conv_w, g_out, w_out, g_ffn_norm, w_gate, w_up, w_down, g_final):
    batch, seq, d_model = x_prompt.shape
    dec_batch, dec_seq, _ = x_sample.shape
    depth, n_groups, n_state = ssm_a_re.shape
    n_heads = ATT_WIDTH // HEAD_DIM
    assert dec_batch == SCAN_SEQS and seq % SCAN_SEQS == 0

    ab_re, ab_im, bb_re, bb_im = _ssm_prep(ssm_a_re, ssm_a_im, ssm_log_dt, ssm_b_re, ssm_b_im)
    cache_t = jnp.transpose(cache_kv, (0, 1, 3, 4, 5, 2))

    row = lambda a: a.reshape(depth, 1, -1)
    ssm_w = (row(ab_re), row(ab_im),
             jnp.concatenate([_block_diag(bb_re), _block_diag(bb_im)], axis=-1).astype(BF16),
             _block_diag(jnp.swapaxes(ssm_c_re, -1, -2)).astype(BF16),
             _block_diag(jnp.swapaxes(ssm_c_im, -1, -2)).astype(BF16),
             row(ssm_d))
    g_mix, g_mixed, g_ffn, b_glu = row(g_mix_norm), row(g_out), row(g_ffn_norm), row(ssm_b_glu)
    state_ssm = (state_ssm_re.reshape(depth, 1, dec_batch, -1), state_ssm_im.reshape(depth, 1, dec_batch, -1))

    xp = x_prompt.reshape(batch * seq, d_model)
    xs = x_sample.reshape(dec_batch * dec_seq, d_model)
    zero_state = jnp.zeros((1, batch, SCAN_SEQS, n_groups * n_state), F32)
    zero_prefix = jnp.zeros((1, batch, CONV_K - 1, CONV_CH), F32)
    kvt = jnp.zeros((depth, batch, 2 * ATT_WIDTH, seq), F32)

    outs_p, outs_s = [], []
    for i in range(depth):
        q, kv, rest, kvt, q_s, kv_s, rest_s = _in_proj(xp, xs, g_mix, w_in, kvt, i, tm=2048, seq=seq, tn=256)

        mixed_p, c_state, h_re, h_im = _mixers(
            q, kv, rest, ssm_w, conv_w, i, batch=batch, cache_t=None, conv_prefix=zero_prefix,
            ssm_h0=(zero_state, zero_state), state_layer=0)
        outs_p.append((c_state, h_re.reshape(batch, n_groups, n_state),
                       h_im.reshape(batch, n_groups, n_state)))

        mixed_s, c_state, h_re, h_im = _mixers(
            q_s, kv_s, rest_s, ssm_w, conv_w, i, batch=dec_batch, cache_t=cache_t, conv_prefix=state_conv,
            ssm_h0=state_ssm, state_layer=i)
        outs_s.append((kv_s.reshape(dec_batch, dec_seq, 2, n_heads, HEAD_DIM), c_state,
                       h_re.reshape(dec_batch, n_groups, n_state),
                       h_im.reshape(dec_batch, n_groups, n_state)))

        xp, xs = _out_proj((*mixed_p, xp), (*mixed_s, xs), g_mixed, ssm_w_glu, b_glu, w_out, i, tm=1024)
        xp, xs = _ffn(xp, xs, g_ffn, w_gate, w_up, w_down, i, tm=1024,
                      g_final=g_final if i == depth - 1 else None)

    y_prompt = xp.reshape(batch, seq, d_model)
    y_sample = xs.reshape(dec_batch, dec_seq, d_model)
    kv_prompt = jnp.transpose(kvt.reshape(depth, batch, 2, n_heads, HEAD_DIM, seq), (0, 1, 5, 2, 3, 4))
    n_keep = min(max(ATT_DILATIONS) * ATT_SPAN, seq)
    stack = lambda outs, k: jnp.stack([o[k] for o in outs])
    return (y_prompt, y_sample,
            kv_prompt[:, :, seq - n_keep:], stack(outs_p, 0), stack(outs_p, 1), stack(outs_p, 2),
            stack(outs_s, 0), stack(outs_s, 1), stack(outs_s, 2), stack(outs_s, 3))
```

```python
import functools
import math

import jax
import jax.numpy as jnp
from jax import lax
from jax.experimental import pallas as pl
from jax.experimental.pallas import tpu as pltpu

F32 = jnp.float32
BF16 = jnp.bfloat16

HEAD_DIM = 64
ATT_WIDTH = 1024
SSM_WIDTH = 512
CONV_CH = 512
CONV_K = 3
SSM_GROUP = 16
SSM_STATE = 64
ATT_DILATIONS = (1, 4, 16)
ATT_SPAN = 128
RMS_EPS = 1e-6
NEG = -1e30

LANES = 128
SUBLANES = 8
SCAN_SEQS = SUBLANES
SSM_BLOCK_GROUPS = LANES // SSM_GROUP
SSM_BLOCK_STATE = SSM_BLOCK_GROUPS * SSM_STATE

MIB = 1024 * 1024
V7X_VMEM_BYTES = 64 * MIB


def _params(semantics, vmem_bytes, hbm_result_bytes=None):
    if hbm_result_bytes is not None and hbm_result_bytes >= V7X_VMEM_BYTES // 8:
        vmem_bytes = max(vmem_bytes, V7X_VMEM_BYTES - hbm_result_bytes + MIB)
    return pltpu.CompilerParams(dimension_semantics=semantics, vmem_limit_bytes=int(vmem_bytes))


def _nbytes(shape, dtype):
    return math.prod(shape) * jnp.dtype(dtype).itemsize


def _rms(x, g):
    ms = jnp.mean(x * x, axis=-1, keepdims=True)
    return x * lax.rsqrt(ms + RMS_EPS) * g


def _sigmoid(x):
    return 1.0 / (1.0 + jnp.exp(-x))


def _gelu_tanh(x):
    c = math.sqrt(2.0 / math.pi)
    return 0.5 * x * (1.0 + jnp.tanh(c * (x + 0.044715 * (x * x * x))))


def _dot(a, b):
    return jnp.dot(a, b, preferred_element_type=F32)


def _dot_nt(a, b):
    return lax.dot_general(a, b, (((1,), (1,)), ((), ())), preferred_element_type=F32)


def _in_proj_kernel(x_ref, xs_ref, g_ref, w_ref, kvt_in_ref, q_ref, kv_ref, rest_ref, kvt_ref,
                    qs_ref, kvs_ref, rests_ref, xn_ref, *, tm, tn, nq, nkv):
    del kvt_in_ref
    i, j = pl.program_id(0), pl.program_id(1)

    @pl.when(j == 0)
    def _():
        xn_ref[:tm] = _rms(x_ref[...], g_ref[...]).astype(BF16)

    @pl.when((j == 0) & (i == 0))
    def _():
        xn_ref[tm:] = _rms(xs_ref[...], g_ref[...]).astype(BF16)

    def project(out_ref, outs_ref, col, transposed_ref=None):
        @pl.when(i == 0)
        def _():
            z = _dot(xn_ref[...], w_ref[...].astype(BF16))
            out_ref[...] = z[:tm]
            outs_ref[:, pl.ds(pl.multiple_of(col * tn, tn), tn)] = z[tm:]
            if transposed_ref is not None:
                transposed_ref[...] = z[:tm].T

        @pl.when(i > 0)
        def _():
            z = _dot(xn_ref[:tm], w_ref[...].astype(BF16))
            out_ref[...] = z
            if transposed_ref is not None:
                transposed_ref[...] = z.T

    @pl.when(j < nq)
    def _():
        project(q_ref, qs_ref, j)

    @pl.when((j >= nq) & (j < nq + nkv))
    def _():
        project(kv_ref, kvs_ref, j - nq, kvt_ref)

    @pl.when(j >= nq + nkv)
    def _():
        project(rest_ref, rests_ref, j - nq - nkv)


def _in_proj(x, xs, g, w, kvt_stack, layer, *, tm, seq, tn=512):
    m, k = x.shape
    ms = xs.shape[0]
    n = w.shape[2]
    nq, nkv = ATT_WIDTH // tn, 2 * ATT_WIDTH // tn
    nrest = n // tn - nq - nkv
    assert seq % tm == 0 and tm % LANES == 0
    assert kvt_stack.shape[1:] == (m // seq, nkv * tn, seq)
    tiles_per_seq = seq // tm
    vmem = (_nbytes((tm, k), F32)
            + 2 * (_nbytes((ms, k), F32) + _nbytes((k, tn), F32) + 4 * _nbytes((tm, tn), F32)
                   + _nbytes((ms, n), F32))
            + _nbytes((tm + ms, k), BF16) + _nbytes((k, tn), BF16) + 3 * _nbytes((tm + ms, tn), F32)
            + 2 * MIB)
    kv_col = lambda j: jnp.clip(j - nq, 0, nkv - 1)
    whole = lambda cols: pl.BlockSpec((ms, cols), lambda i, j: (0, 0))
    return pl.pallas_call(
        functools.partial(_in_proj_kernel, tm=tm, tn=tn, nq=nq, nkv=nkv),
        out_shape=(jax.ShapeDtypeStruct((m, nq * tn), F32),
                   jax.ShapeDtypeStruct((m, nkv * tn), F32),
                   jax.ShapeDtypeStruct((m, nrest * tn), F32),
                   jax.ShapeDtypeStruct(kvt_stack.shape, F32),
                   jax.ShapeDtypeStruct((ms, nq * tn), F32),
                   jax.ShapeDtypeStruct((ms, nkv * tn), F32),
                   jax.ShapeDtypeStruct((ms, nrest * tn), F32)),
        grid=(m // tm, n // tn),
        in_specs=[pl.BlockSpec((tm, k), lambda i, j: (i, 0), pipeline_mode=pl.Buffered(1)),
                  whole(k),
                  pl.BlockSpec((None, 1, k), lambda i, j: (layer, 0, 0)),
                  pl.BlockSpec((None, k, tn), lambda i, j: (layer, 0, j)),
                  pl.BlockSpec(memory_space=pl.ANY)],
        out_specs=(pl.BlockSpec((tm, tn), lambda i, j: (i, jnp.minimum(j, nq - 1))),
                   pl.BlockSpec((tm, tn), lambda i, j: (i, kv_col(j))),
                   pl.BlockSpec((tm, tn), lambda i, j: (i, jnp.clip(j - nq - nkv, 0, nrest - 1))),
                   pl.BlockSpec((None, None, tn, tm),
                                lambda i, j: (layer, i // tiles_per_seq, kv_col(j), i % tiles_per_seq)),
                   whole(nq * tn), whole(nkv * tn), whole(nrest * tn)),
        scratch_shapes=[pltpu.VMEM((tm + ms, k), BF16)],
        input_output_aliases={4: 3},
        compiler_params=_params(("arbitrary", "arbitrary"), vmem),
        name="in_proj",
    )(x, xs, g, w, kvt_stack)


def _attn_prompt_kernel(q_ref, k_ref, v_ref, o_ref, ob_ref, mb_ref, lb_ref, bias_ref):
    s_len = q_ref.shape[0]
    blk = ATT_SPAN
    lane = lax.broadcasted_iota(jnp.int32, (blk, LANES), 1)
    head0 = lane < HEAD_DIM
    scale = HEAD_DIM ** -0.5

    qi = lax.broadcasted_iota(jnp.int32, (2 * blk, 3 * blk), 0) & (blk - 1)
    kj = lax.broadcasted_iota(jnp.int32, (2 * blk, 3 * blk), 1)
    dist = jnp.where(kj < blk, qi - kj, qi + 2 * blk - kj)
    bias_ref[...] = jnp.where((dist >= 0) & (dist <= ATT_SPAN), 0.0, NEG)

    def block(p, d, q0, k0=None):
        def rows(start, n):
            return pl.ds(start, n, stride=d) if d > 1 else pl.ds(start, n)

        qt = q_ref[rows(q0, blk), :] * scale
        qq = jnp.concatenate([jnp.where(head0, qt, 0.0), jnp.where(head0, 0.0, qt)], axis=0).astype(BF16)
        if k0 is None:
            key_rows, bias = rows(q0, blk), bias_ref[:, :blk]
        else:
            key_rows, bias = rows(k0, 2 * blk), bias_ref[:, blk:]
        k = k_ref[key_rows, :].astype(BF16)
        v = v_ref[key_rows, :].astype(BF16)
        s = _dot_nt(qq, k) + bias
        m = jnp.max(s, axis=-1, keepdims=True)
        e = jnp.exp(s - m)
        l = jnp.sum(e, axis=-1, keepdims=True)
        o = _dot(e.astype(BF16), v)
        m = jnp.broadcast_to(m, (2 * blk, LANES))
        l = jnp.broadcast_to(l, (2 * blk, LANES))
        ob_ref[p, rows(q0, blk), :] = jnp.where(head0, o[:blk], o[blk:])
        mb_ref[p, rows(q0, blk), :] = jnp.where(head0, m[:blk], m[blk:])
        lb_ref[p, rows(q0, blk), :] = jnp.where(head0, l[:blk], l[blk:])

    for p, d in enumerate(ATT_DILATIONS):
        nblk = s_len // (d * blk)
        for r in range(d):
            for b in range(nblk):
                block(p, d, r + d * blk * b, None if b == 0 else r + d * blk * (b - 1))

    def merge(c, carry):
        rows = pl.ds(pl.multiple_of(c * blk, blk), blk)
        m0, m1, m2 = mb_ref[0, rows, :], mb_ref[1, rows, :], mb_ref[2, rows, :]
        m = jnp.maximum(jnp.maximum(m0, m1), m2)
        e0, e1, e2 = jnp.exp(m0 - m), jnp.exp(m1 - m), jnp.exp(m2 - m)
        num = e0 * ob_ref[0, rows, :] + e1 * ob_ref[1, rows, :] + e2 * ob_ref[2, rows, :]
        den = e0 * lb_ref[0, rows, :] + e1 * lb_ref[1, rows, :] + e2 * lb_ref[2, rows, :]
        o_ref[rows, :] = num / den
        return carry

    lax.fori_loop(0, s_len // blk, merge, 0)


def _attn_prompt(q, kv):
    b, s, _ = q.shape
    assert s % (max(ATT_DILATIONS) * ATT_SPAN) == 0
    npair = ATT_WIDTH // LANES
    blk_bytes = _nbytes((s, LANES), F32)
    vmem = 2 * 4 * blk_bytes + 3 * len(ATT_DILATIONS) * blk_bytes + 4 * MIB
    return pl.pallas_call(
        _attn_prompt_kernel,
        out_shape=jax.ShapeDtypeStruct((b, s, ATT_WIDTH), F32),
        grid=(b, npair),
        in_specs=[pl.BlockSpec((None, s, LANES), lambda i, j: (i, 0, j)),
                  pl.BlockSpec((None, s, LANES), lambda i, j: (i, 0, j)),
                  pl.BlockSpec((None, s, LANES), lambda i, j: (i, 0, npair + j))],
        out_specs=pl.BlockSpec((None, s, LANES), lambda i, j: (i, 0, j)),
        scratch_shapes=[pltpu.VMEM((len(ATT_DILATIONS), s, LANES), F32),
                        pltpu.VMEM((len(ATT_DILATIONS), s, LANES), F32),
                        pltpu.VMEM((len(ATT_DILATIONS), s, LANES), F32),
                        pltpu.VMEM((2 * ATT_SPAN, 3 * ATT_SPAN), F32)],
        compiler_params=_params(("parallel", "parallel"), vmem, _nbytes((b, s, ATT_WIDTH), F32)),
        name="attn_prompt",
    )(q, kv, kv)


def _attn_sample_kernel(q_ref, kn_ref, vn_ref, kct_ref, vct_ref, o_ref):
    t_len = q_ref.shape[0]
    w_len = kct_ref.shape[-1]
    pairs = q_ref.shape[1] // LANES
    pad = LANES
    zpad = jnp.zeros((pad - t_len, LANES), F32)

    def multiplicity(dist):
        c = jnp.zeros(dist.shape, F32)
        for d in ATT_DILATIONS:
            ok = (dist >= 0) & ((dist & (d - 1)) == 0) & (dist <= ATT_SPAN * d)
            c = c + jnp.where(ok, 1.0, 0.0)
        return c

    tc = lax.broadcasted_iota(jnp.int32, (t_len, w_len), 0)
    jc = lax.broadcasted_iota(jnp.int32, (t_len, w_len), 1)
    cnt_c = multiplicity(w_len + tc - jc)
    tn = lax.broadcasted_iota(jnp.int32, (t_len, pad), 0)
    jn = lax.broadcasted_iota(jnp.int32, (t_len, pad), 1)
    cnt_n = multiplicity(jnp.where(jn < t_len, tn - jn, -1))

    lane = lax.broadcasted_iota(jnp.int32, (t_len, LANES), 1)
    head0 = lane < HEAD_DIM
    for p in range(pairs):
        lanes = slice(p * LANES, (p + 1) * LANES)
        q = q_ref[:, lanes] * (HEAD_DIM ** -0.5)
        kn = jnp.concatenate([kn_ref[:, lanes], zpad], axis=0).astype(BF16)
        vn = jnp.concatenate([vn_ref[:, lanes], zpad], axis=0).astype(BF16)
        kct = kct_ref[2 * p:2 * p + 2].reshape(LANES, w_len).astype(BF16)
        vct = vct_ref[2 * p:2 * p + 2].reshape(LANES, w_len).astype(BF16)
        outs = []
        for h in range(2):
            hm = head0 if h == 0 else jnp.logical_not(head0)
            qh = jnp.where(hm, q, 0.0).astype(BF16)
            sc = jnp.where(cnt_c > 0, _dot(qh, kct), NEG)
            sn = jnp.where(cnt_n > 0, _dot_nt(qh, kn), NEG)
            m = jnp.maximum(jnp.max(sc, axis=-1, keepdims=True), jnp.max(sn, axis=-1, keepdims=True))
            ec = cnt_c * jnp.exp(sc - m)
            en = cnt_n * jnp.exp(sn - m)
            l = jnp.sum(ec, axis=-1, keepdims=True) + jnp.sum(en, axis=-1, keepdims=True)
            pv = _dot_nt(ec.astype(BF16), vct) + _dot(en.astype(BF16), vn)
            outs.append(pv / l)
        o_ref[:, lanes] = jnp.where(head0, outs[0], outs[1])


def _attn_sample(q, kv, cache_t, layer, *, pairs=4):
    b, t, _ = q.shape
    w = cache_t.shape[-1]
    assert all(d & (d - 1) == 0 for d in ATT_DILATIONS) and t <= LANES
    width = pairs * LANES
    nstep = ATT_WIDTH // width
    vmem = 2 * 2 * _nbytes((width, w), F32) + 10 * _nbytes((LANES, w), F32) * pairs + 4 * MIB
    new_spec = lambda off: pl.BlockSpec((None, t, width), lambda i, j: (i, 0, off + j))
    buf_spec = lambda c: pl.BlockSpec((None, None, None, 2 * pairs, HEAD_DIM, w),
                                      lambda i, j: (layer, i, c, j, 0, 0))
    return pl.pallas_call(
        _attn_sample_kernel,
        out_shape=jax.ShapeDtypeStruct((b, t, ATT_WIDTH), F32),
        grid=(b, nstep),
        in_specs=[new_spec(0), new_spec(0), new_spec(nstep), buf_spec(0), buf_spec(1)],
        out_specs=new_spec(0),
        compiler_params=_params(("parallel", "parallel"), vmem),
        name="attn_sample",
    )(q, kv, kv, cache_t, cache_t)


def _ssm_prep_kernel(are_ref, aim_ref, ldt_ref, bre_ref, bim_ref, abre_ref, abim_ref, bbre_ref, bbim_ref):
    a_re, a_im = are_ref[...], aim_ref[...]
    dt = jnp.exp(ldt_ref[...])
    mag = jnp.exp(a_re * dt)
    ang = a_im * dt
    ab_re, ab_im = mag * jnp.cos(ang), mag * jnp.sin(ang)
    den = a_re * a_re + a_im * a_im
    nr, ni = ab_re - 1.0, ab_im
    cf_re = (nr * a_re + ni * a_im) / den
    cf_im = (ni * a_re - nr * a_im) / den
    abre_ref[...] = ab_re
    abim_ref[...] = ab_im
    for c in range(SSM_GROUP):
        b_re, b_im = bre_ref[c], bim_ref[c]
        bbre_ref[c] = cf_re * b_re - cf_im * b_im
        bbim_ref[c] = cf_re * b_im + cf_im * b_re


def _ssm_prep(a_re, a_im, log_dt, b_re, b_im):
    depth, g, n = a_re.shape
    rows = depth * g
    flat = lambda a: a.reshape(rows, n)
    chan_major = lambda b: jnp.transpose(b.reshape(rows, n, SSM_GROUP), (2, 0, 1))
    ab_re, ab_im, bb_re, bb_im = pl.pallas_call(
        _ssm_prep_kernel,
        out_shape=(jax.ShapeDtypeStruct((rows, n), F32), jax.ShapeDtypeStruct((rows, n), F32),
                   jax.ShapeDtypeStruct((SSM_GROUP, rows, n), F32),
                   jax.ShapeDtypeStruct((SSM_GROUP, rows, n), F32)),
        name="ssm_prep",
    )(flat(a_re), flat(a_im), log_dt.reshape(rows, 1), chan_major(b_re), chan_major(b_im))
    unflat = lambda b: jnp.transpose(b, (1, 0, 2)).reshape(depth, g, SSM_GROUP, n)
    return ab_re.reshape(depth, g, n), ab_im.reshape(depth, g, n), unflat(bb_re), unflat(bb_im)


def _block_diag(m):
    *lead, g, r, c = m.shape
    nb = g // SSM_BLOCK_GROUPS
    m = m.reshape(*lead, nb, SSM_BLOCK_GROUPS, r, c)
    eye = jnp.eye(SSM_BLOCK_GROUPS, dtype=m.dtype)
    out = m[..., :, :, None, :] * eye[:, None, :, None]
    return out.reshape(*lead, nb, SSM_BLOCK_GROUPS * r, SSM_BLOCK_GROUPS * c)


def _ssm_kernel(u_ref, h0re_ref, h0im_ref, are_ref, aim_ref, bb_ref, cre_ref, cim_ref, d_ref,
                y_ref, hre_ref, him_ref, x_scr, up_scr, *, steps, chained):
    ns = SSM_BLOCK_STATE
    nrows = steps * SCAN_SEQS
    chunk = min(nrows, 256)

    nchunks = nrows // chunk
    chunk_steps = chunk // SCAN_SEQS
    unroll = min(steps, 8)

    def permute_in(t, c):
        up_scr[pl.ds(pl.multiple_of(t * SCAN_SEQS, SCAN_SEQS), SCAN_SEQS), :] = (
            u_ref[pl.ds(t, SCAN_SEQS, stride=steps), :])
        return c
    lax.fori_loop(0, steps, permute_in, 0, unroll=unroll)

    a_re = jnp.broadcast_to(are_ref[...], (SCAN_SEQS, ns))
    a_im = jnp.broadcast_to(aim_ref[...], (SCAN_SEQS, ns))

    def input_map(c):
        rows = pl.ds(c * chunk, chunk)
        x_scr[rows, :] = _dot(up_scr[rows, :].astype(BF16), bb_ref[...])

    def scan_chunk(c, h, store):
        h_re, h_im = h
        for t in range(c * chunk_steps, (c + 1) * chunk_steps):
            rows = pl.ds(t * SCAN_SEQS, SCAN_SEQS)
            h_re, h_im = (a_re * h_re - a_im * h_im + x_scr[rows, :ns],
                          a_re * h_im + a_im * h_re + x_scr[rows, ns:])
            if store:
                x_scr[rows, :ns] = h_re
                x_scr[rows, ns:] = h_im
        return h_re, h_im

    def output_map(c):
        rows = pl.ds(c * chunk, chunk)
        y = (_dot(x_scr[rows, :ns].astype(BF16), cre_ref[...])
             - _dot(x_scr[rows, ns:].astype(BF16), cim_ref[...])
             + d_ref[...] * up_scr[rows, :])
        up_scr[rows, :] = _gelu_tanh(y)

    init = (h0re_ref[...], h0im_ref[...])
    input_map(0)
    if chained:
        assert steps & (steps - 1) == 0
        e_re, e_im = init
        for c in range(nchunks):
            if c + 1 < nchunks:
                input_map(c + 1)
            e_re, e_im = scan_chunk(c, (e_re, e_im), store=False)
        p_re, p_im = a_re, a_im
        for _ in range(steps.bit_length() - 1):
            p_re, p_im = p_re * p_re - p_im * p_im, 2.0 * p_re * p_im
        sub = lax.broadcasted_iota(jnp.int32, (SCAN_SEQS, ns), 0)
        shift = 1
        while shift < SCAN_SEQS:
            r_re = pltpu.roll(e_re, shift, axis=0)
            r_im = pltpu.roll(e_im, shift, axis=0)
            keep = sub >= shift
            e_re, e_im = (e_re + jnp.where(keep, p_re * r_re - p_im * r_im, 0.0),
                          e_im + jnp.where(keep, p_re * r_im + p_im * r_re, 0.0))
            p_re, p_im = p_re * p_re - p_im * p_im, 2.0 * p_re * p_im
            shift *= 2
        first = sub == 0
        init = (init[0] + jnp.where(first, 0.0, pltpu.roll(e_re, 1, axis=0)),
                init[1] + jnp.where(first, 0.0, pltpu.roll(e_im, 1, axis=0)))
    else:
        for c in range(1, nchunks):
            input_map(c)
    h = init
    for c in range(nchunks):
        h = scan_chunk(c, h, store=True)
        if c > 0:
            output_map(c - 1)
    output_map(nchunks - 1)
    hre_ref[...] = h[0]
    him_ref[...] = h[1]

    def permute_out(t, c):
        y_ref[pl.ds(t, SCAN_SEQS, stride=steps), :] = (
            up_scr[pl.ds(pl.multiple_of(t * SCAN_SEQS, SCAN_SEQS), SCAN_SEQS), :])
        return c
    lax.fori_loop(0, steps, permute_out, 0, unroll=unroll)


def _ssm(u, h0_re, h0_im, h0_layer, ssm_w, layer, *, chained):
    nb, rows, _ = u.shape
    steps = rows // SCAN_SEQS
    nblk = SSM_WIDTH // LANES
    ns = SSM_BLOCK_STATE
    vmem = (2 * 2 * _nbytes((rows, LANES), F32) + _nbytes((rows, 2 * ns), F32)
            + _nbytes((rows, LANES), F32) + 4 * _nbytes((min(rows, 256), 2 * ns), F32) + 8 * MIB)
    h0_spec = pl.BlockSpec((None, None, SCAN_SEQS, ns), lambda i, j: (h0_layer, i, 0, j))
    state_spec = pl.BlockSpec((None, SCAN_SEQS, ns), lambda i, j: (i, 0, j))
    return pl.pallas_call(
        functools.partial(_ssm_kernel, steps=steps, chained=chained),
        out_shape=(jax.ShapeDtypeStruct((nb, rows, SSM_WIDTH), F32),
                   jax.ShapeDtypeStruct(h0_re.shape[1:], F32),
                   jax.ShapeDtypeStruct(h0_im.shape[1:], F32)),
        grid=(nb, nblk),
        in_specs=[pl.BlockSpec((None, rows, LANES), lambda i, j: (i, 0, j)),
                  h0_spec, h0_spec,
                  pl.BlockSpec((None, 1, ns), lambda i, j: (layer, 0, j)),
                  pl.BlockSpec((None, 1, ns), lambda i, j: (layer, 0, j)),
                  pl.BlockSpec((None, None, LANES, 2 * ns), lambda i, j: (layer, j, 0, 0)),
                  pl.BlockSpec((None, None, ns, LANES), lambda i, j: (layer, j, 0, 0)),
                  pl.BlockSpec((None, None, ns, LANES), lambda i, j: (layer, j, 0, 0)),
                  pl.BlockSpec((None, 1, LANES), lambda i, j: (layer, 0, j))],
        out_specs=(pl.BlockSpec((None, rows, LANES), lambda i, j: (i, 0, j)), state_spec, state_spec),
        scratch_shapes=[pltpu.VMEM((rows, 2 * ns), F32), pltpu.VMEM((rows, LANES), F32)],
        compiler_params=_params(("parallel", "parallel"), vmem,
                                _nbytes((nb, rows, SSM_WIDTH), F32)),
        name="ssm_chained" if chained else "ssm_step",
    )(u, h0_re, h0_im, *ssm_w)


def _conv_kernel(h_ref, gb_ref, gc_ref, w_ref, pre_ref, y_ref, st_ref):
    n = h_ref.shape[0]
    u = gc_ref[...] * h_ref[...]
    row = lax.broadcasted_iota(jnp.int32, u.shape, 0)
    pre = pre_ref[...]
    p0, p1 = pre[0:1, :], pre[1:2, :]
    u1 = jnp.where(row == 0, p1, pltpu.roll(u, 1, axis=0))
    u2 = jnp.where(row == 0, p0, jnp.where(row == 1, p1, pltpu.roll(u, 2, axis=0)))
    w = w_ref[...]
    y = w[0:1, :] * u2 + w[1:2, :] * u1 + w[2:3, :] * u
    y_ref[...] = gb_ref[...] * y
    st_ref[...] = u[n - (CONV_K - 1):, :]


def _conv(rest, w, layer, prefix, prefix_layer):
    b, n, _ = rest.shape
    assert n >= CONV_K - 1
    nblk = CONV_CH // LANES
    off = SSM_WIDTH // LANES
    col = lambda part: pl.BlockSpec((None, n, LANES), lambda i, j: (i, 0, off + part * nblk + j))
    vmem = 2 * 4 * _nbytes((n, LANES), F32) + 8 * _nbytes((n, LANES), F32) + 4 * MIB
    return pl.pallas_call(
        _conv_kernel,
        out_shape=(jax.ShapeDtypeStruct((b, n, CONV_CH), F32),
                   jax.ShapeDtypeStruct((b, CONV_K - 1, CONV_CH), F32)),
        grid=(b, nblk),
        in_specs=[col(0), col(1), col(2),
                  pl.BlockSpec((None, CONV_K, LANES), lambda i, j: (layer, 0, j)),
                  pl.BlockSpec((None, None, CONV_K - 1, LANES), lambda i, j: (prefix_layer, i, 0, j))],
        out_specs=(pl.BlockSpec((None, n, LANES), lambda i, j: (i, 0, j)),
                   pl.BlockSpec((None, CONV_K - 1, LANES), lambda i, j: (i, 0, j))),
        compiler_params=_params(("parallel", "parallel"), vmem,
                                _nbytes((b, n, CONV_CH), F32)),
        name="conv",
    )(rest, rest, rest, w, prefix)


def _out_proj_kernel(oatt_ref, yssm_ref, yconv_ref, x_ref, oatts_ref, yssms_ref, yconvs_ref, xs_ref,
                     g_ref, wglu_ref, bglu_ref, w_ref, o_ref, os_ref, mix_ref, *, tm, tn):
    i, j = pl.program_id(0), pl.program_id(1)
    a0, a1 = ATT_WIDTH, ATT_WIDTH + SSM_WIDTH

    def mixed(rows, o_att, y_ssm, y_conv):
        g = g_ref[...]
        mix_ref[rows, :a0] = _rms(o_att, g[:, :a0]).astype(BF16)
        y_ssm = y_ssm * _sigmoid(_dot(y_ssm.astype(BF16), wglu_ref[...].astype(BF16)) + bglu_ref[...])
        mix_ref[rows, a0:a1] = _rms(y_ssm, g[:, a0:a1]).astype(BF16)
        mix_ref[rows, a1:] = _rms(y_conv, g[:, a1:]).astype(BF16)

    @pl.when(j == 0)
    def _():
        mixed(slice(0, tm), oatt_ref[...], yssm_ref[...], yconv_ref[...])

    @pl.when((j == 0) & (i == 0))
    def _():
        mixed(slice(tm, None), oatts_ref[...], yssms_ref[...], yconvs_ref[...])

    @pl.when(i == 0)
    def _():
        z = _dot(mix_ref[...], w_ref[...].astype(BF16))
        o_ref[...] = x_ref[...] + z[:tm]
        cols = pl.ds(pl.multiple_of(j * tn, tn), tn)
        os_ref[:, cols] = xs_ref[:, cols] + z[tm:]

    @pl.when(i > 0)
    def _():
        o_ref[...] = x_ref[...] + _dot(mix_ref[:tm], w_ref[...].astype(BF16))


def _out_proj(prompt, sample, g, w_glu, b_glu, w, layer, *, tm, tn=512):
    m, d = prompt[3].shape
    ms = sample[3].shape[0]
    k = w.shape[1]
    vmem = (2 * (_nbytes((tm, k), F32) + 2 * _nbytes((tm, tn), F32) + 3 * _nbytes((ms, k), F32)
                 + _nbytes((k, tn), F32) + _nbytes((SSM_WIDTH, SSM_WIDTH), F32))
            + _nbytes((tm + ms, k), BF16) + _nbytes((k, tn), BF16) + 2 * _nbytes((tm + ms, k), F32)
            + 2 * MIB)
    row = lambda c: pl.BlockSpec(
        (tm, c), lambda i, j: (jnp.minimum(jnp.where(j > 0, i + 1, i), m // tm - 1), 0))
    whole = lambda c: pl.BlockSpec((ms, c), lambda i, j: (0, 0))
    layered = lambda r, c: pl.BlockSpec((None, r, c), lambda i, j: (layer, 0, 0))
    return pl.pallas_call(
        functools.partial(_out_proj_kernel, tm=tm, tn=tn),
        out_shape=(jax.ShapeDtypeStruct((m, d), F32), jax.ShapeDtypeStruct((ms, d), F32)),
        grid=(m // tm, d // tn),
        in_specs=[row(ATT_WIDTH), row(SSM_WIDTH), row(CONV_CH),
                  pl.BlockSpec((tm, tn), lambda i, j: (i, j)),
                  whole(ATT_WIDTH), whole(SSM_WIDTH), whole(CONV_CH), whole(d),
                  layered(1, k),
                  layered(SSM_WIDTH, SSM_WIDTH),
                  layered(1, SSM_WIDTH),
                  pl.BlockSpec((None, k, tn), lambda i, j: (layer, 0, j))],
        out_specs=(pl.BlockSpec((tm, tn), lambda i, j: (i, j)), whole(d)),
        scratch_shapes=[pltpu.VMEM((tm + ms, k), BF16)],
        compiler_params=_params(("arbitrary", "arbitrary"), vmem),
        name="out_proj",
    )(*prompt, *sample, g, w_glu, b_glu, w)


def _ffn_kernel(x_ref, xs_ref, g_ref, gfin_ref, wg_ref, wu_ref, wd_ref, o_ref, os_ref, xn_ref, *,
                tm, final):
    i, f = pl.program_id(0), pl.program_id(1)
    last = f == pl.num_programs(1) - 1

    @pl.when(f == 0)
    def _():
        x = x_ref[...]
        xn_ref[:tm] = _rms(x, g_ref[...]).astype(BF16)
        o_ref[...] = x

    @pl.when((f == 0) & (i == 0))
    def _():
        xs = xs_ref[...]
        xn_ref[tm:] = _rms(xs, g_ref[...]).astype(BF16)
        os_ref[...] = xs

    def swiglu(xn):
        gate = _dot(xn, wg_ref[...].astype(BF16))
        up = _dot(xn, wu_ref[...].astype(BF16))
        return _dot((gate * _sigmoid(gate) * up).astype(BF16), wd_ref[...].astype(BF16))

    @pl.when(i == 0)
    def _():
        y = swiglu(xn_ref[...])
        o_ref[...] += y[:tm]
        os_ref[...] += y[tm:]

    @pl.when(i > 0)
    def _():
        o_ref[...] += swiglu(xn_ref[:tm])

    if final:
        @pl.when(last)
        def _():
            o_ref[...] = _rms(o_ref[...], gfin_ref[...])

        @pl.when(last & (i == 0))
        def _():
            os_ref[...] = _rms(os_ref[...], gfin_ref[...])


def _ffn(x, xs, g, w_gate, w_up, w_down, layer, *, tm, g_final=None, tf=256):
    final = g_final is not None
    m, d = x.shape
    g_last = g_final.reshape(1, d) if final else g[layer]
    ms = xs.shape[0]
    ff = w_gate.shape[2]
    vmem = (2 * (2 * _nbytes((tm, d), F32) + 2 * _nbytes((ms, d), F32) + 3 * _nbytes((d, tf), F32))
            + _nbytes((tm + ms, d), BF16) + 3 * _nbytes((d, tf), BF16) + 4 * _nbytes((tm + ms, tf), F32)
            + 2 * MIB)
    return pl.pallas_call(
        functools.partial(_ffn_kernel, tm=tm, final=final),
        out_shape=(jax.ShapeDtypeStruct((m, d), F32), jax.ShapeDtypeStruct((ms, d), F32)),
        grid=(m // tm, ff // tf),
        in_specs=[pl.BlockSpec((tm, d), lambda i, f: (jnp.minimum(jnp.where(f > 0, i + 1, i), m // tm - 1), 0)),
                  pl.BlockSpec((ms, d), lambda i, f: (0, 0)),
                  pl.BlockSpec((None, 1, d), lambda i, f: (layer, 0, 0)),
                  pl.BlockSpec((1, d), lambda i, f: (0, 0)),
                  pl.BlockSpec((None, d, tf), lambda i, f: (layer, 0, f)),
                  pl.BlockSpec((None, d, tf), lambda i, f: (layer, 0, f)),
                  pl.BlockSpec((None, tf, d), lambda i, f: (layer, f, 0))],
        out_specs=(pl.BlockSpec((tm, d), lambda i, f: (i, 0)),
                   pl.BlockSpec((ms, d), lambda i, f: (0, 0))),
        scratch_shapes=[pltpu.VMEM((tm + ms, d), BF16)],
        compiler_params=_params(("arbitrary", "arbitrary"), vmem),
        name="ffn",
    )(x, xs, g, g_last, w_gate, w_up, w_down)


def _mixers(q, kv, rest, ssm_w, conv_w, layer, *, batch, cache_t, conv_prefix, ssm_h0, state_layer):
    m = q.shape[0]
    n = m // batch
    q3, kv3, rest3 = (a.reshape(batch, n, a.shape[-1]) for a in (q, kv, rest))
    if cache_t is None:
        o_att = _attn_prompt(q3, kv3)
        y_ssm, h_re, h_im = _ssm(rest3, *ssm_h0, state_layer, ssm_w, layer, chained=True)
        h_re, h_im = h_re[:, SCAN_SEQS - 1], h_im[:, SCAN_SEQS - 1]
    else:
        o_att = _attn_sample(q3, kv3, cache_t, layer)
        y_ssm, h_re, h_im = _ssm(rest.reshape(1, m, rest.shape[-1]), *ssm_h0, state_layer, ssm_w, layer,
                                 chained=False)
        h_re, h_im = h_re[0], h_im[0]
    y_conv, conv_state = _conv(rest3, conv_w, layer, conv_prefix, state_layer)
    mixed = (o_att.reshape(m, ATT_WIDTH), y_ssm.reshape(m, SSM_WIDTH), y_conv.reshape(m, CONV_CH))
    return mixed, conv_state, h_re, h_im


def kernel(x_prompt, x_sample, cache_kv, state_conv, state_ssm_re, state_ssm_im, g_mix_norm, w_in,
           ssm_a_re, ssm_a_im, ssm_log_dt, ssm_b_re, ssm_b_im, ssm_c_re, ssm_c_im, ssm_d, ssm_w_glu,
           ssm_b_glu, conv_w, g_out, w_out, g_ffn_norm, w_gate, w_up, w_down, g_final):
    batch, seq, d_model = x_prompt.shape
    dec_batch, dec_seq, _ = x_sample.shape
    depth, n_groups, n_state = ssm_a_re.shape
    n_heads = ATT_WIDTH // HEAD_DIM
    assert dec_batch == SCAN_SEQS and seq % SCAN_SEQS == 0

    ab_re, ab_im, bb_re, bb_im = _ssm_prep(ssm_a_re, ssm_a_im, ssm_log_dt, ssm_b_re, ssm_b_im)
    cache_t = jnp.transpose(cache_kv, (0, 1, 3, 4, 5, 2))

    row = lambda a: a.reshape(depth, 1, -1)
    ssm_w = (row(ab_re), row(ab_im),
             jnp.concatenate([_block_diag(bb_re), _block_diag(bb_im)], axis=-1).astype(BF16),
             _block_diag(jnp.swapaxes(ssm_c_re, -1, -2)).astype(BF16),
             _block_diag(jnp.swapaxes(ssm_c_im, -1, -2)).astype(BF16),
             row(ssm_d))
    g_mix, g_mixed, g_ffn, b_glu = row(g_mix_norm), row(g_out), row(g_ffn_norm), row(ssm_b_glu)
    state_ssm = (state_ssm_re.reshape(depth, 1, dec_batch, -1), state_ssm_im.reshape(depth, 1, dec_batch, -1))

    xp = x_prompt.reshape(batch * seq, d_model)
    xs = x_sample.reshape(dec_batch * dec_seq, d_model)
    zero_state = jnp.zeros((1, batch, SCAN_SEQS, n_groups * n_state), F32)
    zero_prefix = jnp.zeros((1, batch, CONV_K - 1, CONV_CH), F32)
    kvt = jnp.zeros((depth, batch, 2 * ATT_WIDTH, seq), F32)

    outs_p, outs_s = [], []
    for i in range(depth):
        q, kv, rest, kvt, q_s, kv_s, rest_s = _in_proj(xp, xs, g_mix, w_in, kvt, i, tm=2048, seq=seq, tn=256)

        mixed_p, c_state, h_re, h_im = _mixers(
            q, kv, rest, ssm_w, conv_w, i, batch=batch, cache_t=None, conv_prefix=zero_prefix,
            ssm_h0=(zero_state, zero_state), state_layer=0)
        outs_p.append((c_state, h_re.reshape(batch, n_groups, n_state),
                       h_im.reshape(batch, n_groups, n_state)))

        mixed_s, c_state, h_re, h_im = _mixers(
            q_s, kv_s, rest_s, ssm_w, conv_w, i, batch=dec_batch, cache_t=cache_t, conv_prefix=state_conv,
            ssm_h0=state_ssm, state_layer=i)
        outs_s.append((kv_s.reshape(dec_batch, dec_seq, 2, n_heads, HEAD_DIM), c_state,
                       h_re.reshape(dec_batch, n_groups, n_state),
                       h_im.reshape(dec_batch, n_groups, n_state)))

        xp, xs = _out_proj((*mixed_p, xp), (*mixed_s, xs), g_mixed, ssm_w_glu, b_glu, w_out, i, tm=1024)
        xp, xs = _ffn(xp, xs, g_ffn, w_gate, w_up, w_down, i, tm=1024,
                      g_final=g_final if i == depth - 1 else None)

    y_prompt = xp.reshape(batch, seq, d_model)
    y_sample = xs.reshape(dec_batch, dec_seq, d_model)
    kv_prompt = jnp.transpose(kvt.reshape(depth, batch, 2, n_heads, HEAD_DIM, seq), (0, 1, 5, 2, 3, 4))
    n_keep = min(max(ATT_DILATIONS) * ATT_SPAN, seq)
    stack = lambda outs, k: jnp.stack([o[k] for o in outs])
    return (y_prompt, y_sample,
            kv_prompt[:, :, seq - n_keep:], stack(outs_p, 0), stack(outs_p, 1), stack(outs_p, 2),
            stack(outs_s, 0), stack(outs_s, 1), stack(outs_s, 2), stack(outs_s, 3))
```

```python
import functools
import math

import jax
import jax.numpy as jnp
from jax import lax
from jax.experimental import pallas as pl
from jax.experimental.pallas import tpu as pltpu

F32 = jnp.float32
BF16 = jnp.bfloat16

HEAD_DIM = 64
ATT_WIDTH = 1024
SSM_WIDTH = 512
CONV_CH = 512
CONV_K = 3
SSM_GROUP = 16
SSM_STATE = 64
ATT_DILATIONS = (1, 4, 16)
ATT_SPAN = 128
RMS_EPS = 1e-6
NEG = -1e30

LANES = 128
SUBLANES = 8
SCAN_SEQS = SUBLANES
SSM_BLOCK_GROUPS = LANES // SSM_GROUP
SSM_BLOCK_STATE = SSM_BLOCK_GROUPS * SSM_STATE

MIB = 1024 * 1024
V7X_VMEM_BYTES = 64 * MIB


def _params(semantics, vmem_bytes, hbm_result_bytes=None):
    if hbm_result_bytes is not None and hbm_result_bytes >= V7X_VMEM_BYTES // 8:
        vmem_bytes = max(vmem_bytes, V7X_VMEM_BYTES - hbm_result_bytes + MIB)
    return pltpu.CompilerParams(dimension_semantics=semantics, vmem_limit_bytes=int(vmem_bytes))


def _nbytes(shape, dtype):
    return math.prod(shape) * jnp.dtype(dtype).itemsize


def _rms(x, g):
    ms = jnp.mean(x * x, axis=-1, keepdims=True)
    return x * lax.rsqrt(ms + RMS_EPS) * g


def _sigmoid(x):
    return 1.0 / (1.0 + jnp.exp(-x))


def _gelu_tanh(x):
    c = math.sqrt(2.0 / math.pi)
    return 0.5 * x * (1.0 + jnp.tanh(c * (x + 0.044715 * (x * x * x))))


def _dot(a, b):
    return jnp.dot(a, b, preferred_element_type=F32)


def _dot_nt(a, b):
    return lax.dot_general(a, b, (((1,), (1,)), ((), ())), preferred_element_type=F32)


def _in_proj_kernel(x_ref, xs_ref, g_ref, w_ref, kvt_in_ref, q_ref, kv_ref, rest_ref, kvt_ref,
                    qs_ref, kvs_ref, rests_ref, xn_ref, *, tm, tn, nq, nkv):
    del kvt_in_ref
    i, j = pl.program_id(0), pl.program_id(1)

    @pl.when(j == 0)
    def _():
        xn_ref[:tm] = _rms(x_ref[...], g_ref[...]).astype(BF16)

    @pl.when((j == 0) & (i == 0))
    def _():
        xn_ref[tm:] = _rms(xs_ref[...], g_ref[...]).astype(BF16)

    def project(out_ref, outs_ref, col, transposed_ref=None):
        @pl.when(i == 0)
        def _():
            z = _dot(xn_ref[...], w_ref[...].astype(BF16))
            out_ref[...] = z[:tm]
            outs_ref[:, pl.ds(pl.multiple_of(col * tn, tn), tn)] = z[tm:]
            if transposed_ref is not None:
                transposed_ref[...] = z[:tm].T

        @pl.when(i > 0)
        def _():
            z = _dot(xn_ref[:tm], w_ref[...].astype(BF16))
            out_ref[...] = z
            if transposed_ref is not None:
                transposed_ref[...] = z.T

    @pl.when(j < nq)
    def _():
        project(q_ref, qs_ref, j)

    @pl.when((j >= nq) & (j < nq + nkv))
    def _():
        project(kv_ref, kvs_ref, j - nq, kvt_ref)

    @pl.when(j >= nq + nkv)
    def _():
        project(rest_ref, rests_ref, j - nq - nkv)


def _in_proj(x, xs, g, w, kvt_stack, layer, *, tm, seq, tn=512):
    m, k = x.shape
    ms = xs.shape[0]
    n = w.shape[2]
    nq, nkv = ATT_WIDTH // tn, 2 * ATT_WIDTH // tn
    nrest = n // tn - nq - nkv
    assert seq % tm == 0 and tm % LANES == 0
    assert kvt_stack.shape[1:] == (m // seq, nkv * tn, seq)
    tiles_per_seq = seq // tm
    vmem = (_nbytes((tm, k), F32)
            + 2 * (_nbytes((ms, k), F32) + _nbytes((k, tn), F32) + 4 * _nbytes((tm, tn), F32)
                   + _nbytes((ms, n), F32))
            + _nbytes((tm + ms, k), BF16) + _nbytes((k, tn), BF16) + 3 * _nbytes((tm + ms, tn), F32)
            + 2 * MIB)
    kv_col = lambda j: jnp.clip(j - nq, 0, nkv - 1)
    whole = lambda cols: pl.BlockSpec((ms, cols), lambda i, j: (0, 0))
    return pl.pallas_call(
        functools.partial(_in_proj_kernel, tm=tm, tn=tn, nq=nq, nkv=nkv),
        out_shape=(jax.ShapeDtypeStruct((m, nq * tn), F32),
                   jax.ShapeDtypeStruct((m, nkv * tn), F32),
                   jax.ShapeDtypeStruct((m, nrest * tn), F32),
                   jax.ShapeDtypeStruct(kvt_stack.shape, F32),
                   jax.ShapeDtypeStruct((ms, nq * tn), F32),
                   jax.ShapeDtypeStruct((ms, nkv * tn), F32),
                   jax.ShapeDtypeStruct((ms, nrest * tn), F32)),
        grid=(m // tm, n // tn),
        in_specs=[pl.BlockSpec((tm, k), lambda i, j: (i, 0), pipeline_mode=pl.Buffered(1)),
                  whole(k),
                  pl.BlockSpec((None, 1, k), lambda i, j: (layer, 0, 0)),
                  pl.BlockSpec((None, k, tn), lambda i, j: (layer, 0, j)),
                  pl.BlockSpec(memory_space=pl.ANY)],
        out_specs=(pl.BlockSpec((tm, tn), lambda i, j: (i, jnp.minimum(j, nq - 1))),
                   pl.BlockSpec((tm, tn), lambda i, j: (i, kv_col(j))),
                   pl.BlockSpec((tm, tn), lambda i, j: (i, jnp.clip(j - nq - nkv, 0, nrest - 1))),
                   pl.BlockSpec((None, None, tn, tm),
                                lambda i, j: (layer, i // tiles_per_seq, kv_col(j), i % tiles_per_seq)),
                   whole(nq * tn), whole(nkv * tn), whole(nrest * tn)),
        scratch_shapes=[pltpu.VMEM((tm + ms, k), BF16)],
        input_output_aliases={4: 3},
        compiler_params=_params(("arbitrary", "arbitrary"), vmem),
        name="in_proj",
    )(x, xs, g, w, kvt_stack)


def _attn_prompt_kernel(q_ref, k_ref, v_ref, o_ref, ob_ref, mb_ref, lb_ref, bias_ref):
    s_len = q_ref.shape[0]
    blk = ATT_SPAN
    lane = lax.broadcasted_iota(jnp.int32, (blk, LANES), 1)
    head0 = lane < HEAD_DIM
    scale = HEAD_DIM ** -0.5

    qi = lax.broadcasted_iota(jnp.int32, (2 * blk, 3 * blk), 0) & (blk - 1)
    kj = lax.broadcasted_iota(jnp.int32, (2 * blk, 3 * blk), 1)
    dist = jnp.where(kj < blk, qi - kj, qi + 2 * blk - kj)
    bias_ref[...] = jnp.where((dist >= 0) & (dist <= ATT_SPAN), 0.0, NEG)

    def block(p, d, q0, k0=None):
        def rows(start, n):
            return pl.ds(start, n, stride=d) if d > 1 else pl.ds(start, n)

        qt = q_ref[rows(q0, blk), :] * scale
        qq = jnp.concatenate([jnp.where(head0, qt, 0.0), jnp.where(head0, 0.0, qt)], axis=0).astype(BF16)
        if k0 is None:
            key_rows, bias = rows(q0, blk), bias_ref[:, :blk]
        else:
            key_rows, bias = rows(k0, 2 * blk), bias_ref[:, blk:]
        k = k_ref[key_rows, :].astype(BF16)
        v = v_ref[key_rows, :].astype(BF16)
        s = _dot_nt(qq, k) + bias
        m = jnp.max(s, axis=-1, keepdims=True)
        e = jnp.exp(s - m)
        l = jnp.sum(e, axis=-1, keepdims=True)
        o = _dot(e.astype(BF16), v)
        m = jnp.broadcast_to(m, (2 * blk, LANES))
        l = jnp.broadcast_to(l, (2 * blk, LANES))
        ob_ref[p, rows(q0, blk), :] = jnp.where(head0, o[:blk], o[blk:])
        mb_ref[p, rows(q0, blk), :] = jnp.where(head0, m[:blk], m[blk:])
        lb_ref[p, rows(q0, blk), :] = jnp.where(head0, l[:blk], l[blk:])

    for p, d in enumerate(ATT_DILATIONS):
        nblk = s_len // (d * blk)
        for r in range(d):
            for b in range(nblk):
                block(p, d, r + d * blk * b, None if b == 0 else r + d * blk * (b - 1))

    def merge(c, carry):
        rows = pl.ds(pl.multiple_of(c * blk, blk), blk)
        m0, m1, m2 = mb_ref[0, rows, :], mb_ref[1, rows, :], mb_ref[2, rows, :]
        m = jnp.maximum(jnp.maximum(m0, m1), m2)
        e0, e1, e2 = jnp.exp(m0 - m), jnp.exp(m1 - m), jnp.exp(m2 - m)
        num = e0 * ob_ref[0, rows, :] + e1 * ob_ref[1, rows, :] + e2 * ob_ref[2, rows, :]
        den = e0 * lb_ref[0, rows, :] + e1 * lb_ref[1, rows, :] + e2 * lb_ref[2, rows, :]
        o_ref[rows, :] = num / den
        return carry

    lax.fori_loop(0, s_len // blk, merge, 0)


def _attn_prompt(q, kv):
    b, s, _ = q.shape
    assert s % (max(ATT_DILATIONS) * ATT_SPAN) == 0
    npair = ATT_WIDTH // LANES
    blk_bytes = _nbytes((s, LANES), F32)
    vmem = 2 * 4 * blk_bytes + 3 * len(ATT_DILATIONS) * blk_bytes + 4 * MIB
    return pl.pallas_call(
        _attn_prompt_kernel,
        out_shape=jax.ShapeDtypeStruct((b, s, ATT_WIDTH), F32),
        grid=(b, npair),
        in_specs=[pl.BlockSpec((None, s, LANES), lambda i, j: (i, 0, j)),
                  pl.BlockSpec((None, s, LANES), lambda i, j: (i, 0, j)),
                  pl.BlockSpec((None, s, LANES), lambda i, j: (i, 0, npair + j))],
        out_specs=pl.BlockSpec((None, s, LANES), lambda i, j: (i, 0, j)),
        scratch_shapes=[pltpu.VMEM((len(ATT_DILATIONS), s, LANES), F32),
                        pltpu.VMEM((len(ATT_DILATIONS), s, LANES), F32),
                        pltpu.VMEM((len(ATT_DILATIONS), s, LANES), F32),
                        pltpu.VMEM((2 * ATT_SPAN, 3 * ATT_SPAN), F32)],
        compiler_params=_params(("parallel", "parallel"), vmem, _nbytes((b, s, ATT_WIDTH), F32)),
        name="attn_prompt",
    )(q, kv, kv)


def _attn_sample_kernel(q_ref, kn_ref, vn_ref, kct_ref, vct_ref, o_ref):
    t_len = q_ref.shape[0]
    w_len = kct_ref.shape[-1]
    pairs = q_ref.shape[1] // LANES
    pad = LANES
    zpad = jnp.zeros((pad - t_len, LANES), F32)

    def multiplicity(dist):
        c = jnp.zeros(dist.shape, F32)
        for d in ATT_DILATIONS:
            ok = (dist >= 0) & ((dist & (d - 1)) == 0) & (dist <= ATT_SPAN * d)
            c = c + jnp.where(ok, 1.0, 0.0)
        return c

    tc = lax.broadcasted_iota(jnp.int32, (t_len, w_len), 0)
    jc = lax.broadcasted_iota(jnp.int32, (t_len, w_len), 1)
    cnt_c = multiplicity(w_len + tc - jc)
    tn = lax.broadcasted_iota(jnp.int32, (t_len, pad), 0)
    jn = lax.broadcasted_iota(jnp.int32, (t_len, pad), 1)
    cnt_n = multiplicity(jnp.where(jn < t_len, tn - jn, -1))

    lane = lax.broadcasted_iota(jnp.int32, (t_len, LANES), 1)
    head0 = lane < HEAD_DIM
    for p in range(pairs):
        lanes = slice(p * LANES, (p + 1) * LANES)
        q = q_ref[:, lanes] * (HEAD_DIM ** -0.5)
        kn = jnp.concatenate([kn_ref[:, lanes], zpad], axis=0).astype(BF16)
        vn = jnp.concatenate([vn_ref[:, lanes], zpad], axis=0).astype(BF16)
        kct = kct_ref[2 * p:2 * p + 2].reshape(LANES, w_len).astype(BF16)
        vct = vct_ref[2 * p:2 * p + 2].reshape(LANES, w_len).astype(BF16)
        outs = []
        for h in range(2):
            hm = head0 if h == 0 else jnp.logical_not(head0)
            qh = jnp.where(hm, q, 0.0).astype(BF16)
            sc = jnp.where(cnt_c > 0, _dot(qh, kct), NEG)
            sn = jnp.where(cnt_n > 0, _dot_nt(qh, kn), NEG)
            m = jnp.maximum(jnp.max(sc, axis=-1, keepdims=True), jnp.max(sn, axis=-1, keepdims=True))
            ec = cnt_c * jnp.exp(sc - m)
            en = cnt_n * jnp.exp(sn - m)
            l = jnp.sum(ec, axis=-1, keepdims=True) + jnp.sum(en, axis=-1, keepdims=True)
            pv = _dot_nt(ec.astype(BF16), vct) + _dot(en.astype(BF16), vn)
            outs.append(pv / l)
        o_ref[:, lanes] = jnp.where(head0, outs[0], outs[1])


def _attn_sample(q, kv, cache_t, layer, *, pairs=4):
    b, t, _ = q.shape
    w = cache_t.shape[-1]
    assert all(d & (d - 1) == 0 for d in ATT_DILATIONS) and t <= LANES
    width = pairs * LANES
    nstep = ATT_WIDTH // width
    vmem = 2 * 2 * _nbytes((width, w), F32) + 10 * _nbytes((LANES, w), F32) * pairs + 4 * MIB
    new_spec = lambda off: pl.BlockSpec((None, t, width), lambda i, j: (i, 0, off + j))
    buf_spec = lambda c: pl.BlockSpec((None, None, None, 2 * pairs, HEAD_DIM, w),
                                      lambda i, j: (layer, i, c, j, 0, 0))
    return pl.pallas_call(
        _attn_sample_kernel,
        out_shape=jax.ShapeDtypeStruct((b, t, ATT_WIDTH), F32),
        grid=(b, nstep),
        in_specs=[new_spec(0), new_spec(0), new_spec(nstep), buf_spec(0), buf_spec(1)],
        out_specs=new_spec(0),
        compiler_params=_params(("parallel", "parallel"), vmem),
        name="attn_sample",
    )(q, kv, kv, cache_t, cache_t)


def _ssm_prep_kernel(are_ref, aim_ref, ldt_ref, bre_ref, bim_ref, abre_ref, abim_ref, bbre_ref, bbim_ref):
    a_re, a_im = are_ref[...], aim_ref[...]
    dt = jnp.exp(ldt_ref[...])
    mag = jnp.exp(a_re * dt)
    ang = a_im * dt
    ab_re, ab_im = mag * jnp.cos(ang), mag * jnp.sin(ang)
    den = a_re * a_re + a_im * a_im
    nr, ni = ab_re - 1.0, ab_im
    cf_re = (nr * a_re + ni * a_im) / den
    cf_im = (ni * a_re - nr * a_im) / den
    abre_ref[...] = ab_re
    abim_ref[...] = ab_im
    for c in range(SSM_GROUP):
        b_re, b_im = bre_ref[c], bim_ref[c]
        bbre_ref[c] = cf_re * b_re - cf_im * b_im
        bbim_ref[c] = cf_re * b_im + cf_im * b_re


def _ssm_prep(a_re, a_im, log_dt, b_re, b_im):
    depth, g, n = a_re.shape
    rows = depth * g
    flat = lambda a: a.reshape(rows, n)
    chan_major = lambda b: jnp.transpose(b.reshape(rows, n, SSM_GROUP), (2, 0, 1))
    ab_re, ab_im, bb_re, bb_im = pl.pallas_call(
        _ssm_prep_kernel,
        out_shape=(jax.ShapeDtypeStruct((rows, n), F32), jax.ShapeDtypeStruct((rows, n), F32),
                   jax.ShapeDtypeStruct((SSM_GROUP, rows, n), F32),
                   jax.ShapeDtypeStruct((SSM_GROUP, rows, n), F32)),
        name="ssm_prep",
    )(flat(a_re), flat(a_im), log_dt.reshape(rows, 1), chan_major(b_re), chan_major(b_im))
    unflat = lambda b: jnp.transpose(b, (1, 0, 2)).reshape(depth, g, SSM_GROUP, n)
    return ab_re.reshape(depth, g, n), ab_im.reshape(depth, g, n), unflat(bb_re), unflat(bb_im)


def _block_diag(m):
    *lead, g, r, c = m.shape
    nb = g // SSM_BLOCK_GROUPS
    m = m.reshape(*lead, nb, SSM_BLOCK_GROUPS, r, c)
    eye = jnp.eye(SSM_BLOCK_GROUPS, dtype=m.dtype)
    out = m[..., :, :, None, :] * eye[:, None, :, None]
    return out.reshape(*lead, nb, SSM_BLOCK_GROUPS * r, SSM_BLOCK_GROUPS * c)


def _ssm_kernel(u_ref, h0re_ref, h0im_ref, are_ref, aim_ref, bb_ref, cre_ref, cim_ref, d_ref,
                y_ref, hre_ref, him_ref, x_scr, up_scr, *, steps, chained):
    ns = SSM_BLOCK_STATE
    nrows = steps * SCAN_SEQS
    chunk = min(nrows, 256)

    nchunks = nrows // chunk
    chunk_steps = chunk // SCAN_SEQS
    unroll = min(steps, 8)

    def permute_in(t, c):
        up_scr[pl.ds(pl.multiple_of(t * SCAN_SEQS, SCAN_SEQS), SCAN_SEQS), :] = (
            u_ref[pl.ds(t, SCAN_SEQS, stride=steps), :])
        return c
    lax.fori_loop(0, steps, permute_in, 0, unroll=unroll)

    a_re = jnp.broadcast_to(are_ref[...], (SCAN_SEQS, ns))
    a_im = jnp.broadcast_to(aim_ref[...], (SCAN_SEQS, ns))

    def input_map(c):
        rows = pl.ds(c * chunk, chunk)
        x_scr[rows, :] = _dot(up_scr[rows, :].astype(BF16), bb_ref[...])

    def scan_chunk(c, h, store):
        h_re, h_im = h
        for t in range(c * chunk_steps, (c + 1) * chunk_steps):
            rows = pl.ds(t * SCAN_SEQS, SCAN_SEQS)
            h_re, h_im = (a_re * h_re - a_im * h_im + x_scr[rows, :ns],
                          a_re * h_im + a_im * h_re + x_scr[rows, ns:])
            if store:
                x_scr[rows, :ns] = h_re
                x_scr[rows, ns:] = h_im
        return h_re, h_im

    def output_map(c):
        rows = pl.ds(c * chunk, chunk)
        y = (_dot(x_scr[rows, :ns].astype(BF16), cre_ref[...])
             - _dot(x_scr[rows, ns:].astype(BF16), cim_ref[...])
             + d_ref[...] * up_scr[rows, :])
        up_scr[rows, :] = _gelu_tanh(y)

    init = (h0re_ref[...], h0im_ref[...])
    input_map(0)
    if chained:
        assert steps & (steps - 1) == 0
        e_re, e_im = init
        for c in range(nchunks):
            if c + 1 < nchunks:
                input_map(c + 1)
            e_re, e_im = scan_chunk(c, (e_re, e_im), store=False)
        p_re, p_im = a_re, a_im
        for _ in range(steps.bit_length() - 1):
            p_re, p_im = p_re * p_re - p_im * p_im, 2.0 * p_re * p_im
        sub = lax.broadcasted_iota(jnp.int32, (SCAN_SEQS, ns), 0)
        shift = 1
        while shift < SCAN_SEQS:
            r_re = pltpu.roll(e_re, shift, axis=0)
            r_im = pltpu.roll(e_im, shift, axis=0)
            keep = sub >= shift
            e_re, e_im = (e_re + jnp.where(keep, p_re * r_re - p_im * r_im, 0.0),
                          e_im + jnp.where(keep, p_re * r_im + p_im * r_re, 0.0))
            p_re, p_im = p_re * p_re - p_im * p_im, 2.0 * p_re * p_im
            shift *= 2
        first = sub == 0
        init = (init[0] + jnp.where(first, 0.0, pltpu.roll(e_re, 1, axis=0)),
                init[1] + jnp.where(first, 0.0, pltpu.roll(e_im, 1, axis=0)))
    else:
        for c in range(1, nchunks):
            input_map(c)
    h = init
    for c in range(nchunks):
        h = scan_chunk(c, h, store=True)
        if c > 0:
            output_map(c - 1)
    output_map(nchunks - 1)
    hre_ref[...] = h[0]
    him_ref[...] = h[1]

    def permute_out(t, c):
        y_ref[pl.ds(t, SCAN_SEQS, stride=steps), :] = (
            up_scr[pl.ds(pl.multiple_of(t * SCAN_SEQS, SCAN_SEQS), SCAN_SEQS), :])
        return c
    lax.fori_loop(0, steps, permute_out, 0, unroll=unroll)


def _ssm(u, h0_re, h0_im, h0_layer, ssm_w, layer, *, chained):
    nb, rows, _ = u.shape
    steps = rows // SCAN_SEQS
    nblk = SSM_WIDTH // LANES
    ns = SSM_BLOCK_STATE
    vmem = (2 * 2 * _nbytes((rows, LANES), F32) + _nbytes((rows, 2 * ns), F32)
            + _nbytes((rows, LANES), F32) + 4 * _nbytes((min(rows, 256), 2 * ns), F32) + 8 * MIB)
    h0_spec = pl.BlockSpec((None, None, SCAN_SEQS, ns), lambda i, j: (h0_layer, i, 0, j))
    state_spec = pl.BlockSpec((None, SCAN_SEQS, ns), lambda i, j: (i, 0, j))
    return pl.pallas_call(
        functools.partial(_ssm_kernel, steps=steps, chained=chained),
        out_shape=(jax.ShapeDtypeStruct((nb, rows, SSM_WIDTH), F32),
                   jax.ShapeDtypeStruct(h0_re.shape[1:], F32),
                   jax.ShapeDtypeStruct(h0_im.shape[1:], F32)),
        grid=(nb, nblk),
        in_specs=[pl.BlockSpec((None, rows, LANES), lambda i, j: (i, 0, j)),
                  h0_spec, h0_spec,
                  pl.BlockSpec((None, 1, ns), lambda i, j: (layer, 0, j)),
                  pl.BlockSpec((None, 1, ns), lambda i, j: (layer, 0, j)),
                  pl.BlockSpec((None, None, LANES, 2 * ns), lambda i, j: (layer, j, 0, 0)),
                  pl.BlockSpec((None, None, ns, LANES), lambda i, j: (layer, j, 0, 0)),
                  pl.BlockSpec((None, None, ns, LANES), lambda i, j: (layer, j, 0, 0)),
                  pl.BlockSpec((None, 1, LANES), lambda i, j: (layer, 0, j))],
        out_specs=(pl.BlockSpec((None, rows, LANES), lambda i, j: (i, 0, j)), state_spec, state_spec),
        scratch_shapes=[pltpu.VMEM((rows, 2 * ns), F32), pltpu.VMEM((rows, LANES), F32)],
        compiler_params=_params(("parallel", "parallel"), vmem,
                                _nbytes((nb, rows, SSM_WIDTH), F32)),
        name="ssm_chained" if chained else "ssm_step",
    )(u, h0_re, h0_im, *ssm_w)


def _conv_kernel(h_ref, gb_ref, gc_ref, w_ref, pre_ref, y_ref, st_ref):
    n = h_ref.shape[0]
    u = gc_ref[...] * h_ref[...]
    row = lax.broadcasted_iota(jnp.int32, u.shape, 0)
    pre = pre_ref[...]
    p0, p1 = pre[0:1, :], pre[1:2, :]
    u1 = jnp.where(row == 0, p1, pltpu.roll(u, 1, axis=0))
    u2 = jnp.where(row == 0, p0, jnp.where(row == 1, p1, pltpu.roll(u, 2, axis=0)))
    w = w_ref[...]
    y = w[0:1, :] * u2 + w[1:2, :] * u1 + w[2:3, :] * u
    y_ref[...] = gb_ref[...] * y
    st_ref[...] = u[n - (CONV_K - 1):, :]


def _conv(rest, w, layer, prefix, prefix_layer):
    b, n, _ = rest.shape
    assert n >= CONV_K - 1
    nblk = CONV_CH // LANES
    off = SSM_WIDTH // LANES
    col = lambda part: pl.BlockSpec((None, n, LANES), lambda i, j: (i, 0, off + part * nblk + j))
    vmem = 2 * 4 * _nbytes((n, LANES), F32) + 8 * _nbytes((n, LANES), F32) + 4 * MIB
    return pl.pallas_call(
        _conv_kernel,
        out_shape=(jax.ShapeDtypeStruct((b, n, CONV_CH), F32),
                   jax.ShapeDtypeStruct((b, CONV_K - 1, CONV_CH), F32)),
        grid=(b, nblk),
        in_specs=[col(0), col(1), col(2),
                  pl.BlockSpec((None, CONV_K, LANES), lambda i, j: (layer, 0, j)),
                  pl.BlockSpec((None, None, CONV_K - 1, LANES), lambda i, j: (prefix_layer, i, 0, j))],
        out_specs=(pl.BlockSpec((None, n, LANES), lambda i, j: (i, 0, j)),
                   pl.BlockSpec((None, CONV_K - 1, LANES), lambda i, j: (i, 0, j))),
        compiler_params=_params(("parallel", "parallel"), vmem,
                                _nbytes((b, n, CONV_CH), F32)),
        name="conv",
    )(rest, rest, rest, w, prefix)


def _out_proj_kernel(oatt_ref, yssm_ref, yconv_ref, x_ref, oatts_ref, yssms_ref, yconvs_ref, xs_ref,
                     g_ref, wglu_ref, bglu_ref, w_ref, o_ref, os_ref, mix_ref, *, tm, tn):
    i, j = pl.program_id(0), pl.program_id(1)
    a0, a1 = ATT_WIDTH, ATT_WIDTH + SSM_WIDTH

    def mixed(rows, o_att, y_ssm, y_conv):
        g = g_ref[...]
        mix_ref[rows, :a0] = _rms(o_att, g[:, :a0]).astype(BF16)
        y_ssm = y_ssm * _sigmoid(_dot(y_ssm.astype(BF16), wglu_ref[...].astype(BF16)) + bglu_ref[...])
        mix_ref[rows, a0:a1] = _rms(y_ssm, g[:, a0:a1]).astype(BF16)
        mix_ref[rows, a1:] = _rms(y_conv, g[:, a1:]).astype(BF16)

    @pl.when(j == 0)
    def _():
        mixed(slice(0, tm), oatt_ref[...], yssm_ref[...], yconv_ref[...])

    @pl.when((j == 0) & (i == 0))
    def _():
        mixed(slice(tm, None), oatts_ref[...], yssms_ref[...], yconvs_ref[...])

    @pl.when(i == 0)
    def _():
        z = _dot(mix_ref[...], w_ref[...].astype(BF16))
        o_ref[...] = x_ref[...] + z[:tm]
        cols = pl.ds(pl.multiple_of(j * tn, tn), tn)
        os_ref[:, cols] = xs_ref[:, cols] + z[tm:]

    @pl.when(i > 0)
    def _():
        o_ref[...] = x_ref[...] + _dot(mix_ref[:tm], w_ref[...].astype(BF16))


def _out_proj(prompt, sample, g, w_glu, b_glu, w, layer, *, tm, tn=512):
    m, d = prompt[3].shape
    ms = sample[3].shape[0]
    k = w.shape[1]
    vmem = (2 * (_nbytes((tm, k), F32) + 2 * _nbytes((tm, tn), F32) + 3 * _nbytes((ms, k), F32)
                 + _nbytes((k, tn), F32) + _nbytes((SSM_WIDTH, SSM_WIDTH), F32))
            + _nbytes((tm + ms, k), BF16) + _nbytes((k, tn), BF16) + 2 * _nbytes((tm + ms, k), F32)
            + 2 * MIB)
    row = lambda c: pl.BlockSpec(
        (tm, c), lambda i, j: (jnp.minimum(jnp.where(j > 0, i + 1, i), m // tm - 1), 0))
    whole = lambda c: pl.BlockSpec((ms, c), lambda i, j: (0, 0))
    layered = lambda r, c: pl.BlockSpec((None, r, c), lambda i, j: (layer, 0, 0))
    return pl.pallas_call(
        functools.partial(_out_proj_kernel, tm=tm, tn=tn),
        out_shape=(jax.ShapeDtypeStruct((m, d), F32), jax.ShapeDtypeStruct((ms, d), F32)),
        grid=(m // tm, d // tn),
        in_specs=[row(ATT_WIDTH), row(SSM_WIDTH), row(CONV_CH),
                  pl.BlockSpec((tm, tn), lambda i, j: (i, j)),
                  whole(ATT_WIDTH), whole(SSM_WIDTH), whole(CONV_CH), whole(d),
                  layered(1, k),
                  layered(SSM_WIDTH, SSM_WIDTH),
                  layered(1, SSM_WIDTH),
                  pl.BlockSpec((None, k, tn), lambda i, j: (layer, 0, j))],
        out_specs=(pl.BlockSpec((tm, tn), lambda i, j: (i, j)), whole(d)),
        scratch_shapes=[pltpu.VMEM((tm + ms, k), BF16)],
        compiler_params=_params(("arbitrary", "arbitrary"), vmem),
        name="out_proj",
    )(*prompt, *sample, g, w_glu, b_glu, w)


def _ffn_kernel(x_ref, xs_ref, g_ref, gfin_ref, wg_ref, wu_ref, wd_ref, o_ref, os_ref, xn_ref, *,
                tm, final):
    i, f = pl.program_id(0), pl.program_id(1)
    last = f == pl.num_programs(1) - 1

    @pl.when(f == 0)
    def _():
        x = x_ref[...]
        xn_ref[:tm] = _rms(x, g_ref[...]).astype(BF16)
        o_ref[...] = x

    @pl.when((f == 0) & (i == 0))
    def _():
        xs = xs_ref[...]
        xn_ref[tm:] = _rms(xs, g_ref[...]).astype(BF16)
        os_ref[...] = xs

    def swiglu(xn):
        gate = _dot(xn, wg_ref[...].astype(BF16))
        up = _dot(xn, wu_ref[...].astype(BF16))
        return _dot((gate * _sigmoid(gate) * up).astype(BF16), wd_ref[...].astype(BF16))

    @pl.when(i == 0)
    def _():
        y = swiglu(xn_ref[...])
        o_ref[...] += y[:tm]
        os_ref[...] += y[tm:]

    @pl.when(i > 0)
    def _():
        o_ref[...] += swiglu(xn_ref[:tm])

    if final:
        @pl.when(last)
        def _():
            o_ref[...] = _rms(o_ref[...], gfin_ref[...])

        @pl.when(last & (i == 0))
        def _():
            os_ref[...] = _rms(os_ref[...], gfin_ref[...])


def _ffn(x, xs, g, w_gate, w_up, w_down, layer, *, tm, g_final=None, tf=256):
    final = g_final is not None
    m, d = x.shape
    g_last = g_final.reshape(1, d) if final else g[layer]
    ms = xs.shape[0]
    ff = w_gate.shape[2]
    vmem = (2 * (2 * _nbytes((tm, d), F32) + 2 * _nbytes((ms, d), F32) + 3 * _nbytes((d, tf), F32))
            + _nbytes((tm + ms, d), BF16) + 3 * _nbytes((d, tf), BF16) + 4 * _nbytes((tm + ms, tf), F32)
            + 2 * MIB)
    return pl.pallas_call(
        functools.partial(_ffn_kernel, tm=tm, final=final),
        out_shape=(jax.ShapeDtypeStruct((m, d), F32), jax.ShapeDtypeStruct((ms, d), F32)),
        grid=(m // tm, ff // tf),
        in_specs=[pl.BlockSpec((tm, d), lambda i, f: (i, 0)),
                  pl.BlockSpec((ms, d), lambda i, f: (0, 0)),
                  pl.BlockSpec((None, 1, d), lambda i, f: (layer, 0, 0)),
                  pl.BlockSpec((1, d), lambda i, f: (0, 0)),
                  pl.BlockSpec((None, d, tf), lambda i, f: (layer, 0, f)),
                  pl.BlockSpec((None, d, tf), lambda i, f: (layer, 0, f)),
                  pl.BlockSpec((None, tf, d), lambda i, f: (layer, f, 0))],
        out_specs=(pl.BlockSpec((tm, d), lambda i, f: (i, 0)),
                   pl.BlockSpec((ms, d), lambda i, f: (0, 0))),
        scratch_shapes=[pltpu.VMEM((tm + ms, d), BF16)],
        compiler_params=_params(("arbitrary", "arbitrary"), vmem),
        name="ffn",
    )(x, xs, g, g_last, w_gate, w_up, w_down)


def _mixers(q, kv, rest, ssm_w, conv_w, layer, *, batch, cache_t, conv_prefix, ssm_h0, state_layer):
    m = q.shape[0]
    n = m // batch
    q3, kv3, rest3 = (a.reshape(batch, n, a.shape[-1]) for a in (q, kv, rest))
    if cache_t is None:
        o_att = _attn_prompt(q3, kv3)
        y_ssm, h_re, h_im = _ssm(rest3, *ssm_h0, state_layer, ssm_w, layer, chained=True)
        h_re, h_im = h_re[:, SCAN_SEQS - 1], h_im[:, SCAN_SEQS - 1]
    else:
        o_att = _attn_sample(q3, kv3, cache_t, layer)
        y_ssm, h_re, h_im = _ssm(rest.reshape(1, m, rest.shape[-1]), *ssm_h0, state_layer, ssm_w, layer,
                                 chained=False)
        h_re, h_im = h_re[0], h_im[0]
    y_conv, conv_state = _conv(rest3, conv_w, layer, conv_prefix, state_layer)
    mixed = (o_att.reshape(m, ATT_WIDTH), y_ssm.reshape(m, SSM_WIDTH), y_conv.reshape(m, CONV_CH))
    return mixed, conv_state, h_re, h_im


def kernel(x_prompt, x_sample, cache_kv, state_conv, state_ssm_re, state_ssm_im, g_mix_norm, w_in,
           ssm_a_re, ssm_a_im, ssm_log_dt, ssm_b_re, ssm_b_im, ssm_c_re, ssm_c_im, ssm_d, ssm_w_glu,
           ssm_b_glu, conv_w, g_out, w_out, g_ffn_norm, w_gate, w_up, w_down, g_final):
    batch, seq, d_model = x_prompt.shape
    dec_batch, dec_seq, _ = x_sample.shape
    depth, n_groups, n_state = ssm_a_re.shape
    n_heads = ATT_WIDTH // HEAD_DIM
    assert dec_batch == SCAN_SEQS and seq % SCAN_SEQS == 0

    ab_re, ab_im, bb_re, bb_im = _ssm_prep(ssm_a_re, ssm_a_im, ssm_log_dt, ssm_b_re, ssm_b_im)
    cache_t = jnp.transpose(cache_kv, (0, 1, 3, 4, 5, 2))

    row = lambda a: a.reshape(depth, 1, -1)
    ssm_w = (row(ab_re), row(ab_im),
             jnp.concatenate([_block_diag(bb_re), _block_diag(bb_im)], axis=-1).astype(BF16),
             _block_diag(jnp.swapaxes(ssm_c_re, -1, -2)).astype(BF16),
             _block_diag(jnp.swapaxes(ssm_c_im, -1, -2)).astype(BF16),
             row(ssm_d))
    g_mix, g_mixed, g_ffn, b_glu = row(g_mix_norm), row(g_out), row(g_ffn_norm), row(ssm_b_glu)
    state_ssm = (state_ssm_re.reshape(depth, 1, dec_batch, -1), state_ssm_im.reshape(depth, 1, dec_batch, -1))

    xp = x_prompt.reshape(batch * seq, d_model)
    xs = x_sample.reshape(dec_batch * dec_seq, d_model)
    zero_state = jnp.zeros((1, batch, SCAN_SEQS, n_groups * n_state), F32)
    zero_prefix = jnp.zeros((1, batch, CONV_K - 1, CONV_CH), F32)
    kvt = jnp.zeros((depth, batch, 2 * ATT_WIDTH, seq), F32)

    outs_p, outs_s = [], []
    for i in range(depth):
        q, kv, rest, kvt, q_s, kv_s, rest_s = _in_proj(xp, xs, g_mix, w_in, kvt, i, tm=2048, seq=seq, tn=256)

        mixed_p, c_state, h_re, h_im = _mixers(
            q, kv, rest, ssm_w, conv_w, i, batch=batch, cache_t=None, conv_prefix=zero_prefix,
            ssm_h0=(zero_state, zero_state), state_layer=0)
        outs_p.append((c_state, h_re.reshape(batch, n_groups, n_state),
                       h_im.reshape(batch, n_groups, n_state)))

        mixed_s, c_state, h_re, h_im = _mixers(
            q_s, kv_s, rest_s, ssm_w, conv_w, i, batch=dec_batch, cache_t=cache_t, conv_prefix=state_conv,
            ssm_h0=state_ssm, state_layer=i)
        outs_s.append((kv_s.reshape(dec_batch, dec_seq, 2, n_heads, HEAD_DIM), c_state,
                       h_re.reshape(dec_batch, n_groups, n_state),
                       h_im.reshape(dec_batch, n_groups, n_state)))

        xp, xs = _out_proj((*mixed_p, xp), (*mixed_s, xs), g_mixed, ssm_w_glu, b_glu, w_out, i, tm=1024)
        xp, xs = _ffn(xp, xs, g_ffn, w_gate, w_up, w_down, i, tm=1024,
                      g_final=g_final if i == depth - 1 else None)

    y_prompt = xp.reshape(batch, seq, d_model)
    y_sample = xs.reshape(dec_batch, dec_seq, d_model)
    kv_prompt = jnp.transpose(kvt.reshape(depth, batch, 2, n_heads, HEAD_DIM, seq), (0, 1, 5, 2, 3, 4))
    n_keep = min(max(ATT_DILATIONS) * ATT_SPAN, seq)
    stack = lambda outs, k: jnp.stack([o[k] for o in outs])
    return (y_prompt, y_sample,
            kv_prompt[:, :, seq - n_keep:], stack(outs_p, 0), stack(outs_p, 1), stack(outs_p, 2),
            stack(outs_s, 0), stack(outs_s, 1), stack(outs_s, 2), stack(outs_s, 3))
```

```python
import functools
import math

import jax
import jax.numpy as jnp
from jax import lax
from jax.experimental import pallas as pl
from jax.experimental.pallas import tpu as pltpu

F32 = jnp.float32
BF16 = jnp.bfloat16

HEAD_DIM = 64
ATT_WIDTH = 1024
SSM_WIDTH = 512
CONV_CH = 512
CONV_K = 3
SSM_GROUP = 16
SSM_STATE = 64
ATT_DILATIONS = (1, 4, 16)
ATT_SPAN = 128
RMS_EPS = 1e-6
NEG = -1e30

LANES = 128
SUBLANES = 8
SCAN_SEQS = SUBLANES
SSM_BLOCK_GROUPS = LANES // SSM_GROUP
SSM_BLOCK_STATE = SSM_BLOCK_GROUPS * SSM_STATE

MIB = 1024 * 1024
V7X_VMEM_BYTES = 64 * MIB


def _params(semantics, vmem_bytes, hbm_result_bytes=None):
    if hbm_result_bytes is not None and hbm_result_bytes >= V7X_VMEM_BYTES // 8:
        vmem_bytes = max(vmem_bytes, V7X_VMEM_BYTES - hbm_result_bytes + MIB)
    return pltpu.CompilerParams(dimension_semantics=semantics, vmem_limit_bytes=int(vmem_bytes))


def _nbytes(shape, dtype):
    return math.prod(shape) * jnp.dtype(dtype).itemsize


def _rms(x, g):
    ms = jnp.mean(x * x, axis=-1, keepdims=True)
    return x * lax.rsqrt(ms + RMS_EPS) * g


def _sigmoid(x):
    return 1.0 / (1.0 + jnp.exp(-x))


def _gelu_tanh(x):
    c = math.sqrt(2.0 / math.pi)
    return 0.5 * x * (1.0 + jnp.tanh(c * (x + 0.044715 * (x * x * x))))


def _dot(a, b):
    return jnp.dot(a, b, preferred_element_type=F32)


def _dot_nt(a, b):
    return lax.dot_general(a, b, (((1,), (1,)), ((), ())), preferred_element_type=F32)


def _in_proj_kernel(x_ref, xs_ref, g_ref, w_ref, kvt_in_ref, q_ref, kv_ref, rest_ref, kvt_ref,
                    qs_ref, kvs_ref, rests_ref, xn_ref, *, tm, tn, nq, nkv):
    del kvt_in_ref
    i, j = pl.program_id(0), pl.program_id(1)

    @pl.when(j == 0)
    def _():
        xn_ref[:tm] = _rms(x_ref[...], g_ref[...]).astype(BF16)

    @pl.when((j == 0) & (i == 0))
    def _():
        xn_ref[tm:] = _rms(xs_ref[...], g_ref[...]).astype(BF16)

    def project(out_ref, outs_ref, col, transposed_ref=None):
        @pl.when(i == 0)
        def _():
            z = _dot(xn_ref[...], w_ref[...].astype(BF16))
            out_ref[...] = z[:tm]
            outs_ref[:, pl.ds(pl.multiple_of(col * tn, tn), tn)] = z[tm:]
            if transposed_ref is not None:
                transposed_ref[...] = z[:tm].T

        @pl.when(i > 0)
        def _():
            z = _dot(xn_ref[:tm], w_ref[...].astype(BF16))
            out_ref[...] = z
            if transposed_ref is not None:
                transposed_ref[...] = z.T

    @pl.when(j < nq)
    def _():
        project(q_ref, qs_ref, j)

    @pl.when((j >= nq) & (j < nq + nkv))
    def _():
        project(kv_ref, kvs_ref, j - nq, kvt_ref)

    @pl.when(j >= nq + nkv)
    def _():
        project(rest_ref, rests_ref, j - nq - nkv)


def _in_proj(x, xs, g, w, kvt_stack, layer, *, tm, seq, tn=512):
    m, k = x.shape
    ms = xs.shape[0]
    n = w.shape[2]
    nq, nkv = ATT_WIDTH // tn, 2 * ATT_WIDTH // tn
    nrest = n // tn - nq - nkv
    assert seq % tm == 0 and tm % LANES == 0
    assert kvt_stack.shape[1:] == (m // seq, nkv * tn, seq)
    tiles_per_seq = seq // tm
    vmem = (_nbytes((tm, k), F32)
            + 2 * (_nbytes((ms, k), F32) + _nbytes((k, tn), F32) + 4 * _nbytes((tm, tn), F32)
                   + _nbytes((ms, n), F32))
            + _nbytes((tm + ms, k), BF16) + _nbytes((k, tn), BF16) + 3 * _nbytes((tm + ms, tn), F32)
            + 2 * MIB)
    kv_col = lambda j: jnp.clip(j - nq, 0, nkv - 1)
    whole = lambda cols: pl.BlockSpec((ms, cols), lambda i, j: (0, 0))
    return pl.pallas_call(
        functools.partial(_in_proj_kernel, tm=tm, tn=tn, nq=nq, nkv=nkv),
        out_shape=(jax.ShapeDtypeStruct((m, nq * tn), F32),
                   jax.ShapeDtypeStruct((m, nkv * tn), F32),
                   jax.ShapeDtypeStruct((m, nrest * tn), F32),
                   jax.ShapeDtypeStruct(kvt_stack.shape, F32),
                   jax.ShapeDtypeStruct((ms, nq * tn), F32),
                   jax.ShapeDtypeStruct((ms, nkv * tn), F32),
                   jax.ShapeDtypeStruct((ms, nrest * tn), F32)),
        grid=(m // tm, n // tn),
        in_specs=[pl.BlockSpec((tm, k), lambda i, j: (i, 0), pipeline_mode=pl.Buffered(1)),
                  whole(k),
                  pl.BlockSpec((None, 1, k), lambda i, j: (layer, 0, 0)),
                  pl.BlockSpec((None, k, tn), lambda i, j: (layer, 0, j)),
                  pl.BlockSpec(memory_space=pl.ANY)],
        out_specs=(pl.BlockSpec((tm, tn), lambda i, j: (i, jnp.minimum(j, nq - 1))),
                   pl.BlockSpec((tm, tn), lambda i, j: (i, kv_col(j))),
                   pl.BlockSpec((tm, tn), lambda i, j: (i, jnp.clip(j - nq - nkv, 0, nrest - 1))),
                   pl.BlockSpec((None, None, tn, tm),
                                lambda i, j: (layer, i // tiles_per_seq, kv_col(j), i % tiles_per_seq)),
                   whole(nq * tn), whole(nkv * tn), whole(nrest * tn)),
        scratch_shapes=[pltpu.VMEM((tm + ms, k), BF16)],
        input_output_aliases={4: 3},
        compiler_params=_params(("arbitrary", "arbitrary"), vmem),
        name="in_proj",
    )(x, xs, g, w, kvt_stack)


def _attn_prompt_kernel(q_ref, k_ref, v_ref, o_ref, ob_ref, mb_ref, lb_ref, bias_ref):
    s_len = q_ref.shape[0]
    blk = ATT_SPAN
    lane = lax.broadcasted_iota(jnp.int32, (blk, LANES), 1)
    head0 = lane < HEAD_DIM
    scale = HEAD_DIM ** -0.5

    qi = lax.broadcasted_iota(jnp.int32, (2 * blk, 3 * blk), 0) & (blk - 1)
    kj = lax.broadcasted_iota(jnp.int32, (2 * blk, 3 * blk), 1)
    dist = jnp.where(kj < blk, qi - kj, qi + 2 * blk - kj)
    bias_ref[...] = jnp.where((dist >= 0) & (dist <= ATT_SPAN), 0.0, NEG)

    def block(p, d, q0, k0=None):
        def rows(start, n):
            return pl.ds(start, n, stride=d) if d > 1 else pl.ds(start, n)

        qt = q_ref[rows(q0, blk), :] * scale
        qq = jnp.concatenate([jnp.where(head0, qt, 0.0), jnp.where(head0, 0.0, qt)], axis=0).astype(BF16)
        if k0 is None:
            key_rows, bias = rows(q0, blk), bias_ref[:, :blk]
        else:
            key_rows, bias = rows(k0, 2 * blk), bias_ref[:, blk:]
        k = k_ref[key_rows, :].astype(BF16)
        v = v_ref[key_rows, :].astype(BF16)
        s = _dot_nt(qq, k) + bias
        m = jnp.max(s, axis=-1, keepdims=True)
        e = jnp.exp(s - m)
        l = jnp.sum(e, axis=-1, keepdims=True)
        o = _dot(e.astype(BF16), v)
        m = jnp.broadcast_to(m, (2 * blk, LANES))
        l = jnp.broadcast_to(l, (2 * blk, LANES))
        ob_ref[p, rows(q0, blk), :] = jnp.where(head0, o[:blk], o[blk:])
        mb_ref[p, rows(q0, blk), :] = jnp.where(head0, m[:blk], m[blk:])
        lb_ref[p, rows(q0, blk), :] = jnp.where(head0, l[:blk], l[blk:])

    for p, d in enumerate(ATT_DILATIONS):
        nblk = s_len // (d * blk)
        for r in range(d):
            for b in range(nblk):
                block(p, d, r + d * blk * b, None if b == 0 else r + d * blk * (b - 1))

    def merge(c, carry):
        rows = pl.ds(pl.multiple_of(c * blk, blk), blk)
        m0, m1, m2 = mb_ref[0, rows, :], mb_ref[1, rows, :], mb_ref[2, rows, :]
        m = jnp.maximum(jnp.maximum(m0, m1), m2)
        e0, e1, e2 = jnp.exp(m0 - m), jnp.exp(m1 - m), jnp.exp(m2 - m)
        num = e0 * ob_ref[0, rows, :] + e1 * ob_ref[1, rows, :] + e2 * ob_ref[2, rows, :]
        den = e0 * lb_ref[0, rows, :] + e1 * lb_ref[1, rows, :] + e2 * lb_ref[2, rows, :]
        o_ref[rows, :] = num / den
        return carry

    lax.fori_loop(0, s_len // blk, merge, 0)


def _attn_prompt(q, kv):
    b, s, _ = q.shape
    assert s % (max(ATT_DILATIONS) * ATT_SPAN) == 0
    npair = ATT_WIDTH // LANES
    blk_bytes = _nbytes((s, LANES), F32)
    vmem = 2 * 4 * blk_bytes + 3 * len(ATT_DILATIONS) * blk_bytes + 4 * MIB
    return pl.pallas_call(
        _attn_prompt_kernel,
        out_shape=jax.ShapeDtypeStruct((b, s, ATT_WIDTH), F32),
        grid=(b, npair),
        in_specs=[pl.BlockSpec((None, s, LANES), lambda i, j: (i, 0, j)),
                  pl.BlockSpec((None, s, LANES), lambda i, j: (i, 0, j)),
                  pl.BlockSpec((None, s, LANES), lambda i, j: (i, 0, npair + j))],
        out_specs=pl.BlockSpec((None, s, LANES), lambda i, j: (i, 0, j)),
        scratch_shapes=[pltpu.VMEM((len(ATT_DILATIONS), s, LANES), F32),
                        pltpu.VMEM((len(ATT_DILATIONS), s, LANES), F32),
                        pltpu.VMEM((len(ATT_DILATIONS), s, LANES), F32),
                        pltpu.VMEM((2 * ATT_SPAN, 3 * ATT_SPAN), F32)],
        compiler_params=_params(("parallel", "parallel"), vmem, _nbytes((b, s, ATT_WIDTH), F32)),
        name="attn_prompt",
    )(q, kv, kv)


def _attn_sample_kernel(q_ref, kn_ref, vn_ref, kct_ref, vct_ref, o_ref):
    t_len = q_ref.shape[0]
    w_len = kct_ref.shape[-1]
    pairs = q_ref.shape[1] // LANES
    pad = LANES
    zpad = jnp.zeros((pad - t_len, LANES), F32)

    def multiplicity(dist):
        c = jnp.zeros(dist.shape, F32)
        for d in ATT_DILATIONS:
            ok = (dist >= 0) & ((dist & (d - 1)) == 0) & (dist <= ATT_SPAN * d)
            c = c + jnp.where(ok, 1.0, 0.0)
        return c

    tc = lax.broadcasted_iota(jnp.int32, (t_len, w_len), 0)
    jc = lax.broadcasted_iota(jnp.int32, (t_len, w_len), 1)
    cnt_c = multiplicity(w_len + tc - jc)
    tn = lax.broadcasted_iota(jnp.int32, (t_len, pad), 0)
    jn = lax.broadcasted_iota(jnp.int32, (t_len, pad), 1)
    cnt_n = multiplicity(jnp.where(jn < t_len, tn - jn, -1))

    lane = lax.broadcasted_iota(jnp.int32, (t_len, LANES), 1)
    head0 = lane < HEAD_DIM
    for p in range(pairs):
        lanes = slice(p * LANES, (p + 1) * LANES)
        q = q_ref[:, lanes] * (HEAD_DIM ** -0.5)
        kn = jnp.concatenate([kn_ref[:, lanes], zpad], axis=0).astype(BF16)
        vn = jnp.concatenate([vn_ref[:, lanes], zpad], axis=0).astype(BF16)
        kct = kct_ref[2 * p:2 * p + 2].reshape(LANES, w_len).astype(BF16)
        vct = vct_ref[2 * p:2 * p + 2].reshape(LANES, w_len).astype(BF16)
        outs = []
        for h in range(2):
            hm = head0 if h == 0 else jnp.logical_not(head0)
            qh = jnp.where(hm, q, 0.0).astype(BF16)
            sc = jnp.where(cnt_c > 0, _dot(qh, kct), NEG)
            sn = jnp.where(cnt_n > 0, _dot_nt(qh, kn), NEG)
            m = jnp.maximum(jnp.max(sc, axis=-1, keepdims=True), jnp.max(sn, axis=-1, keepdims=True))
            ec = cnt_c * jnp.exp(sc - m)
            en = cnt_n * jnp.exp(sn - m)
            l = jnp.sum(ec, axis=-1, keepdims=True) + jnp.sum(en, axis=-1, keepdims=True)
            pv = _dot_nt(ec.astype(BF16), vct) + _dot(en.astype(BF16), vn)
            outs.append(pv / l)
        o_ref[:, lanes] = jnp.where(head0, outs[0], outs[1])


def _attn_sample(q, kv, cache_t, layer, *, pairs=4):
    b, t, _ = q.shape
    w = cache_t.shape[-1]
    assert all(d & (d - 1) == 0 for d in ATT_DILATIONS) and t <= LANES
    width = pairs * LANES
    nstep = ATT_WIDTH // width
    vmem = 2 * 2 * _nbytes((width, w), F32) + 10 * _nbytes((LANES, w), F32) * pairs + 4 * MIB
    new_spec = lambda off: pl.BlockSpec((None, t, width), lambda i, j: (i, 0, off + j))
    buf_spec = lambda c: pl.BlockSpec((None, None, None, 2 * pairs, HEAD_DIM, w),
                                      lambda i, j: (layer, i, c, j, 0, 0))
    return pl.pallas_call(
        _attn_sample_kernel,
        out_shape=jax.ShapeDtypeStruct((b, t, ATT_WIDTH), F32),
        grid=(b, nstep),
        in_specs=[new_spec(0), new_spec(0), new_spec(nstep), buf_spec(0), buf_spec(1)],
        out_specs=new_spec(0),
        compiler_params=_params(("parallel", "parallel"), vmem),
        name="attn_sample",
    )(q, kv, kv, cache_t, cache_t)


def _ssm_prep_kernel(are_ref, aim_ref, ldt_ref, bre_ref, bim_ref, abre_ref, abim_ref, bbre_ref, bbim_ref):
    a_re, a_im = are_ref[...], aim_ref[...]
    dt = jnp.exp(ldt_ref[...])
    mag = jnp.exp(a_re * dt)
    ang = a_im * dt
    ab_re, ab_im = mag * jnp.cos(ang), mag * jnp.sin(ang)
    den = a_re * a_re + a_im * a_im
    nr, ni = ab_re - 1.0, ab_im
    cf_re = (nr * a_re + ni * a_im) / den
    cf_im = (ni * a_re - nr * a_im) / den
    abre_ref[...] = ab_re
    abim_ref[...] = ab_im
    for c in range(SSM_GROUP):
        b_re, b_im = bre_ref[c], bim_ref[c]
        bbre_ref[c] = cf_re * b_re - cf_im * b_im
        bbim_ref[c] = cf_re * b_im + cf_im * b_re


def _ssm_prep(a_re, a_im, log_dt, b_re, b_im):
    depth, g, n = a_re.shape
    rows = depth * g
    flat = lambda a: a.reshape(rows, n)
    chan_major = lambda b: jnp.transpose(b.reshape(rows, n, SSM_GROUP), (2, 0, 1))
    ab_re, ab_im, bb_re, bb_im = pl.pallas_call(
        _ssm_prep_kernel,
        out_shape=(jax.ShapeDtypeStruct((rows, n), F32), jax.ShapeDtypeStruct((rows, n), F32),
                   jax.ShapeDtypeStruct((SSM_GROUP, rows, n), F32),
                   jax.ShapeDtypeStruct((SSM_GROUP, rows, n), F32)),
        name="ssm_prep",
    )(flat(a_re), flat(a_im), log_dt.reshape(rows, 1), chan_major(b_re), chan_major(b_im))
    unflat = lambda b: jnp.transpose(b, (1, 0, 2)).reshape(depth, g, SSM_GROUP, n)
    return ab_re.reshape(depth, g, n), ab_im.reshape(depth, g, n), unflat(bb_re), unflat(bb_im)


def _block_diag(m):
    *lead, g, r, c = m.shape
    nb = g // SSM_BLOCK_GROUPS
    m = m.reshape(*lead, nb, SSM_BLOCK_GROUPS, r, c)
    eye = jnp.eye(SSM_BLOCK_GROUPS, dtype=m.dtype)
    out = m[..., :, :, None, :] * eye[:, None, :, None]
    return out.reshape(*lead, nb, SSM_BLOCK_GROUPS * r, SSM_BLOCK_GROUPS * c)


def _ssm_kernel(u_ref, h0re_ref, h0im_ref, are_ref, aim_ref, bb_ref, cre_ref, cim_ref, d_ref,
                y_ref, hre_ref, him_ref, x_scr, up_scr, *, steps, chained):
    ns = SSM_BLOCK_STATE
    nrows = steps * SCAN_SEQS
    chunk = min(nrows, 256)

    nchunks = nrows // chunk
    chunk_steps = chunk // SCAN_SEQS
    unroll = min(steps, 8)

    def permute_in(t, c):
        up_scr[pl.ds(pl.multiple_of(t * SCAN_SEQS, SCAN_SEQS), SCAN_SEQS), :] = (
            u_ref[pl.ds(t, SCAN_SEQS, stride=steps), :])
        return c
    lax.fori_loop(0, steps, permute_in, 0, unroll=unroll)

    a_re = jnp.broadcast_to(are_ref[...], (SCAN_SEQS, ns))
    a_im = jnp.broadcast_to(aim_ref[...], (SCAN_SEQS, ns))

    def input_map(c):
        rows = pl.ds(c * chunk, chunk)
        x_scr[rows, :] = _dot(up_scr[rows, :].astype(BF16), bb_ref[...])

    def scan_chunk(c, h, store):
        h_re, h_im = h
        for t in range(c * chunk_steps, (c + 1) * chunk_steps):
            rows = pl.ds(t * SCAN_SEQS, SCAN_SEQS)
            h_re, h_im = (a_re * h_re - a_im * h_im + x_scr[rows, :ns],
                          a_re * h_im + a_im * h_re + x_scr[rows, ns:])
            if store:
                x_scr[rows, :ns] = h_re
                x_scr[rows, ns:] = h_im
        return h_re, h_im

    def output_map(c):
        rows = pl.ds(c * chunk, chunk)
        y = (_dot(x_scr[rows, :ns].astype(BF16), cre_ref[...])
             - _dot(x_scr[rows, ns:].astype(BF16), cim_ref[...])
             + d_ref[...] * up_scr[rows, :])
        up_scr[rows, :] = _gelu_tanh(y)

    init = (h0re_ref[...], h0im_ref[...])
    input_map(0)
    if chained:
        assert steps & (steps - 1) == 0
        e_re, e_im = init
        for c in range(nchunks):
            if c + 1 < nchunks:
                input_map(c + 1)
            e_re, e_im = scan_chunk(c, (e_re, e_im), store=False)
        p_re, p_im = a_re, a_im
        for _ in range(steps.bit_length() - 1):
            p_re, p_im = p_re * p_re - p_im * p_im, 2.0 * p_re * p_im
        sub = lax.broadcasted_iota(jnp.int32, (SCAN_SEQS, ns), 0)
        shift = 1
        while shift < SCAN_SEQS:
            r_re = pltpu.roll(e_re, shift, axis=0)
            r_im = pltpu.roll(e_im, shift, axis=0)
            keep = sub >= shift
            e_re, e_im = (e_re + jnp.where(keep, p_re * r_re - p_im * r_im, 0.0),
                          e_im + jnp.where(keep, p_re * r_im + p_im * r_re, 0.0))
            p_re, p_im = p_re * p_re - p_im * p_im, 2.0 * p_re * p_im
            shift *= 2
        first = sub == 0
        init = (init[0] + jnp.where(first, 0.0, pltpu.roll(e_re, 1, axis=0)),
                init[1] + jnp.where(first, 0.0, pltpu.roll(e_im, 1, axis=0)))
    else:
        for c in range(1, nchunks):
            input_map(c)
    h = init
    for c in range(nchunks):
        h = scan_chunk(c, h, store=True)
        if c > 0:
            output_map(c - 1)
    output_map(nchunks - 1)
    hre_ref[...] = h[0]
    him_ref[...] = h[1]

    def permute_out(t, c):
        y_ref[pl.ds(t, SCAN_SEQS, stride=steps), :] = (
            up_scr[pl.ds(pl.multiple_of(t * SCAN_SEQS, SCAN_SEQS), SCAN_SEQS), :])
        return c
    lax.fori_loop(0, steps, permute_out, 0, unroll=unroll)


def _ssm(u, h0_re, h0_im, h0_layer, ssm_w, layer, *, chained):
    nb, rows, _ = u.shape
    steps = rows // SCAN_SEQS
    nblk = SSM_WIDTH // LANES
    ns = SSM_BLOCK_STATE
    vmem = (2 * 2 * _nbytes((rows, LANES), F32) + _nbytes((rows, 2 * ns), F32)
            + _nbytes((rows, LANES), F32) + 4 * _nbytes((min(rows, 256), 2 * ns), F32) + 8 * MIB)
    h0_spec = pl.BlockSpec((None, None, SCAN_SEQS, ns), lambda i, j: (h0_layer, i, 0, j))
    state_spec = pl.BlockSpec((None, SCAN_SEQS, ns), lambda i, j: (i, 0, j))
    return pl.pallas_call(
        functools.partial(_ssm_kernel, steps=steps, chained=chained),
        out_shape=(jax.ShapeDtypeStruct((nb, rows, SSM_WIDTH), F32),
                   jax.ShapeDtypeStruct(h0_re.shape[1:], F32),
                   jax.ShapeDtypeStruct(h0_im.shape[1:], F32)),
        grid=(nb, nblk),
        in_specs=[pl.BlockSpec((None, rows, LANES), lambda i, j: (i, 0, j)),
                  h0_spec, h0_spec,
                  pl.BlockSpec((None, 1, ns), lambda i, j: (layer, 0, j)),
                  pl.BlockSpec((None, 1, ns), lambda i, j: (layer, 0, j)),
                  pl.BlockSpec((None, None, LANES, 2 * ns), lambda i, j: (layer, j, 0, 0)),
                  pl.BlockSpec((None, None, ns, LANES), lambda i, j: (layer, j, 0, 0)),
                  pl.BlockSpec((None, None, ns, LANES), lambda i, j: (layer, j, 0, 0)),
                  pl.BlockSpec((None, 1, LANES), lambda i, j: (layer, 0, j))],
        out_specs=(pl.BlockSpec((None, rows, LANES), lambda i, j: (i, 0, j)), state_spec, state_spec),
        scratch_shapes=[pltpu.VMEM((rows, 2 * ns), F32), pltpu.VMEM((rows, LANES), F32)],
        compiler_params=_params(("parallel", "parallel"), vmem,
                                _nbytes((nb, rows, SSM_WIDTH), F32)),
        name="ssm_chained" if chained else "ssm_step",
    )(u, h0_re, h0_im, *ssm_w)


def _conv_kernel(h_ref, gb_ref, gc_ref, w_ref, pre_ref, y_ref, st_ref):
    n = h_ref.shape[0]
    u = gc_ref[...] * h_ref[...]
    row = lax.broadcasted_iota(jnp.int32, u.shape, 0)
    pre = pre_ref[...]
    p0, p1 = pre[0:1, :], pre[1:2, :]
    u1 = jnp.where(row == 0, p1, pltpu.roll(u, 1, axis=0))
    u2 = jnp.where(row == 0, p0, jnp.where(row == 1, p1, pltpu.roll(u, 2, axis=0)))
    w = w_ref[...]
    y = w[0:1, :] * u2 + w[1:2, :] * u1 + w[2:3, :] * u
    y_ref[...] = gb_ref[...] * y
    st_ref[...] = u[n - (CONV_K - 1):, :]


def _conv(rest, w, layer, prefix, prefix_layer):
    b, n, _ = rest.shape
    assert n >= CONV_K - 1
    nblk = CONV_CH // LANES
    off = SSM_WIDTH // LANES
    col = lambda part: pl.BlockSpec((None, n, LANES), lambda i, j: (i, 0, off + part * nblk + j))
    vmem = 2 * 4 * _nbytes((n, LANES), F32) + 8 * _nbytes((n, LANES), F32) + 4 * MIB
    return pl.pallas_call(
        _conv_kernel,
        out_shape=(jax.ShapeDtypeStruct((b, n, CONV_CH), F32),
                   jax.ShapeDtypeStruct((b, CONV_K - 1, CONV_CH), F32)),
        grid=(b, nblk),
        in_specs=[col(0), col(1), col(2),
                  pl.BlockSpec((None, CONV_K, LANES), lambda i, j: (layer, 0, j)),
                  pl.BlockSpec((None, None, CONV_K - 1, LANES), lambda i, j: (prefix_layer, i, 0, j))],
        out_specs=(pl.BlockSpec((None, n, LANES), lambda i, j: (i, 0, j)),
                   pl.BlockSpec((None, CONV_K - 1, LANES), lambda i, j: (i, 0, j))),
        compiler_params=_params(("parallel", "parallel"), vmem,
                                _nbytes((b, n, CONV_CH), F32)),
        name="conv",
    )(rest, rest, rest, w, prefix)


def _out_proj_kernel(oatt_ref, yssm_ref, yconv_ref, x_ref, oatts_ref, yssms_ref, yconvs_ref, xs_ref,
                     g_ref, wglu_ref, bglu_ref, w_ref, o_ref, os_ref, mix_ref, *, tm, tn):
    i, j = pl.program_id(0), pl.program_id(1)
    a0, a1 = ATT_WIDTH, ATT_WIDTH + SSM_WIDTH

    def mixed(rows, o_att, y_ssm, y_conv):
        g = g_ref[...]
        mix_ref[rows, :a0] = _rms(o_att, g[:, :a0]).astype(BF16)
        y_ssm = y_ssm * _sigmoid(_dot(y_ssm.astype(BF16), wglu_ref[...].astype(BF16)) + bglu_ref[...])
        mix_ref[rows, a0:a1] = _rms(y_ssm, g[:, a0:a1]).astype(BF16)
        mix_ref[rows, a1:] = _rms(y_conv, g[:, a1:]).astype(BF16)

    @pl.when(j == 0)
    def _():
        mixed(slice(0, tm), oatt_ref[...], yssm_ref[...], yconv_ref[...])

    @pl.when((j == 0) & (i == 0))
    def _():
        mixed(slice(tm, None), oatts_ref[...], yssms_ref[...], yconvs_ref[...])

    @pl.when(i == 0)
    def _():
        z = _dot(mix_ref[...], w_ref[...].astype(BF16))
        o_ref[...] = x_ref[...] + z[:tm]
        cols = pl.ds(pl.multiple_of(j * tn, tn), tn)
        os_ref[:, cols] = xs_ref[:, cols] + z[tm:]

    @pl.when(i > 0)
    def _():
        o_ref[...] = x_ref[...] + _dot(mix_ref[:tm], w_ref[...].astype(BF16))


def _out_proj(prompt, sample, g, w_glu, b_glu, w, layer, *, tm, tn=512):
    m, d = prompt[3].shape
    ms = sample[3].shape[0]
    k = w.shape[1]
    vmem = (2 * (_nbytes((tm, k), F32) + 2 * _nbytes((tm, tn), F32) + 3 * _nbytes((ms, k), F32)
                 + _nbytes((k, tn), F32) + _nbytes((SSM_WIDTH, SSM_WIDTH), F32))
            + _nbytes((tm + ms, k), BF16) + _nbytes((k, tn), BF16) + 2 * _nbytes((tm + ms, k), F32)
            + 2 * MIB)
    row = lambda c: pl.BlockSpec(
        (tm, c), lambda i, j: (jnp.minimum(jnp.where(j > 0, i + 1, i), m // tm - 1), 0))
    whole = lambda c: pl.BlockSpec((ms, c), lambda i, j: (0, 0))
    layered = lambda r, c: pl.BlockSpec((None, r, c), lambda i, j: (layer, 0, 0))
    return pl.pallas_call(
        functools.partial(_out_proj_kernel, tm=tm, tn=tn),
        out_shape=(jax.ShapeDtypeStruct((m, d), F32), jax.ShapeDtypeStruct((ms, d), F32)),
        grid=(m // tm, d // tn),
        in_specs=[row(ATT_WIDTH), row(SSM_WIDTH), row(CONV_CH),
                  pl.BlockSpec((tm, tn), lambda i, j: (i, j)),
                  whole(ATT_WIDTH), whole(SSM_WIDTH), whole(CONV_CH), whole(d),
                  layered(1, k),
                  layered(SSM_WIDTH, SSM_WIDTH),
                  layered(1, SSM_WIDTH),
                  pl.BlockSpec((None, k, tn), lambda i, j: (layer, 0, j))],
        out_specs=(pl.BlockSpec((tm, tn), lambda i, j: (i, j)), whole(d)),
        scratch_shapes=[pltpu.VMEM((tm + ms, k), BF16)],
        compiler_params=_params(("arbitrary", "arbitrary"), vmem),
        name="out_proj",
    )(*prompt, *sample, g, w_glu, b_glu, w)


def _ffn_kernel(x_ref, xs_ref, g_ref, gfin_ref, wg_ref, wu_ref, wd_ref, o_ref, os_ref, xn_ref, *,
                tm, final):
    i, f = pl.program_id(0), pl.program_id(1)
    last = f == pl.num_programs(1) - 1

    @pl.when(f == 0)
    def _():
        x = x_ref[...]
        xn_ref[:tm] = _rms(x, g_ref[...]).astype(BF16)
        o_ref[...] = x

    @pl.when((f == 0) & (i == 0))
    def _():
        xs = xs_ref[...]
        xn_ref[tm:] = _rms(xs, g_ref[...]).astype(BF16)
        os_ref[...] = xs

    def swiglu(xn):
        gate = _dot(xn, wg_ref[...].astype(BF16))
        up = _dot(xn, wu_ref[...].astype(BF16))
        return _dot((gate * _sigmoid(gate) * up).astype(BF16), wd_ref[...].astype(BF16))

    @pl.when(i == 0)
    def _():
        y = swiglu(xn_ref[...])
        o_ref[...] += y[:tm]
        os_ref[...] += y[tm:]

    @pl.when(i > 0)
    def _():
        o_ref[...] += swiglu(xn_ref[:tm])

    if final:
        @pl.when(last)
        def _():
            o_ref[...] = _rms(o_ref[...], gfin_ref[...])

        @pl.when(last & (i == 0))
        def _():
            os_ref[...] = _rms(os_ref[...], gfin_ref[...])


def _ffn(x, xs, g, w_gate, w_up, w_down, layer, *, tm, g_final=None, tf=512):
    final = g_final is not None
    m, d = x.shape
    g_last = g_final.reshape(1, d) if final else g[layer]
    ms = xs.shape[0]
    ff = w_gate.shape[2]
    vmem = (_nbytes((tm, d), F32)
            + 2 * (_nbytes((tm, d), F32) + 2 * _nbytes((ms, d), F32) + 3 * _nbytes((d, tf), F32))
            + _nbytes((tm + ms, d), BF16) + 2 * _nbytes((tm + ms, tf), F32) + 2 * MIB)
    return pl.pallas_call(
        functools.partial(_ffn_kernel, tm=tm, final=final),
        out_shape=(jax.ShapeDtypeStruct((m, d), F32), jax.ShapeDtypeStruct((ms, d), F32)),
        grid=(m // tm, ff // tf),
        in_specs=[pl.BlockSpec((tm, d), lambda i, f: (i, 0), pipeline_mode=pl.Buffered(1)),
                  pl.BlockSpec((ms, d), lambda i, f: (0, 0)),
                  pl.BlockSpec((None, 1, d), lambda i, f: (layer, 0, 0)),
                  pl.BlockSpec((1, d), lambda i, f: (0, 0)),
                  pl.BlockSpec((None, d, tf), lambda i, f: (layer, 0, f)),
                  pl.BlockSpec((None, d, tf), lambda i, f: (layer, 0, f)),
                  pl.BlockSpec((None, tf, d), lambda i, f: (layer, f, 0))],
        out_specs=(pl.BlockSpec((tm, d), lambda i, f: (i, 0)),
                   pl.BlockSpec((ms, d), lambda i, f: (0, 0))),
        scratch_shapes=[pltpu.VMEM((tm + ms, d), BF16)],
        compiler_params=_params(("arbitrary", "arbitrary"), vmem),
        name="ffn",
    )(x, xs, g, g_last, w_gate, w_up, w_down)


def _mixers(q, kv, rest, ssm_w, conv_w, layer, *, batch, cache_t, conv_prefix, ssm_h0, state_layer):
    m = q.shape[0]
    n = m // batch
    q3, kv3, rest3 = (a.reshape(batch, n, a.shape[-1]) for a in (q, kv, rest))
    if cache_t is None:
        o_att = _attn_prompt(q3, kv3)
        y_ssm, h_re, h_im = _ssm(rest3, *ssm_h0, state_layer, ssm_w, layer, chained=True)
        h_re, h_im = h_re[:, SCAN_SEQS - 1], h_im[:, SCAN_SEQS - 1]
    else:
        o_att = _attn_sample(q3, kv3, cache_t, layer)
        y_ssm, h_re, h_im = _ssm(rest.reshape(1, m, rest.shape[-1]), *ssm_h0, state_layer, ssm_w, layer,
                                 chained=False)
        h_re, h_im = h_re[0], h_im[0]
    y_conv, conv_state = _conv(rest3, conv_w, layer, conv_prefix, state_layer)
    mixed = (o_att.reshape(m, ATT_WIDTH), y_ssm.reshape(m, SSM_WIDTH), y_conv.reshape(m, CONV_CH))
    return mixed, conv_state, h_re, h_im


def kernel(x_prompt, x_sample, cache_kv, state_conv, state_ssm_re, state_ssm_im, g_mix_norm, w_in,
           ssm_a_re, ssm_a_im, ssm_log_dt, ssm_b_re, ssm_b_im, ssm_c_re, ssm_c_im, ssm_d, ssm_w_glu,
           ssm_b_glu, conv_w, g_out, w_out, g_ffn_norm, w_gate, w_up, w_down, g_final):
    batch, seq, d_model = x_prompt.shape
    dec_batch, dec_seq, _ = x_sample.shape
    depth, n_groups, n_state = ssm_a_re.shape
    n_heads = ATT_WIDTH // HEAD_DIM
    assert dec_batch == SCAN_SEQS and seq % SCAN_SEQS == 0

    ab_re, ab_im, bb_re, bb_im = _ssm_prep(ssm_a_re, ssm_a_im, ssm_log_dt, ssm_b_re, ssm_b_im)
    cache_t = jnp.transpose(cache_kv, (0, 1, 3, 4, 5, 2))

    row = lambda a: a.reshape(depth, 1, -1)
    ssm_w = (row(ab_re), row(ab_im),
             jnp.concatenate([_block_diag(bb_re), _block_diag(bb_im)], axis=-1).astype(BF16),
             _block_diag(jnp.swapaxes(ssm_c_re, -1, -2)).astype(BF16),
             _block_diag(jnp.swapaxes(ssm_c_im, -1, -2)).astype(BF16),
             row(ssm_d))
    g_mix, g_mixed, g_ffn, b_glu = row(g_mix_norm), row(g_out), row(g_ffn_norm), row(ssm_b_glu)
    state_ssm = (state_ssm_re.reshape(depth, 1, dec_batch, -1), state_ssm_im.reshape(depth, 1, dec_batch, -1))

    xp = x_prompt.reshape(batch * seq, d_model)
    xs = x_sample.reshape(dec_batch * dec_seq, d_model)
    zero_state = jnp.zeros((1, batch, SCAN_SEQS, n_groups * n_state), F32)
    zero_prefix = jnp.zeros((1, batch, CONV_K - 1, CONV_CH), F32)
    kvt = jnp.zeros((depth, batch, 2 * ATT_WIDTH, seq), F32)

    outs_p, outs_s = [], []
    for i in range(depth):
        q, kv, rest, kvt, q_s, kv_s, rest_s = _in_proj(xp, xs, g_mix, w_in, kvt, i, tm=2048, seq=seq, tn=256)

        mixed_p, c_state, h_re, h_im = _mixers(
            q, kv, rest, ssm_w, conv_w, i, batch=batch, cache_t=None, conv_prefix=zero_prefix,
            ssm_h0=(zero_state, zero_state), state_layer=0)
        outs_p.append((c_state, h_re.reshape(batch, n_groups, n_state),
                       h_im.reshape(batch, n_groups, n_state)))

        mixed_s, c_state, h_re, h_im = _mixers(
            q_s, kv_s, rest_s, ssm_w, conv_w, i, batch=dec_batch, cache_t=cache_t, conv_prefix=state_conv,
            ssm_h0=state_ssm, state_layer=i)
        outs_s.append((kv_s.reshape(dec_batch, dec_seq, 2, n_heads, HEAD_DIM), c_state,
                       h_re.reshape(dec_batch, n_groups, n_state),
                       h_im.reshape(dec_batch, n_groups, n_state)))

        xp, xs = _out_proj((*mixed_p, xp), (*mixed_s, xs), g_mixed, ssm_w_glu, b_glu, w_out, i, tm=1024)
        xp, xs = _ffn(xp, xs, g_ffn, w_gate, w_up, w_down, i, tm=1024,
                      g_final=g_final if i == depth - 1 else None)

    y_prompt = xp.reshape(batch, seq, d_model)
    y_sample = xs.reshape(dec_batch, dec_seq, d_model)
    kv_prompt = jnp.transpose(kvt.reshape(depth, batch, 2, n_heads, HEAD_DIM, seq), (0, 1, 5, 2, 3, 4))
    n_keep = min(max(ATT_DILATIONS) * ATT_SPAN, seq)
    stack = lambda outs, k: jnp.stack([o[k] for o in outs])
    return (y_prompt, y_sample,
            kv_prompt[:, :, seq - n_keep:], stack(outs_p, 0), stack(outs_p, 1), stack(outs_p, 2),
            stack(outs_s, 0), stack(outs_s, 1), stack(outs_s, 2), stack(outs_s, 3))
```

```python
import functools
import math

import jax
import jax.numpy as jnp
from jax import lax
from jax.experimental import pallas as pl
from jax.experimental.pallas import tpu as pltpu

F32 = jnp.float32
BF16 = jnp.bfloat16

HEAD_DIM = 64
ATT_WIDTH = 1024
SSM_WIDTH = 512
CONV_CH = 512
CONV_K = 3
SSM_GROUP = 16
SSM_STATE = 64
ATT_DILATIONS = (1, 4, 16)
ATT_SPAN = 128
RMS_EPS = 1e-6
NEG = -1e30

LANES = 128
SUBLANES = 8
SCAN_SEQS = SUBLANES
SSM_BLOCK_GROUPS = LANES // SSM_GROUP
SSM_BLOCK_STATE = SSM_BLOCK_GROUPS * SSM_STATE

MIB = 1024 * 1024
V7X_VMEM_BYTES = 64 * MIB


def _params(semantics, vmem_bytes, hbm_result_bytes=None):
    if hbm_result_bytes is not None and hbm_result_bytes >= V7X_VMEM_BYTES // 8:
        vmem_bytes = max(vmem_bytes, V7X_VMEM_BYTES - hbm_result_bytes + MIB)
    return pltpu.CompilerParams(dimension_semantics=semantics, vmem_limit_bytes=int(vmem_bytes))


def _nbytes(shape, dtype):
    return math.prod(shape) * jnp.dtype(dtype).itemsize


def _rms(x, g):
    ms = jnp.mean(x * x, axis=-1, keepdims=True)
    return x * lax.rsqrt(ms + RMS_EPS) * g


def _sigmoid(x):
    return 1.0 / (1.0 + jnp.exp(-x))


def _gelu_tanh(x):
    c = math.sqrt(2.0 / math.pi)
    return 0.5 * x * (1.0 + jnp.tanh(c * (x + 0.044715 * (x * x * x))))


def _dot(a, b):
    return jnp.dot(a, b, preferred_element_type=F32)


def _dot_nt(a, b):
    return lax.dot_general(a, b, (((1,), (1,)), ((), ())), preferred_element_type=F32)


def _in_proj_kernel(x_ref, xs_ref, g_ref, w_ref, kvt_in_ref, q_ref, kv_ref, rest_ref, kvt_ref,
                    qs_ref, kvs_ref, rests_ref, xn_ref, *, tm, tn, nq, nkv):
    del kvt_in_ref
    i, j = pl.program_id(0), pl.program_id(1)

    @pl.when(j == 0)
    def _():
        xn_ref[:tm] = _rms(x_ref[...], g_ref[...]).astype(BF16)

    @pl.when((j == 0) & (i == 0))
    def _():
        xn_ref[tm:] = _rms(xs_ref[...], g_ref[...]).astype(BF16)

    def project(out_ref, outs_ref, col, transposed_ref=None):
        @pl.when(i == 0)
        def _():
            z = _dot(xn_ref[...], w_ref[...].astype(BF16))
            out_ref[...] = z[:tm]
            outs_ref[:, pl.ds(pl.multiple_of(col * tn, tn), tn)] = z[tm:]
            if transposed_ref is not None:
                transposed_ref[...] = z[:tm].T

        @pl.when(i > 0)
        def _():
            z = _dot(xn_ref[:tm], w_ref[...].astype(BF16))
            out_ref[...] = z
            if transposed_ref is not None:
                transposed_ref[...] = z.T

    @pl.when(j < nq)
    def _():
        project(q_ref, qs_ref, j)

    @pl.when((j >= nq) & (j < nq + nkv))
    def _():
        project(kv_ref, kvs_ref, j - nq, kvt_ref)

    @pl.when(j >= nq + nkv)
    def _():
        project(rest_ref, rests_ref, j - nq - nkv)


def _in_proj(x, xs, g, w, kvt_stack, layer, *, tm, seq, tn=512):
    m, k = x.shape
    ms = xs.shape[0]
    n = w.shape[2]
    nq, nkv = ATT_WIDTH // tn, 2 * ATT_WIDTH // tn
    nrest = n // tn - nq - nkv
    assert seq % tm == 0 and tm % LANES == 0
    assert kvt_stack.shape[1:] == (m // seq, nkv * tn, seq)
    tiles_per_seq = seq // tm
    vmem = (_nbytes((tm, k), F32)
            + 2 * (_nbytes((ms, k), F32) + _nbytes((k, tn), F32) + 4 * _nbytes((tm, tn), F32)
                   + _nbytes((ms, n), F32))
            + _nbytes((tm + ms, k), BF16) + _nbytes((k, tn), BF16) + 3 * _nbytes((tm + ms, tn), F32)
            + 2 * MIB)
    kv_col = lambda j: jnp.clip(j - nq, 0, nkv - 1)
    whole = lambda cols: pl.BlockSpec((ms, cols), lambda i, j: (0, 0))
    return pl.pallas_call(
        functools.partial(_in_proj_kernel, tm=tm, tn=tn, nq=nq, nkv=nkv),
        out_shape=(jax.ShapeDtypeStruct((m, nq * tn), F32),
                   jax.ShapeDtypeStruct((m, nkv * tn), F32),
                   jax.ShapeDtypeStruct((m, nrest * tn), F32),
                   jax.ShapeDtypeStruct(kvt_stack.shape, F32),
                   jax.ShapeDtypeStruct((ms, nq * tn), F32),
                   jax.ShapeDtypeStruct((ms, nkv * tn), F32),
                   jax.ShapeDtypeStruct((ms, nrest * tn), F32)),
        grid=(m // tm, n // tn),
        in_specs=[pl.BlockSpec((tm, k), lambda i, j: (i, 0), pipeline_mode=pl.Buffered(1)),
                  whole(k),
                  pl.BlockSpec((None, 1, k), lambda i, j: (layer, 0, 0)),
                  pl.BlockSpec((None, k, tn), lambda i, j: (layer, 0, j)),
                  pl.BlockSpec(memory_space=pl.ANY)],
        out_specs=(pl.BlockSpec((tm, tn), lambda i, j: (i, jnp.minimum(j, nq - 1))),
                   pl.BlockSpec((tm, tn), lambda i, j: (i, kv_col(j))),
                   pl.BlockSpec((tm, tn), lambda i, j: (i, jnp.clip(j - nq - nkv, 0, nrest - 1))),
                   pl.BlockSpec((None, None, tn, tm),
                                lambda i, j: (layer, i // tiles_per_seq, kv_col(j), i % tiles_per_seq)),
                   whole(nq * tn), whole(nkv * tn), whole(nrest * tn)),
        scratch_shapes=[pltpu.VMEM((tm + ms, k), BF16)],
        input_output_aliases={4: 3},
        compiler_params=_params(("arbitrary", "arbitrary"), vmem),
        name="in_proj",
    )(x, xs, g, w, kvt_stack)


def _attn_prompt_kernel(q_ref, k_ref, v_ref, o_ref, ob_ref, mb_ref, lb_ref, bias_ref):
    s_len = q_ref.shape[0]
    blk = ATT_SPAN
    lane = lax.broadcasted_iota(jnp.int32, (blk, LANES), 1)
    head0 = lane < HEAD_DIM
    scale = HEAD_DIM ** -0.5

    qi = lax.broadcasted_iota(jnp.int32, (2 * blk, 3 * blk), 0) & (blk - 1)
    kj = lax.broadcasted_iota(jnp.int32, (2 * blk, 3 * blk), 1)
    dist = jnp.where(kj < blk, qi - kj, qi + 2 * blk - kj)
    bias_ref[...] = jnp.where((dist >= 0) & (dist <= ATT_SPAN), 0.0, NEG)

    def block(p, d, q0, k0=None):
        def rows(start, n):
            return pl.ds(start, n, stride=d) if d > 1 else pl.ds(start, n)

        qt = q_ref[rows(q0, blk), :] * scale
        qq = jnp.concatenate([jnp.where(head0, qt, 0.0), jnp.where(head0, 0.0, qt)], axis=0).astype(BF16)
        if k0 is None:
            key_rows, bias = rows(q0, blk), bias_ref[:, :blk]
        else:
            key_rows, bias = rows(k0, 2 * blk), bias_ref[:, blk:]
        k = k_ref[key_rows, :].astype(BF16)
        v = v_ref[key_rows, :].astype(BF16)
        s = _dot_nt(qq, k) + bias
        m = jnp.max(s, axis=-1, keepdims=True)
        e = jnp.exp(s - m)
        l = jnp.sum(e, axis=-1, keepdims=True)
        o = _dot(e.astype(BF16), v)
        m = jnp.broadcast_to(m, (2 * blk, LANES))
        l = jnp.broadcast_to(l, (2 * blk, LANES))
        ob_ref[p, rows(q0, blk), :] = jnp.where(head0, o[:blk], o[blk:])
        mb_ref[p, rows(q0, blk), :] = jnp.where(head0, m[:blk], m[blk:])
        lb_ref[p, rows(q0, blk), :] = jnp.where(head0, l[:blk], l[blk:])

    for p, d in enumerate(ATT_DILATIONS):
        nblk = s_len // (d * blk)
        for r in range(d):
            for b in range(nblk):
                block(p, d, r + d * blk * b, None if b == 0 else r + d * blk * (b - 1))

    def merge(c, carry):
        rows = pl.ds(pl.multiple_of(c * blk, blk), blk)
        m0, m1, m2 = mb_ref[0, rows, :], mb_ref[1, rows, :], mb_ref[2, rows, :]
        m = jnp.maximum(jnp.maximum(m0, m1), m2)
        e0, e1, e2 = jnp.exp(m0 - m), jnp.exp(m1 - m), jnp.exp(m2 - m)
        num = e0 * ob_ref[0, rows, :] + e1 * ob_ref[1, rows, :] + e2 * ob_ref[2, rows, :]
        den = e0 * lb_ref[0, rows, :] + e1 * lb_ref[1, rows, :] + e2 * lb_ref[2, rows, :]
        o_ref[rows, :] = num / den
        return carry

    lax.fori_loop(0, s_len // blk, merge, 0)


def _attn_prompt(q, kv):
    b, s, _ = q.shape
    assert s % (max(ATT_DILATIONS) * ATT_SPAN) == 0
    npair = ATT_WIDTH // LANES
    blk_bytes = _nbytes((s, LANES), F32)
    vmem = 2 * 4 * blk_bytes + 3 * len(ATT_DILATIONS) * blk_bytes + 4 * MIB
    return pl.pallas_call(
        _attn_prompt_kernel,
        out_shape=jax.ShapeDtypeStruct((b, s, ATT_WIDTH), F32),
        grid=(b, npair),
        in_specs=[pl.BlockSpec((None, s, LANES), lambda i, j: (i, 0, j)),
                  pl.BlockSpec((None, s, LANES), lambda i, j: (i, 0, j)),
                  pl.BlockSpec((None, s, LANES), lambda i, j: (i, 0, npair + j))],
        out_specs=pl.BlockSpec((None, s, LANES), lambda i, j: (i, 0, j)),
        scratch_shapes=[pltpu.VMEM((len(ATT_DILATIONS), s, LANES), F32),
                        pltpu.VMEM((len(ATT_DILATIONS), s, LANES), F32),
                        pltpu.VMEM((len(ATT_DILATIONS), s, LANES), F32),
                        pltpu.VMEM((2 * ATT_SPAN, 3 * ATT_SPAN), F32)],
        compiler_params=_params(("parallel", "parallel"), vmem, _nbytes((b, s, ATT_WIDTH), F32)),
        name="attn_prompt",
    )(q, kv, kv)


def _attn_sample_kernel(q_ref, kn_ref, vn_ref, kct_ref, vct_ref, o_ref):
    t_len = q_ref.shape[0]
    w_len = kct_ref.shape[-1]
    pairs = q_ref.shape[1] // LANES
    pad = LANES
    zpad = jnp.zeros((pad - t_len, LANES), F32)

    def multiplicity(dist):
        c = jnp.zeros(dist.shape, F32)
        for d in ATT_DILATIONS:
            ok = (dist >= 0) & ((dist & (d - 1)) == 0) & (dist <= ATT_SPAN * d)
            c = c + jnp.where(ok, 1.0, 0.0)
        return c

    tc = lax.broadcasted_iota(jnp.int32, (t_len, w_len), 0)
    jc = lax.broadcasted_iota(jnp.int32, (t_len, w_len), 1)
    cnt_c = multiplicity(w_len + tc - jc)
    tn = lax.broadcasted_iota(jnp.int32, (t_len, pad), 0)
    jn = lax.broadcasted_iota(jnp.int32, (t_len, pad), 1)
    cnt_n = multiplicity(jnp.where(jn < t_len, tn - jn, -1))

    lane = lax.broadcasted_iota(jnp.int32, (t_len, LANES), 1)
    head0 = lane < HEAD_DIM
    for p in range(pairs):
        lanes = slice(p * LANES, (p + 1) * LANES)
        q = q_ref[:, lanes] * (HEAD_DIM ** -0.5)
        kn = jnp.concatenate([kn_ref[:, lanes], zpad], axis=0).astype(BF16)
        vn = jnp.concatenate([vn_ref[:, lanes], zpad], axis=0).astype(BF16)
        kct = kct_ref[2 * p:2 * p + 2].reshape(LANES, w_len).astype(BF16)
        vct = vct_ref[2 * p:2 * p + 2].reshape(LANES, w_len).astype(BF16)
        outs = []
        for h in range(2):
            hm = head0 if h == 0 else jnp.logical_not(head0)
            qh = jnp.where(hm, q, 0.0).astype(BF16)
            sc = jnp.where(cnt_c > 0, _dot(qh, kct), NEG)
            sn = jnp.where(cnt_n > 0, _dot_nt(qh, kn), NEG)
            m = jnp.maximum(jnp.max(sc, axis=-1, keepdims=True), jnp.max(sn, axis=-1, keepdims=True))
            ec = cnt_c * jnp.exp(sc - m)
            en = cnt_n * jnp.exp(sn - m)
            l = jnp.sum(ec, axis=-1, keepdims=True) + jnp.sum(en, axis=-1, keepdims=True)
            pv = _dot_nt(ec.astype(BF16), vct) + _dot(en.astype(BF16), vn)
            outs.append(pv / l)
        o_ref[:, lanes] = jnp.where(head0, outs[0], outs[1])


def _attn_sample(q, kv, cache_t, layer, *, pairs=4):
    b, t, _ = q.shape
    w = cache_t.shape[-1]
    assert all(d & (d - 1) == 0 for d in ATT_DILATIONS) and t <= LANES
    width = pairs * LANES
    nstep = ATT_WIDTH // width
    vmem = 2 * 2 * _nbytes((width, w), F32) + 10 * _nbytes((LANES, w), F32) * pairs + 4 * MIB
    new_spec = lambda off: pl.BlockSpec((None, t, width), lambda i, j: (i, 0, off + j))
    buf_spec = lambda c: pl.BlockSpec((None, None, None, 2 * pairs, HEAD_DIM, w),
                                      lambda i, j: (layer, i, c, j, 0, 0))
    return pl.pallas_call(
        _attn_sample_kernel,
        out_shape=jax.ShapeDtypeStruct((b, t, ATT_WIDTH), F32),
        grid=(b, nstep),
        in_specs=[new_spec(0), new_spec(0), new_spec(nstep), buf_spec(0), buf_spec(1)],
        out_specs=new_spec(0),
        compiler_params=_params(("parallel", "parallel"), vmem),
        name="attn_sample",
    )(q, kv, kv, cache_t, cache_t)


def _ssm_prep_kernel(are_ref, aim_ref, ldt_ref, bre_ref, bim_ref, abre_ref, abim_ref, bbre_ref, bbim_ref):
    a_re, a_im = are_ref[...], aim_ref[...]
    dt = jnp.exp(ldt_ref[...])
    mag = jnp.exp(a_re * dt)
    ang = a_im * dt
    ab_re, ab_im = mag * jnp.cos(ang), mag * jnp.sin(ang)
    den = a_re * a_re + a_im * a_im
    nr, ni = ab_re - 1.0, ab_im
    cf_re = (nr * a_re + ni * a_im) / den
    cf_im = (ni * a_re - nr * a_im) / den
    abre_ref[...] = ab_re
    abim_ref[...] = ab_im
    for c in range(SSM_GROUP):
        b_re, b_im = bre_ref[c], bim_ref[c]
        bbre_ref[c] = cf_re * b_re - cf_im * b_im
        bbim_ref[c] = cf_re * b_im + cf_im * b_re


def _ssm_prep(a_re, a_im, log_dt, b_re, b_im):
    depth, g, n = a_re.shape
    rows = depth * g
    flat = lambda a: a.reshape(rows, n)
    chan_major = lambda b: jnp.transpose(b.reshape(rows, n, SSM_GROUP), (2, 0, 1))
    ab_re, ab_im, bb_re, bb_im = pl.pallas_call(
        _ssm_prep_kernel,
        out_shape=(jax.ShapeDtypeStruct((rows, n), F32), jax.ShapeDtypeStruct((rows, n), F32),
                   jax.ShapeDtypeStruct((SSM_GROUP, rows, n), F32),
                   jax.ShapeDtypeStruct((SSM_GROUP, rows, n), F32)),
        name="ssm_prep",
    )(flat(a_re), flat(a_im), log_dt.reshape(rows, 1), chan_major(b_re), chan_major(b_im))
    unflat = lambda b: jnp.transpose(b, (1, 0, 2)).reshape(depth, g, SSM_GROUP, n)
    return ab_re.reshape(depth, g, n), ab_im.reshape(depth, g, n), unflat(bb_re), unflat(bb_im)


def _block_diag(m):
    *lead, g, r, c = m.shape
    nb = g // SSM_BLOCK_GROUPS
    m = m.reshape(*lead, nb, SSM_BLOCK_GROUPS, r, c)
    eye = jnp.eye(SSM_BLOCK_GROUPS, dtype=m.dtype)
    out = m[..., :, :, None, :] * eye[:, None, :, None]
    return out.reshape(*lead, nb, SSM_BLOCK_GROUPS * r, SSM_BLOCK_GROUPS * c)


def _ssm_kernel(u_ref, h0re_ref, h0im_ref, are_ref, aim_ref, bb_ref, cre_ref, cim_ref, d_ref,
                y_ref, hre_ref, him_ref, x_scr, up_scr, *, steps, chained):
    ns = SSM_BLOCK_STATE
    nrows = steps * SCAN_SEQS
    chunk = min(nrows, 256)

    nchunks = nrows // chunk
    chunk_steps = chunk // SCAN_SEQS
    unroll = min(steps, 8)

    def permute_in(t, c):
        up_scr[pl.ds(pl.multiple_of(t * SCAN_SEQS, SCAN_SEQS), SCAN_SEQS), :] = (
            u_ref[pl.ds(t, SCAN_SEQS, stride=steps), :])
        return c
    lax.fori_loop(0, steps, permute_in, 0, unroll=unroll)

    a_re = jnp.broadcast_to(are_ref[...], (SCAN_SEQS, ns))
    a_im = jnp.broadcast_to(aim_ref[...], (SCAN_SEQS, ns))

    def input_map(c):
        rows = pl.ds(c * chunk, chunk)
        x_scr[rows, :] = _dot(up_scr[rows, :].astype(BF16), bb_ref[...])

    def scan_chunk(c, h, store):
        h_re, h_im = h
        for t in range(c * chunk_steps, (c + 1) * chunk_steps):
            rows = pl.ds(t * SCAN_SEQS, SCAN_SEQS)
            h_re, h_im = (a_re * h_re - a_im * h_im + x_scr[rows, :ns],
                          a_re * h_im + a_im * h_re + x_scr[rows, ns:])
            if store:
                x_scr[rows, :ns] = h_re
                x_scr[rows, ns:] = h_im
        return h_re, h_im

    def output_map(c):
        rows = pl.ds(c * chunk, chunk)
        y = (_dot(x_scr[rows, :ns].astype(BF16), cre_ref[...])
             - _dot(x_scr[rows, ns:].astype(BF16), cim_ref[...])
             + d_ref[...] * up_scr[rows, :])
        up_scr[rows, :] = _gelu_tanh(y)

    init = (h0re_ref[...], h0im_ref[...])
    input_map(0)
    if chained:
        assert steps & (steps - 1) == 0
        e_re, e_im = init
        for c in range(nchunks):
            if c + 1 < nchunks:
                input_map(c + 1)
            e_re, e_im = scan_chunk(c, (e_re, e_im), store=False)
        p_re, p_im = a_re, a_im
        for _ in range(steps.bit_length() - 1):
            p_re, p_im = p_re * p_re - p_im * p_im, 2.0 * p_re * p_im
        sub = lax.broadcasted_iota(jnp.int32, (SCAN_SEQS, ns), 0)
        shift = 1
        while shift < SCAN_SEQS:
            r_re = pltpu.roll(e_re, shift, axis=0)
            r_im = pltpu.roll(e_im, shift, axis=0)
            keep = sub >= shift
            e_re, e_im = (e_re + jnp.where(keep, p_re * r_re - p_im * r_im, 0.0),
                          e_im + jnp.where(keep, p_re * r_im + p_im * r_re, 0.0))
            p_re, p_im = p_re * p_re - p_im * p_im, 2.0 * p_re * p_im
            shift *= 2
        first = sub == 0
        init = (init[0] + jnp.where(first, 0.0, pltpu.roll(e_re, 1, axis=0)),
                init[1] + jnp.where(first, 0.0, pltpu.roll(e_im, 1, axis=0)))
    else:
        for c in range(1, nchunks):
            input_map(c)
    h = init
    for c in range(nchunks):
        h = scan_chunk(c, h, store=True)
        if c > 0:
            output_map(c - 1)
    output_map(nchunks - 1)
    hre_ref[...] = h[0]
    him_ref[...] = h[1]

    def permute_out(t, c):
        y_ref[pl.ds(t, SCAN_SEQS, stride=steps), :] = (
            up_scr[pl.ds(pl.multiple_of(t * SCAN_SEQS, SCAN_SEQS), SCAN_SEQS), :])
        return c
    lax.fori_loop(0, steps, permute_out, 0, unroll=unroll)


def _ssm(u, h0_re, h0_im, h0_layer, ssm_w, layer, *, chained):
    nb, rows, _ = u.shape
    steps = rows // SCAN_SEQS
    nblk = SSM_WIDTH // LANES
    ns = SSM_BLOCK_STATE
    vmem = (2 * 2 * _nbytes((rows, LANES), F32) + _nbytes((rows, 2 * ns), F32)
            + _nbytes((rows, LANES), F32) + 4 * _nbytes((min(rows, 256), 2 * ns), F32) + 8 * MIB)
    h0_spec = pl.BlockSpec((None, None, SCAN_SEQS, ns), lambda i, j: (h0_layer, i, 0, j))
    state_spec = pl.BlockSpec((None, SCAN_SEQS, ns), lambda i, j: (i, 0, j))
    return pl.pallas_call(
        functools.partial(_ssm_kernel, steps=steps, chained=chained),
        out_shape=(jax.ShapeDtypeStruct((nb, rows, SSM_WIDTH), F32),
                   jax.ShapeDtypeStruct(h0_re.shape[1:], F32),
                   jax.ShapeDtypeStruct(h0_im.shape[1:], F32)),
        grid=(nb, nblk),
        in_specs=[pl.BlockSpec((None, rows, LANES), lambda i, j: (i, 0, j)),
                  h0_spec, h0_spec,
                  pl.BlockSpec((None, 1, ns), lambda i, j: (layer, 0, j)),
                  pl.BlockSpec((None, 1, ns), lambda i, j: (layer, 0, j)),
                  pl.BlockSpec((None, None, LANES, 2 * ns), lambda i, j: (layer, j, 0, 0)),
                  pl.BlockSpec((None, None, ns, LANES), lambda i, j: (layer, j, 0, 0)),
                  pl.BlockSpec((None, None, ns, LANES), lambda i, j: (layer, j, 0, 0)),
                  pl.BlockSpec((None, 1, LANES), lambda i, j: (layer, 0, j))],
        out_specs=(pl.BlockSpec((None, rows, LANES), lambda i, j: (i, 0, j)), state_spec, state_spec),
        scratch_shapes=[pltpu.VMEM((rows, 2 * ns), F32), pltpu.VMEM((rows, LANES), F32)],
        compiler_params=_params(("parallel", "parallel"), vmem,
                                _nbytes((nb, rows, SSM_WIDTH), F32)),
        name="ssm_chained" if chained else "ssm_step",
    )(u, h0_re, h0_im, *ssm_w)


def _conv_kernel(h_ref, gb_ref, gc_ref, w_ref, pre_ref, y_ref, st_ref):
    n = h_ref.shape[0]
    u = gc_ref[...] * h_ref[...]
    row = lax.broadcasted_iota(jnp.int32, u.shape, 0)
    pre = pre_ref[...]
    p0, p1 = pre[0:1, :], pre[1:2, :]
    u1 = jnp.where(row == 0, p1, pltpu.roll(u, 1, axis=0))
    u2 = jnp.where(row == 0, p0, jnp.where(row == 1, p1, pltpu.roll(u, 2, axis=0)))
    w = w_ref[...]
    y = w[0:1, :] * u2 + w[1:2, :] * u1 + w[2:3, :] * u
    y_ref[...] = gb_ref[...] * y
    st_ref[...] = u[n - (CONV_K - 1):, :]


def _conv(rest, w, layer, prefix, prefix_layer):
    b, n, _ = rest.shape
    assert n >= CONV_K - 1
    nblk = CONV_CH // LANES
    off = SSM_WIDTH // LANES
    col = lambda part: pl.BlockSpec((None, n, LANES), lambda i, j: (i, 0, off + part * nblk + j))
    vmem = 2 * 4 * _nbytes((n, LANES), F32) + 8 * _nbytes((n, LANES), F32) + 4 * MIB
    return pl.pallas_call(
        _conv_kernel,
        out_shape=(jax.ShapeDtypeStruct((b, n, CONV_CH), F32),
                   jax.ShapeDtypeStruct((b, CONV_K - 1, CONV_CH), F32)),
        grid=(b, nblk),
        in_specs=[col(0), col(1), col(2),
                  pl.BlockSpec((None, CONV_K, LANES), lambda i, j: (layer, 0, j)),
                  pl.BlockSpec((None, None, CONV_K - 1, LANES), lambda i, j: (prefix_layer, i, 0, j))],
        out_specs=(pl.BlockSpec((None, n, LANES), lambda i, j: (i, 0, j)),
                   pl.BlockSpec((None, CONV_K - 1, LANES), lambda i, j: (i, 0, j))),
        compiler_params=_params(("parallel", "parallel"), vmem,
                                _nbytes((b, n, CONV_CH), F32)),
        name="conv",
    )(rest, rest, rest, w, prefix)


def _out_proj_kernel(oatt_ref, yssm_ref, yconv_ref, x_ref, oatts_ref, yssms_ref, yconvs_ref, xs_ref,
                     g_ref, wglu_ref, bglu_ref, w_ref, o_ref, os_ref, mix_ref, *, tm, tn):
    i, j = pl.program_id(0), pl.program_id(1)
    a0, a1 = ATT_WIDTH, ATT_WIDTH + SSM_WIDTH

    def mixed(rows, o_att, y_ssm, y_conv):
        g = g_ref[...]
        mix_ref[rows, :a0] = _rms(o_att, g[:, :a0]).astype(BF16)
        y_ssm = y_ssm * _sigmoid(_dot(y_ssm.astype(BF16), wglu_ref[...].astype(BF16)) + bglu_ref[...])
        mix_ref[rows, a0:a1] = _rms(y_ssm, g[:, a0:a1]).astype(BF16)
        mix_ref[rows, a1:] = _rms(y_conv, g[:, a1:]).astype(BF16)

    @pl.when(j == 0)
    def _():
        mixed(slice(0, tm), oatt_ref[...], yssm_ref[...], yconv_ref[...])

    @pl.when((j == 0) & (i == 0))
    def _():
        mixed(slice(tm, None), oatts_ref[...], yssms_ref[...], yconvs_ref[...])

    @pl.when(i == 0)
    def _():
        z = _dot(mix_ref[...], w_ref[...].astype(BF16))
        o_ref[...] = x_ref[...] + z[:tm]
        cols = pl.ds(pl.multiple_of(j * tn, tn), tn)
        os_ref[:, cols] = xs_ref[:, cols] + z[tm:]

    @pl.when(i > 0)
    def _():
        o_ref[...] = x_ref[...] + _dot(mix_ref[:tm], w_ref[...].astype(BF16))


def _out_proj(prompt, sample, g, w_glu, b_glu, w, layer, *, tm, tn=512):
    m, d = prompt[3].shape
    ms = sample[3].shape[0]
    k = w.shape[1]
    vmem = (2 * (_nbytes((tm, k), F32) + 2 * _nbytes((tm, tn), F32) + 3 * _nbytes((ms, k), F32)
                 + _nbytes((k, tn), F32) + _nbytes((SSM_WIDTH, SSM_WIDTH), F32))
            + _nbytes((tm + ms, k), BF16) + _nbytes((k, tn), BF16) + 2 * _nbytes((tm + ms, k), F32)
            + 2 * MIB)
    row = lambda c: pl.BlockSpec(
        (tm, c), lambda i, j: (jnp.minimum(jnp.where(j > 0, i + 1, i), m // tm - 1), 0))
    whole = lambda c: pl.BlockSpec((ms, c), lambda i, j: (0, 0))
    layered = lambda r, c: pl.BlockSpec((None, r, c), lambda i, j: (layer, 0, 0))
    return pl.pallas_call(
        functools.partial(_out_proj_kernel, tm=tm, tn=tn),
        out_shape=(jax.ShapeDtypeStruct((m, d), F32), jax.ShapeDtypeStruct((ms, d), F32)),
        grid=(m // tm, d // tn),
        in_specs=[row(ATT_WIDTH), row(SSM_WIDTH), row(CONV_CH),
                  pl.BlockSpec((tm, tn), lambda i, j: (i, j)),
                  whole(ATT_WIDTH), whole(SSM_WIDTH), whole(CONV_CH), whole(d),
                  layered(1, k),
                  layered(SSM_WIDTH, SSM_WIDTH),
                  layered(1, SSM_WIDTH),
                  pl.BlockSpec((None, k, tn), lambda i, j: (layer, 0, j))],
        out_specs=(pl.BlockSpec((tm, tn), lambda i, j: (i, j)), whole(d)),
        scratch_shapes=[pltpu.VMEM((tm + ms, k), BF16)],
        compiler_params=_params(("arbitrary", "arbitrary"), vmem),
        name="out_proj",
    )(*prompt, *sample, g, w_glu, b_glu, w)


def _ffn_kernel(x_ref, xs_ref, g_ref, gfin_ref, wg_ref, wu_ref, wd_ref, o_ref, os_ref, xn_ref, *,
                tm, final):
    i, f = pl.program_id(0), pl.program_id(1)
    last = f == pl.num_programs(1) - 1

    @pl.when(f == 0)
    def _():
        x = x_ref[...]
        xn_ref[:tm] = _rms(x, g_ref[...]).astype(BF16)
        o_ref[...] = x

    @pl.when((f == 0) & (i == 0))
    def _():
        xs = xs_ref[...]
        xn_ref[tm:] = _rms(xs, g_ref[...]).astype(BF16)
        os_ref[...] = xs

    def swiglu(xn):
        gate = _dot(xn, wg_ref[...].astype(BF16))
        up = _dot(xn, wu_ref[...].astype(BF16))
        return _dot((gate * _sigmoid(gate) * up).astype(BF16), wd_ref[...].astype(BF16))

    @pl.when(i == 0)
    def _():
        y = swiglu(xn_ref[...])
        o_ref[...] += y[:tm]
        os_ref[...] += y[tm:]

    @pl.when(i > 0)
    def _():
        o_ref[...] += swiglu(xn_ref[:tm])

    if final:
        @pl.when(last)
        def _():
            o_ref[...] = _rms(o_ref[...], gfin_ref[...])

        @pl.when(last & (i == 0))
        def _():
            os_ref[...] = _rms(os_ref[...], gfin_ref[...])


def _ffn(x, xs, g, w_gate, w_up, w_down, layer, *, tm, g_final=None, tf=512):
    final = g_final is not None
    m, d = x.shape
    g_last = g_final.reshape(1, d) if final else g[layer]
    ms = xs.shape[0]
    ff = w_gate.shape[2]
    vmem = (_nbytes((tm, d), F32)
            + 2 * (_nbytes((tm, d), F32) + 2 * _nbytes((ms, d), F32) + 3 * _nbytes((d, tf), F32))
            + _nbytes((tm + ms, d), BF16) + 2 * _nbytes((tm + ms, tf), F32) + 2 * MIB)
    return pl.pallas_call(
        functools.partial(_ffn_kernel, tm=tm, final=final),
        out_shape=(jax.ShapeDtypeStruct((m, d), F32), jax.ShapeDtypeStruct((ms, d), F32)),
        grid=(m // tm, ff // tf),
        in_specs=[pl.BlockSpec((tm, d), lambda i, f: (i, 0), pipeline_mode=pl.Buffered(1)),
                  pl.BlockSpec((ms, d), lambda i, f: (0, 0)),
                  pl.BlockSpec((None, 1, d), lambda i, f: (layer, 0, 0)),
                  pl.BlockSpec((1, d), lambda i, f: (0, 0)),
                  pl.BlockSpec((None, d, tf), lambda i, f: (layer, 0, f)),
                  pl.BlockSpec((None, d, tf), lambda i, f: (layer, 0, f)),
                  pl.BlockSpec((None, tf, d), lambda i, f: (layer, f, 0))],
        out_specs=(pl.BlockSpec((tm, d), lambda i, f: (i, 0)),
                   pl.BlockSpec((ms, d), lambda i, f: (0, 0))),
        scratch_shapes=[pltpu.VMEM((tm + ms, d), BF16)],
        compiler_params=_params(("arbitrary", "arbitrary"), vmem),
        name="ffn",
    )(x, xs, g, g_last, w_gate, w_up, w_down)


def _mixers(q, kv, rest, ssm_w, conv_w, layer, *, batch, cache_t, conv_prefix, ssm_h0, state_layer):
    m = q.shape[0]
    n = m // batch
    q3, kv3, rest3 = (a.reshape(batch, n, a.shape[-1]) for a in (q, kv, rest))
    if cache_t is None:
        o_att = _attn_prompt(q3, kv3)
        y_ssm, h_re, h_im = _ssm(rest3, *ssm_h0, state_layer, ssm_w, layer, chained=True)
        h_re, h_im = h_re[:, SCAN_SEQS - 1], h_im[:, SCAN_SEQS - 1]
    else:
        o_att = _attn_sample(q3, kv3, cache_t, layer)
        y_ssm, h_re, h_im = _ssm(rest.reshape(1, m, rest.shape[-1]), *ssm_h0, state_layer, ssm_w, layer,
                                 chained=False)
        h_re, h_im = h_re[0], h_im[0]
    y_conv, conv_state = _conv(rest3, conv_w, layer, conv_prefix, state_layer)
    mixed = (o_att.reshape(m, ATT_WIDTH), y_ssm.reshape(m, SSM_WIDTH), y_conv.reshape(m, CONV_CH))
    return mixed, conv_state, h_re, h_im


def kernel(x_prompt, x_sample, cache_kv, state_conv, state_ssm_re, state_ssm_im, g_mix_norm, w_in,
           ssm_a_re, ssm_a_im, ssm_log_dt, ssm_b_re, ssm_b_im, ssm_c_re, ssm_c_im, ssm_d, ssm_w_glu,
           ssm_b_glu, conv_w, g_out, w_out, g_ffn_norm, w_gate, w_up, w_down, g_final):
    batch, seq, d_model = x_prompt.shape
    dec_batch, dec_seq, _ = x_sample.shape
    depth, n_groups, n_state = ssm_a_re.shape
    n_heads = ATT_WIDTH // HEAD_DIM
    assert dec_batch == SCAN_SEQS and seq % SCAN_SEQS == 0

    ab_re, ab_im, bb_re, bb_im = _ssm_prep(ssm_a_re, ssm_a_im, ssm_log_dt, ssm_b_re, ssm_b_im)
    cache_t = jnp.transpose(cache_kv, (0, 1, 3, 4, 5, 2))

    row = lambda a: a.reshape(depth, 1, -1)
    ssm_w = (row(ab_re), row(ab_im),
             jnp.concatenate([_block_diag(bb_re), _block_diag(bb_im)], axis=-1).astype(BF16),
             _block_diag(jnp.swapaxes(ssm_c_re, -1, -2)).astype(BF16),
             _block_diag(jnp.swapaxes(ssm_c_im, -1, -2)).astype(BF16),
             row(ssm_d))
    g_mix, g_mixed, g_ffn, b_glu = row(g_mix_norm), row(g_out), row(g_ffn_norm), row(ssm_b_glu)
    state_ssm = (state_ssm_re.reshape(depth, 1, dec_batch, -1), state_ssm_im.reshape(depth, 1, dec_batch, -1))

    xp = x_prompt.reshape(batch * seq, d_model)
    xs = x_sample.reshape(dec_batch * dec_seq, d_model)
    zero_state = jnp.zeros((1, batch, SCAN_SEQS, n_groups * n_state), F32)
    zero_prefix = jnp.zeros((1, batch, CONV_K - 1, CONV_CH), F32)
    kvt = jnp.zeros((depth, batch, 2 * ATT_WIDTH, seq), F32)

    outs_p, outs_s = [], []
    for i in range(depth):
        q, kv, rest, kvt, q_s, kv_s, rest_s = _in_proj(xp, xs, g_mix, w_in, kvt, i, tm=2048, seq=seq, tn=256)

        mixed_p, c_state, h_re, h_im = _mixers(
            q, kv, rest, ssm_w, conv_w, i, batch=batch, cache_t=None, conv_prefix=zero_prefix,
            ssm_h0=(zero_state, zero_state), state_layer=0)
        outs_p.append((c_state, h_re.reshape(batch, n_groups, n_state),
                       h_im.reshape(batch, n_groups, n_state)))

        mixed_s, c_state, h_re, h_im = _mixers(
            q_s, kv_s, rest_s, ssm_w, conv_w, i, batch=dec_batch, cache_t=cache_t, conv_prefix=state_conv,
            ssm_h0=state_ssm, state_layer=i)
        outs_s.append((kv_s.reshape(dec_batch, dec_seq, 2, n_heads, HEAD_DIM), c_state,
                       h_re.reshape(dec_batch, n_groups, n_state),
                       h_im.reshape(dec_batch, n_groups, n_state)))

        xp, xs = _out_proj((*mixed_p, xp), (*mixed_s, xs), g_mixed, ssm_w_glu, b_glu, w_out, i, tm=2048,
                           tn=256)
        xp, xs = _ffn(xp, xs, g_ffn, w_gate, w_up, w_down, i, tm=1024,
                      g_final=g_final if i == depth - 1 else None)

    y_prompt = xp.reshape(batch, seq, d_model)
    y_sample = xs.reshape(dec_batch, dec_seq, d_model)
    kv_prompt = jnp.transpose(kvt.reshape(depth, batch, 2, n_heads, HEAD_DIM, seq), (0, 1, 5, 2, 3, 4))
    n_keep = min(max(ATT_DILATIONS) * ATT_SPAN, seq)
    stack = lambda outs, k: jnp.stack([o[k] for o in outs])
    return (y_prompt, y_sample,
            kv_prompt[:, :, seq - n_keep:], stack(outs_p, 0), stack(outs_p, 1), stack(outs_p, 2),
            stack(outs_s, 0), stack(outs_s, 1), stack(outs_s, 2), stack(outs_s, 3))
```
